```python
import math
import jax, jax.numpy as jnp
from jax import lax
import numpy as np

D_MODEL = 2048
BATCH = 8
SEQ = 2048
DEPTH = 4

N_MIXERS = 3
EPS = 1e-6
BLOCK = 128
NEG_INF = -1e30

N_BUCKETS = 32
MAX_DISTANCE = 128
N_BIAS_HEADS = 32

MEM_LEN = 256
MEM_HEADS = 4
MEM_HEAD_DIM = 256
MEM_WIDTH = MEM_HEADS * MEM_HEAD_DIM

SELF_WIDTH = D_MODEL
BRANCH_WIDTH = SELF_WIDTH + MEM_WIDTH

A_HEADS = 16
A_Q_LORA = 1536
A_KV_LORA = 512
A_NOPE = 128
A_ROPE = 64
A_V = 128
ROPE_THETA = 10000.0
A_IN = A_Q_LORA + A_KV_LORA + A_ROPE + MEM_WIDTH + BRANCH_WIDTH

B_HEADS = N_BIAS_HEADS
B_KV_HEADS = 4
B_HEAD_DIM = 64
IDX_HEADS = 16
IDX_DIM = 64
IDX_TOPK_MAX = 256
B_IN = (B_HEADS * B_HEAD_DIM + 2 * B_KV_HEADS * B_HEAD_DIM + IDX_HEADS * IDX_DIM
        + IDX_DIM + IDX_HEADS + MEM_WIDTH + BRANCH_WIDTH)

C_HEADS = N_BIAS_HEADS
C_KV_HEADS = 4
C_HEAD_DIM = 64
WINDOW = 128
C_IN = C_HEADS * C_HEAD_DIM + 2 * C_KV_HEADS * C_HEAD_DIM + MEM_WIDTH + BRANCH_WIDTH

N_A = (DEPTH + 2) // 3
N_B = (DEPTH + 1) // 3
N_C = DEPTH // 3

kernel_name = "hybrid_mla_dsa_swa_interleaved"


def _rmsnorm(x, g):
    xf = x.astype(jnp.float32)
    y = xf * lax.rsqrt(jnp.mean(xf * xf, axis=-1, keepdims=True) + EPS)
    return (y * g.astype(jnp.float32)).astype(x.dtype)


def _split(h, sizes):
    offs = [int(v) for v in np.cumsum(sizes)[:-1]]
    return jnp.split(h, offs, axis=-1)


def _rope(x):
    S, d = x.shape[1], x.shape[-1]
    inv = 1.0 / (ROPE_THETA ** (jnp.arange(0, d, 2, dtype=jnp.float32) / d))
    ang = jnp.arange(S, dtype=jnp.float32)[:, None] * inv[None, :]
    if x.ndim == 4:
        ang = ang[:, None, :]
    cos, sin = jnp.cos(ang), jnp.sin(ang)
    xf = x.astype(jnp.float32)
    x1, x2 = xf[..., : d // 2], xf[..., d // 2:]
    return jnp.concatenate([x1 * cos - x2 * sin, x2 * cos + x1 * sin], axis=-1).astype(x.dtype)


def _t5_bucket(rel):
    n = jnp.maximum(rel, 0)
    max_exact = N_BUCKETS // 2
    nf = jnp.maximum(n, 1).astype(jnp.float32)
    large = max_exact + (jnp.log(nf / max_exact) / math.log(MAX_DISTANCE / max_exact)
                         * (N_BUCKETS - max_exact)).astype(jnp.int32)
    large = jnp.minimum(large, N_BUCKETS - 1)
    return jnp.where(n < max_exact, n, large)


def _to_blocks(t):
    B, S = t.shape[0], t.shape[1]
    return t.reshape((B, S // BLOCK, BLOCK) + t.shape[2:]).swapaxes(0, 1)


def _from_blocks(o):
    nb, B = o.shape[0], o.shape[1]
    return o.swapaxes(0, 1).reshape((B, nb * BLOCK) + o.shape[3:])


def _mla_branch(h, w_in, q_norm, w_uq, kv_norm, w_ukv):
    B, S, _ = h.shape
    nb = S // BLOCK
    c_q, c_kv, k_rope, mq, gate = _split(h @ w_in, [A_Q_LORA, A_KV_LORA, A_ROPE, MEM_WIDTH, BRANCH_WIDTH])
    q = (_rmsnorm(c_q, q_norm) @ w_uq).reshape(B, S, A_HEADS, A_NOPE + A_ROPE)
    kv = (_rmsnorm(c_kv, kv_norm) @ w_ukv).reshape(B, S, A_HEADS, A_NOPE + A_V)
    q_nope, q_rope = q[..., :A_NOPE], _rope(q[..., A_NOPE:])
    k_nope, v = kv[..., :A_NOPE], kv[..., A_NOPE:]
    k_rope = _rope(k_rope)
    scale = (A_NOPE + A_ROPE) ** -0.5
    key_idx = jnp.arange(S)

    def block(args):
        qn, qr, start = args
        s = (jnp.einsum('bqhd,bkhd->bhqk', qn, k_nope, preferred_element_type=jnp.float32)
             + jnp.einsum('bqhd,bkd->bhqk', qr, k_rope, preferred_element_type=jnp.float32)) * scale
        causal = (start + jnp.arange(BLOCK))[:, None] >= key_idx[None, :]
        p = jax.nn.softmax(jnp.where(causal, s, NEG_INF), axis=-1)
        return jnp.einsum('bhqk,bkhd->bqhd', p.astype(v.dtype), v)

    o = lax.map(block, (_to_blocks(q_nope), _to_blocks(q_rope), jnp.arange(nb) * BLOCK))
    return _from_blocks(o).reshape(B, S, A_HEADS * A_V), mq, gate


def _dsa_branch(h, w_in, rel_bias):
    B, S, _ = h.shape
    nb = S // BLOCK
    k_top = min(IDX_TOPK_MAX, S // 4)
    G = B_HEADS // B_KV_HEADS
    q, k, v, iq, ik, iw, mq, gate = _split(h @ w_in, [
        B_HEADS * B_HEAD_DIM, B_KV_HEADS * B_HEAD_DIM, B_KV_HEADS * B_HEAD_DIM,
        IDX_HEADS * IDX_DIM, IDX_DIM, IDX_HEADS, MEM_WIDTH, BRANCH_WIDTH])
    q = q.reshape(B, S, B_KV_HEADS, G, B_HEAD_DIM)
    k = k.reshape(B, S, B_KV_HEADS, B_HEAD_DIM)
    v = v.reshape(B, S, B_KV_HEADS, B_HEAD_DIM)
    iq = iq.reshape(B, S, IDX_HEADS, IDX_DIM)
    key_idx = jnp.arange(S)
    gather = jax.vmap(lambda t, i: t[i])

    def block(args):
        qb, iqb, iwb, start = args
        qi = start + jnp.arange(BLOCK)
        dots = jnp.einsum('bqhd,bkd->bqhk', iqb, ik, preferred_element_type=jnp.float32) * IDX_DIM ** -0.5
        score = jnp.einsum('bqhk,bqh->bqk', jax.nn.relu(dots), iwb.astype(jnp.float32)) * IDX_HEADS ** -0.5
        score = jnp.where(qi[:, None] >= key_idx[None, :], score, NEG_INF)
        _, sel = lax.top_k(score, k_top)
        kg = gather(k, sel)
        vg = gather(v, sel)
        rel = qi[None, :, None] - sel
        bias = rel_bias[_t5_bucket(rel)].astype(jnp.float32)
        bias = bias.reshape(B, BLOCK, k_top, B_KV_HEADS, G).transpose(0, 1, 3, 4, 2)
        s = jnp.einsum('bqgjd,bqkgd->bqgjk', qb, kg, preferred_element_type=jnp.float32) * B_HEAD_DIM ** -0.5 + bias
        s = jnp.where((rel >= 0)[:, :, None, None, :], s, NEG_INF)
        p = jax.nn.softmax(s, axis=-1)
        return jnp.einsum('bqgjk,bqkgd->bqgjd', p.astype(vg.dtype), vg)

    o = lax.map(block, (_to_blocks(q), _to_blocks(iq), _to_blocks(iw), jnp.arange(nb) * BLOCK))
    return _from_blocks(o).reshape(B, S, B_HEADS * B_HEAD_DIM), mq, gate


def _swa_branch(h, w_in, sinks, rel_bias):
    B, S, _ = h.shape
    nb = S // BLOCK
    G = C_HEADS // C_KV_HEADS
    q, k, v, mq, gate = _split(h @ w_in, [
        C_HEADS * C_HEAD_DIM, C_KV_HEADS * C_HEAD_DIM, C_KV_HEADS * C_HEAD_DIM, MEM_WIDTH, BRANCH_WIDTH])
    q = q.reshape(B, S, C_KV_HEADS, G, C_HEAD_DIM)

    def band(t):
        tb = t.reshape(B, nb, BLOCK, C_KV_HEADS, C_HEAD_DIM)
        prev = jnp.concatenate([jnp.zeros_like(tb[:, :1]), tb[:, :-1]], axis=1)
        return jnp.concatenate([prev, tb], axis=2).swapaxes(0, 1)

    qi = jnp.arange(BLOCK)
    kj = jnp.arange(2 * BLOCK)
    rel = qi[:, None] + BLOCK - kj[None, :]
    in_window = (rel >= 0) & (rel < WINDOW)
    bias = rel_bias[_t5_bucket(rel)].astype(jnp.float32)
    bias = bias.reshape(BLOCK, 2 * BLOCK, C_KV_HEADS, G).transpose(2, 3, 0, 1)
    sink = sinks.astype(jnp.float32).reshape(C_KV_HEADS, G)[None, :, :, None, None]
    scale = C_HEAD_DIM ** -0.5

    def block(args):
        qb, kb, vb, start = args
        valid = in_window & ((start - BLOCK + kj) >= 0)[None, :]
        s = jnp.einsum('bqgjd,bkgd->bgjqk', qb, kb, preferred_element_type=jnp.float32) * scale + bias
        s = jnp.where(valid, s, NEG_INF)
        m = jnp.maximum(jnp.max(s, axis=-1, keepdims=True), sink)
        e = jnp.exp(s - m)
        p = e / (jnp.sum(e, axis=-1, keepdims=True) + jnp.exp(sink - m))
        return jnp.einsum('bgjqk,bkgd->bqgjd', p.astype(vb.dtype), vb)

    o = lax.map(block, (_to_blocks(q), band(k), band(v), jnp.arange(nb) * BLOCK))
    return _from_blocks(o).reshape(B, S, C_HEADS * C_HEAD_DIM), mq, gate


def _memory_attention(mq, mem_kv):
    B, S, _ = mq.shape
    mk, mv = jnp.split(mem_kv, 2, axis=-1)
    mq = mq.reshape(B, S, MEM_HEADS, MEM_HEAD_DIM)
    mk = mk.reshape(B, -1, MEM_HEADS, MEM_HEAD_DIM)
    mv = mv.reshape(B, -1, MEM_HEADS, MEM_HEAD_DIM)
    s = jnp.einsum('bqhd,bkhd->bhqk', mq, mk, preferred_element_type=jnp.float32) * MEM_HEAD_DIM ** -0.5
    p = jax.nn.softmax(s, axis=-1)
    return jnp.einsum('bhqk,bkhd->bqhd', p.astype(mv.dtype), mv).reshape(B, S, MEM_WIDTH)


def setup_inputs(seed: int = 0) -> dict:
    key = jax.random.key(seed)
    ks = jax.random.split(key, 16)
    nrm = jax.random.normal
    f32 = jnp.float32
    return {
        "x": nrm(ks[0], (BATCH, SEQ, D_MODEL), f32),
        "mem": nrm(ks[1], (BATCH, MEM_LEN, D_MODEL), f32),
        "norm_in": 1.0 + 0.02 * nrm(ks[2], (DEPTH, D_MODEL), f32),
        "final_norm": 1.0 + 0.02 * nrm(ks[3], (D_MODEL,), f32),
        "mem_norm": 1.0 + 0.02 * nrm(ks[4], (D_MODEL,), f32),
        "rel_bias": 0.2 * nrm(ks[5], (N_BUCKETS, N_BIAS_HEADS), f32),
        "w_in_a": nrm(ks[6], (N_A, D_MODEL, A_IN), f32) * D_MODEL ** -0.5,
        "a_q_norm": 1.0 + 0.02 * nrm(ks[7], (N_A, A_Q_LORA), f32),
        "w_uq": nrm(ks[8], (N_A, A_Q_LORA, A_HEADS * (A_NOPE + A_ROPE)), f32) * A_Q_LORA ** -0.5,
        "a_kv_norm": 1.0 + 0.02 * nrm(ks[9], (N_A, A_KV_LORA), f32),
        "w_ukv": nrm(ks[10], (N_A, A_KV_LORA, A_HEADS * (A_NOPE + A_V)), f32) * A_KV_LORA ** -0.5,
        "w_in_b": nrm(ks[11], (N_B, D_MODEL, B_IN), f32) * D_MODEL ** -0.5,
        "w_in_c": nrm(ks[12], (N_C, D_MODEL, C_IN), f32) * D_MODEL ** -0.5,
        "c_sinks": nrm(ks[13], (N_C, C_HEADS), f32),
        "w_mem_kv": nrm(ks[14], (DEPTH, D_MODEL, 2 * MEM_WIDTH), f32) * D_MODEL ** -0.5,
        "w_out": nrm(ks[15], (DEPTH, BRANCH_WIDTH, D_MODEL), f32) * BRANCH_WIDTH ** -0.5,
    }


def reference(x, mem, norm_in, final_norm, mem_norm, rel_bias, w_in_a, a_q_norm, w_uq,
              a_kv_norm, w_ukv, w_in_b, w_in_c, c_sinks, w_mem_kv, w_out):
    mem_n = _rmsnorm(mem, mem_norm)
    for i in range(DEPTH):
        h = _rmsnorm(x, norm_in[i])
        kind, j = i % N_MIXERS, i // N_MIXERS
        if kind == 0:
            self_out, mq, gate = _mla_branch(h, w_in_a[j], a_q_norm[j], w_uq[j], a_kv_norm[j], w_ukv[j])
        elif kind == 1:
            self_out, mq, gate = _dsa_branch(h, w_in_b[j], rel_bias)
        else:
            self_out, mq, gate = _swa_branch(h, w_in_c[j], c_sinks[j], rel_bias)
        mem_out = _memory_attention(mq, mem_n @ w_mem_kv[i])
        y = jnp.concatenate([self_out, mem_out], axis=-1) * jax.nn.silu(gate)
        x = x + y @ w_out[i]
    return _rmsnorm(x, final_norm)
```

```python
import functools
import math

import jax
import jax.numpy as jnp
from jax import lax
from jax.experimental import pallas as pl
from jax.experimental.pallas import tpu as pltpu

F32 = jnp.float32
BF16 = jnp.bfloat16

EPS = 1e-6
NEG_INF = -1e30
N_MIXERS = 3

N_BUCKETS = 32
MAX_DISTANCE = 128
N_BIAS_HEADS = 32

MEM_HEADS = 4
MEM_HEAD_DIM = 256
MEM_WIDTH = MEM_HEADS * MEM_HEAD_DIM

A_HEADS = 16
A_Q_LORA = 1536
A_KV_LORA = 512
A_NOPE = 128
A_ROPE = 64
A_V = 128
ROPE_THETA = 10000.0

B_HEADS = 32
B_KV_HEADS = 4
B_HEAD_DIM = 64
IDX_HEADS = 16
IDX_DIM = 64
IDX_TOPK_MAX = 256

C_HEADS = 32
C_KV_HEADS = 4
C_HEAD_DIM = 64
WINDOW = 128

LANE = 128
QB = 128
KT = 256
VMEM_LIMIT = 56 * 1024 * 1024


def _cp(*sem):
    return pltpu.CompilerParams(dimension_semantics=sem, vmem_limit_bytes=VMEM_LIMIT)


def _tile(n, pref):
    t = min(n, pref)
    assert n % t == 0, (n, pref)
    return t


def _dot(a, b):
    return jnp.dot(a, b, preferred_element_type=F32)


def _dot_t(a, b):
    return lax.dot_general(a, b, (((1,), (1,)), ((), ())), preferred_element_type=F32)


def _rms_kernel(x_ref, g_ref, o_ref):
    x = x_ref[...].astype(F32)
    y = x * lax.rsqrt(jnp.mean(x * x, axis=-1, keepdims=True) + EPS)
    o_ref[...] = (y * g_ref[...]).astype(o_ref.dtype)


def _rmsnorm(x2d, g, out_dtype):
    m, d = x2d.shape
    tm = _tile(m, 512)
    return pl.pallas_call(
        _rms_kernel,
        out_shape=jax.ShapeDtypeStruct((m, d), out_dtype),
        grid=(m // tm,),
        in_specs=[pl.BlockSpec((tm, d), lambda i: (i, 0)), pl.BlockSpec((1, d), lambda i: (0, 0))],
        out_specs=pl.BlockSpec((tm, d), lambda i: (i, 0)),
        compiler_params=_cp("parallel"),
        name="rmsnorm",
    )(x2d, g.reshape(1, d).astype(F32))


def _mm_kernel(a_ref, w_ref, o_ref):
    o_ref[...] = _dot(a_ref[...], w_ref[...]).astype(o_ref.dtype)


def _matmul(a, w, out_dtype):
    m, k = a.shape
    n = w.shape[1]
    tm = _tile(m, 1024)
    tn = _tile(n, 512) if n % 512 == 0 else n
    return pl.pallas_call(
        _mm_kernel,
        out_shape=jax.ShapeDtypeStruct((m, n), out_dtype),
        grid=(m // tm, n // tn),
        in_specs=[pl.BlockSpec((tm, k), lambda i, j: (i, 0)), pl.BlockSpec((k, tn), lambda i, j: (0, j))],
        out_specs=pl.BlockSpec((tm, tn), lambda i, j: (i, j)),
        compiler_params=_cp("parallel", "arbitrary"),
        name="matmul",
    )(a, w)


def _rope_tables(s):
    d = A_ROPE
    inv = 1.0 / (ROPE_THETA ** (jnp.arange(0, d, 2, dtype=F32) / d))
    ang = jnp.arange(s, dtype=F32)[:, None] * inv[None, :]
    cos, sin = jnp.cos(ang), jnp.sin(ang)
    z = jnp.zeros((s, LANE - d), F32)
    return jnp.concatenate([cos, cos, z], axis=1), jnp.concatenate([-sin, sin, z], axis=1)


def _swap_halves(w):
    h = w.shape[-1] // 2
    return jnp.concatenate([w[..., h:], w[..., :h]], axis=-1)


def _uq_kernel(cq_ref, g_ref, w_ref, cos_ref, sin_ref, o_ref, *, n_nope):
    x = cq_ref[...]
    y = x * lax.rsqrt(jnp.mean(x * x, axis=-1, keepdims=True) + EPS)
    h = (y * g_ref[...]).astype(BF16)
    n = w_ref.shape[1]
    cw = 512
    for c in range(0, n_nope, cw):
        o_ref[:, c:c + cw] = _dot(h, w_ref[:, c:c + cw]).astype(BF16)
    cos = cos_ref[...]
    sin = sin_ref[...]
    for c in range(n_nope, n, cw):
        y = _dot(h, w_ref[:, c:c + cw])
        parts = []
        for k in range(cw // LANE):
            slab = y[:, k * LANE:(k + 1) * LANE]
            parts.append(slab * cos + pltpu.roll(slab, LANE // 2, 1) * sin)
        o_ref[:, c:c + cw] = jnp.concatenate(parts, axis=1).astype(BF16)


def _uq_proj(cqkv, q_norm, w_uq2, cos_t, sin_t, s):
    m = cqkv.shape[0]
    n = w_uq2.shape[1]
    tm = _tile(s, 512)
    nblk = s // tm
    return pl.pallas_call(
        functools.partial(_uq_kernel, n_nope=A_HEADS * A_NOPE),
        out_shape=jax.ShapeDtypeStruct((m, n), BF16),
        grid=(m // tm,),
        in_specs=[
            pl.BlockSpec((tm, A_Q_LORA), lambda i: (i, 0)),
            pl.BlockSpec((1, A_Q_LORA), lambda i: (0, 0)),
            pl.BlockSpec((A_Q_LORA, n), lambda i: (0, 0)),
            pl.BlockSpec((tm, LANE), lambda i: (i % nblk, 0)),
            pl.BlockSpec((tm, LANE), lambda i: (i % nblk, 0)),
        ],
        out_specs=pl.BlockSpec((tm, n), lambda i: (i, 0)),
        compiler_params=_cp("parallel"),
        name="mla_q_up",
    )(cqkv, q_norm.reshape(1, -1).astype(F32), w_uq2, cos_t, sin_t)


def _ukv_kernel(ckv_ref, kr_ref, g_ref, w_ref, cos_ref, sin_ref, kv_ref, krp_ref):
    x = ckv_ref[...]
    y = x * lax.rsqrt(jnp.mean(x * x, axis=-1, keepdims=True) + EPS)
    h = (y * g_ref[...]).astype(BF16)
    n = w_ref.shape[1]
    cw = 512
    for c in range(0, n, cw):
        kv_ref[:, c:c + cw] = _dot(h, w_ref[:, c:c + cw]).astype(BF16)
    slab = kr_ref[...]
    krp_ref[...] = (slab * cos_ref[...] + pltpu.roll(slab, LANE // 2, 1) * sin_ref[...]).astype(BF16)


def _ukv_proj(cqkv, kr2, kv_norm, w_ukv, cos_t, sin_t, s):
    m = cqkv.shape[0]
    n = w_ukv.shape[1]
    tm = _tile(s, 512)
    nblk = s // tm
    cblk = A_Q_LORA // A_KV_LORA
    return pl.pallas_call(
        _ukv_kernel,
        out_shape=(jax.ShapeDtypeStruct((m, n), BF16), jax.ShapeDtypeStruct((m, LANE), BF16)),
        grid=(m // tm,),
        in_specs=[
            pl.BlockSpec((tm, A_KV_LORA), lambda i: (i, cblk)),
            pl.BlockSpec((tm, LANE), lambda i: (i, 0)),
            pl.BlockSpec((1, A_KV_LORA), lambda i: (0, 0)),
            pl.BlockSpec((A_KV_LORA, n), lambda i: (0, 0)),
            pl.BlockSpec((tm, LANE), lambda i: (i % nblk, 0)),
            pl.BlockSpec((tm, LANE), lambda i: (i % nblk, 0)),
        ],
        out_specs=(pl.BlockSpec((tm, n), lambda i: (i, 0)), pl.BlockSpec((tm, LANE), lambda i: (i, 0))),
        compiler_params=_cp("parallel"),
        name="mla_kv_up",
    )(cqkv, kr2, kv_norm.reshape(1, -1).astype(F32), w_ukv, cos_t, sin_t)


def _mla_attn_kernel(qn_ref, qr_ref, kn_ref, v_ref, kr_ref, o_ref, kcat, m_scr, l_scr, acc_scr, *, tq, scale):
    i = pl.program_id(2)

    @pl.when(i == 0)
    def _():
        kcat[:, :LANE] = kn_ref[...]
        kcat[:, LANE:] = kr_ref[...]

    q = jnp.concatenate([qn_ref[...], qr_ref[...]], axis=1)
    m_scr[...] = jnp.full(m_scr.shape, -jnp.inf, F32)
    l_scr[...] = jnp.zeros(l_scr.shape, F32)
    acc_scr[...] = jnp.zeros(acc_scr.shape, F32)

    def step(j, masked):
        k0 = pl.multiple_of(j * tq, tq)
        s = _dot_t(q, kcat[pl.ds(k0, tq), :]) * scale
        if masked:
            row = lax.broadcasted_iota(jnp.int32, s.shape, 0)
            col = lax.broadcasted_iota(jnp.int32, s.shape, 1)
            s = jnp.where(row >= col, s, NEG_INF)
        m_prev = m_scr[...]
        m_new = jnp.maximum(m_prev, jnp.max(s, axis=1, keepdims=True))
        alpha = jnp.exp(m_prev - m_new)
        p = jnp.exp(s - m_new)
        l_scr[...] = alpha * l_scr[...] + jnp.sum(p, axis=1, keepdims=True)
        acc_scr[...] = alpha * acc_scr[...] + _dot(p.astype(BF16), v_ref[pl.ds(k0, tq), :])
        m_scr[...] = m_new

    def body(j, c):
        step(j, False)
        return c

    lax.fori_loop(0, i, body, 0)
    step(i, True)
    o_ref[...] = (acc_scr[...] / l_scr[...]).astype(o_ref.dtype)


def _mla_attention(qp, kv, krp, b, s):
    tq = _tile(s, 256)
    hn = A_HEADS
    return pl.pallas_call(
        functools.partial(_mla_attn_kernel, tq=tq, scale=(A_NOPE + A_ROPE) ** -0.5),
        out_shape=jax.ShapeDtypeStruct((b, s, hn * A_V), BF16),
        grid=(b, hn, s // tq),
        in_specs=[
            pl.BlockSpec((None, tq, LANE), lambda bb, h, i: (bb, i, h)),
            pl.BlockSpec((None, tq, LANE), lambda bb, h, i: (bb, i, hn + h)),
            pl.BlockSpec((None, s, LANE), lambda bb, h, i: (bb, 0, 2 * h)),
            pl.BlockSpec((None, s, LANE), lambda bb, h, i: (bb, 0, 2 * h + 1)),
            pl.BlockSpec((None, s, LANE), lambda bb, h, i: (bb, 0, 0)),
        ],
        out_specs=pl.BlockSpec((None, tq, LANE), lambda bb, h, i: (bb, i, h)),
        scratch_shapes=[
            pltpu.VMEM((s, 2 * LANE), BF16),
            pltpu.VMEM((tq, 1), F32),
            pltpu.VMEM((tq, 1), F32),
            pltpu.VMEM((tq, A_V), F32),
        ],
        compiler_params=_cp("parallel", "parallel", "arbitrary"),
        name="mla_attention",
    )(qp, qp, kv, kv, krp)


def _mem_attn_kernel(q_ref, kv_ref, o_ref):
    d = MEM_HEAD_DIM
    for h in range(MEM_HEADS):
        q = q_ref[:, h * d:(h + 1) * d]
        k = kv_ref[:, h * d:(h + 1) * d]
        v = kv_ref[:, MEM_WIDTH + h * d:MEM_WIDTH + (h + 1) * d]
        s = _dot_t(q, k) * (d ** -0.5)
        m = jnp.max(s, axis=1, keepdims=True)
        e = jnp.exp(s - m)
        p = e / jnp.sum(e, axis=1, keepdims=True)
        o_ref[:, h * d:(h + 1) * d] = _dot(p.astype(BF16), v).astype(o_ref.dtype)


def _mem_attention(g3, mq_blk, memkv3):
    b, s, _ = g3.shape
    ml = memkv3.shape[1]
    tq = _tile(s, 512)
    return pl.pallas_call(
        _mem_attn_kernel,
        out_shape=jax.ShapeDtypeStruct((b, s, MEM_WIDTH), BF16),
        grid=(b, s // tq),
        in_specs=[
            pl.BlockSpec((None, tq, MEM_WIDTH), lambda bb, i: (bb, i, mq_blk)),
            pl.BlockSpec((None, ml, 2 * MEM_WIDTH), lambda bb, i: (bb, 0, 0)),
        ],
        out_specs=pl.BlockSpec((None, tq, MEM_WIDTH), lambda bb, i: (bb, i, 0)),
        compiler_params=_cp("parallel", "parallel"),
        name="mem_attention",
    )(g3, memkv3)


def _out_kernel(so_ref, mo_ref, g_ref, x_ref, w_ref, o_ref, y_scr):
    @pl.when(pl.program_id(1) == 0)
    def _():
        ws = so_ref.shape[1]
        g = g_ref[...].astype(F32)
        sg = g * (1.0 / (1.0 + jnp.exp(-g)))
        y_scr[:, :ws] = (so_ref[...].astype(F32) * sg[:, :ws]).astype(BF16)
        y_scr[:, ws:] = (mo_ref[...].astype(F32) * sg[:, ws:]).astype(BF16)

    o_ref[...] = x_ref[...] + _dot(y_scr[...], w_ref[...])


def _out_proj(self_out, mem_out, g2, x2d, w_out):
    m, ws = self_out.shape
    wm = mem_out.shape[1]
    kk, n = w_out.shape
    tm = _tile(m, 512)
    tn = _tile(n, 1024)
    return pl.pallas_call(
        _out_kernel,
        out_shape=jax.ShapeDtypeStruct((m, n), F32),
        grid=(m // tm, n // tn),
        in_specs=[
            pl.BlockSpec((tm, ws), lambda i, j: (i, 0)),
            pl.BlockSpec((tm, wm), lambda i, j: (i, 0)),
            pl.BlockSpec((tm, kk), lambda i, j: (i, 0)),
            pl.BlockSpec((tm, tn), lambda i, j: (i, j)),
            pl.BlockSpec((kk, tn), lambda i, j: (0, j)),
        ],
        out_specs=pl.BlockSpec((tm, tn), lambda i, j: (i, j)),
        scratch_shapes=[pltpu.VMEM((tm, kk), BF16)],
        compiler_params=_cp("parallel", "arbitrary"),
        name="gated_out_proj",
    )(self_out, mem_out, g2, x2d, w_out)


def _t5_bucket(rel):
    n = jnp.maximum(rel, 0)
    max_exact = N_BUCKETS // 2
    nf = jnp.maximum(n, 1).astype(F32)
    large = max_exact + (jnp.log(nf / max_exact) / math.log(MAX_DISTANCE / max_exact)
                         * (N_BUCKETS - max_exact)).astype(jnp.int32)
    large = jnp.minimum(large, N_BUCKETS - 1)
    return jnp.where(n < max_exact, n, large)


def _bias_kernel(rb_ref, bd_ref, bp_ref, tswa_ref, tdsa_ref):
    bd = bd_ref[...]
    bp = bp_ref[...]
    for h in range(N_BIAS_HEADS):
        g, j = divmod(h, N_BIAS_HEADS // B_KV_HEADS)

        def body(b, c, h=h):
            d, p = c
            v = rb_ref[b, h]
            return jnp.where(bd == b, v, d), jnp.where(bp == b, v, p)

        z = jnp.zeros((QB, QB), F32)
        d, p = lax.fori_loop(0, N_BUCKETS, body, (z, z))
        far = rb_ref[N_BUCKETS - 1, h]
        rows = pl.ds(j * QB, QB)
        tswa_ref[0, g, rows, :] = d
        tswa_ref[1, g, rows, :] = p
        tdsa_ref[0, g, rows, :] = z
        tdsa_ref[1, g, rows, :] = d - far
        tdsa_ref[2, g, rows, :] = p - far


def _bias_tables(rel_bias):
    t = jnp.arange(QB)[:, None]
    c = jnp.arange(QB)[None, :]
    bd = _t5_bucket(t - c).astype(jnp.int32)
    bp = _t5_bucket(t + QB - c).astype(jnp.int32)
    rows = (N_BIAS_HEADS // B_KV_HEADS) * QB
    return pl.pallas_call(
        _bias_kernel,
        out_shape=(jax.ShapeDtypeStruct((2, B_KV_HEADS, rows, QB), F32),
                   jax.ShapeDtypeStruct((3, B_KV_HEADS, rows, QB), F32)),
        in_specs=[pl.BlockSpec(memory_space=pltpu.SMEM),
                  pl.BlockSpec(memory_space=pltpu.VMEM),
                  pl.BlockSpec(memory_space=pltpu.VMEM)],
        out_specs=(pl.BlockSpec(memory_space=pltpu.VMEM), pl.BlockSpec(memory_space=pltpu.VMEM)),
        compiler_params=pltpu.CompilerParams(vmem_limit_bytes=VMEM_LIMIT),
        name="t5_bias_tables",
    )(rel_bias.astype(F32), bd, bp)


def _stack_heads(ref, g, n_per, d):
    return jnp.concatenate([ref[:, (g * n_per + j) * d:(g * n_per + j + 1) * d] for j in range(n_per)], axis=0)


def _unstack_heads(o, n_per):
    rows = o.shape[0] // n_per
    return jnp.concatenate([o[j * rows:(j + 1) * rows] for j in range(n_per)], axis=1)


def _swa_kernel(sink_ref, q_ref, kp_ref, kc_ref, vp_ref, vc_ref, t_ref, o_ref):
    i = pl.program_id(1)
    n_per = C_HEADS // C_KV_HEADS
    d = C_HEAD_DIM
    rows = n_per * QB
    kb = jnp.concatenate([kp_ref[...], kc_ref[...]], axis=0)
    vb = jnp.concatenate([vp_ref[...], vc_ref[...]], axis=0)
    t_loc = lax.broadcasted_iota(jnp.int32, (rows, 2 * QB), 0) & (QB - 1)
    kj = lax.broadcasted_iota(jnp.int32, (rows, 2 * QB), 1)
    rel = t_loc + QB - kj
    valid = (rel >= 0) & (rel < WINDOW) & ((i * QB - QB + kj) >= 0)
    head = lax.shift_right_logical(lax.broadcasted_iota(jnp.int32, (rows, 1), 0), QB.bit_length() - 1)
    for g in range(C_KV_HEADS):
        qs = _stack_heads(q_ref, g, n_per, d)
        s = _dot_t(qs, kb[:, g * d:(g + 1) * d]) * (d ** -0.5)
        s = s + jnp.concatenate([t_ref[1, g], t_ref[0, g]], axis=1)
        s = jnp.where(valid, s, NEG_INF)
        sink = jnp.zeros((rows, 1), F32)
        for j in range(n_per):
            sink = jnp.where(head == j, sink_ref[0, g * n_per + j], sink)
        m = jnp.maximum(jnp.max(s, axis=1, keepdims=True), sink)
        e = jnp.exp(s - m)
        p = e / (jnp.sum(e, axis=1, keepdims=True) + jnp.exp(sink - m))
        o = _dot(p.astype(BF16), vb[:, g * d:(g + 1) * d])
        o_ref[:, g * n_per * d:(g + 1) * n_per * d] = _unstack_heads(o, n_per).astype(o_ref.dtype)


def _swa_attention(g3, sinks, tswa, q_blk, k_blk, v_blk):
    b, s, _ = g3.shape
    wq = C_HEADS * C_HEAD_DIM
    wk = C_KV_HEADS * C_HEAD_DIM
    prev = lambda i: jnp.maximum(i - 1, 0)
    return pl.pallas_call(
        _swa_kernel,
        out_shape=jax.ShapeDtypeStruct((b, s, wq), BF16),
        grid=(b, s // QB),
        in_specs=[
            pl.BlockSpec(memory_space=pltpu.SMEM),
            pl.BlockSpec((None, QB, wq), lambda bb, i: (bb, i, q_blk)),
            pl.BlockSpec((None, QB, wk), lambda bb, i: (bb, prev(i), k_blk)),
            pl.BlockSpec((None, QB, wk), lambda bb, i: (bb, i, k_blk)),
            pl.BlockSpec((None, QB, wk), lambda bb, i: (bb, prev(i), v_blk)),
            pl.BlockSpec((None, QB, wk), lambda bb, i: (bb, i, v_blk)),
            pl.BlockSpec(tswa.shape, lambda bb, i: (0, 0, 0, 0)),
        ],
        out_specs=pl.BlockSpec((None, QB, wq), lambda bb, i: (bb, i, 0)),
        compiler_params=_cp("parallel", "parallel"),
        name="swa_attention",
    )(sinks.reshape(1, -1).astype(F32), g3, g3, g3, g3, g3, tswa)


def _ordered_to_f32(key):
    bits = jnp.where(key >= 0, key, key ^ jnp.int32(0x7FFFFFFF))
    return lax.bitcast_convert_type(bits, F32)


def _dsa_kernel(q_ref, k_ref, v_ref, iq_ref, ik_ref, iw_ref, t_ref, o_ref,
                sc, madd, iqs, m_scr, l_scr, acc_scr, *, s_len, k_top):
    i = pl.program_id(1)
    n_per = B_HEADS // B_KV_HEADS
    d = B_HEAD_DIM
    half = lax.shift_right_logical(i, 1)
    odd = i & 1
    n_t = half + 1
    row = i * QB + lax.broadcasted_iota(jnp.int32, (QB, KT), 0)
    col0 = lax.broadcasted_iota(jnp.int32, (QB, KT), 1)

    for h in range(IDX_HEADS):
        iqs[h] = iq_ref[:, h * IDX_DIM:(h + 1) * IDX_DIM]
    w = iw_ref[...] * (IDX_DIM ** -0.5 * IDX_HEADS ** -0.5)
    wcols = [w[:, h:h + 1] for h in range(IDX_HEADS)]

    def score_tile(kt, c):
        k0 = pl.multiple_of(kt * KT, KT)
        ikt = ik_ref[pl.ds(k0, KT), :]
        acc = jnp.zeros((QB, KT), F32)
        for h in range(IDX_HEADS):
            acc = acc + jnp.maximum(_dot_t(iqs[h], ikt), 0.0) * wcols[h]
        sc[:, pl.ds(k0, KT)] = jnp.where(row >= k0 + col0, acc, NEG_INF)
        return c

    lax.fori_loop(0, n_t, score_tile, 0)

    tail = (s_len - n_t * KT).astype(F32)
    int_min = jnp.int32(-2 ** 31)

    def bisect(b, prefix):
        cand = prefix + lax.shift_left(jnp.int32(1), 31 - b)
        cand_f = _ordered_to_f32(cand)

        def count_tile(kt, c):
            k0 = pl.multiple_of(kt * KT, KT)
            ge = jnp.where(sc[:, pl.ds(k0, KT)] >= cand_f, 1.0, 0.0)
            return c + ge[:, :LANE] + ge[:, LANE:]

        cnt = lax.fori_loop(0, n_t, count_tile, jnp.zeros((QB, LANE), F32))
        total = jnp.sum(cnt, axis=1, keepdims=True) + jnp.where(cand_f <= NEG_INF, tail, 0.0)
        return jnp.where(total >= k_top, cand, prefix)

    thr = _ordered_to_f32(lax.fori_loop(0, 32, bisect, jnp.full((QB, 1), int_min, jnp.int32)))

    def mask_tile(kt, c):
        k0 = pl.multiple_of(kt * KT, KT)
        sel = (sc[:, pl.ds(k0, KT)] >= thr) & (row >= k0 + col0)
        madd[:, pl.ds(k0, KT)] = jnp.where(sel, 0.0, NEG_INF)
        return c

    lax.fori_loop(0, n_t, mask_tile, 0)

    near0 = jnp.where(odd == 1, half, jnp.maximum(half - 1, 0))
    rows = n_per * QB
    for g in range(B_KV_HEADS):
        qs = _stack_heads(q_ref, g, n_per, d) * (d ** -0.5)
        m_scr[...] = jnp.full(m_scr.shape, -jnp.inf, F32)
        l_scr[...] = jnp.zeros(l_scr.shape, F32)
        acc_scr[...] = jnp.zeros(acc_scr.shape, F32)

        def step(kt, near, g=g, qs=qs):
            k0 = pl.multiple_of(kt * KT, KT)
            s = _dot_t(qs, k_ref[pl.ds(k0, KT), g * d:(g + 1) * d])
            if near:
                li = jnp.where(odd == 1, 2, jnp.where(kt == half, 1, 0))
                ri = jnp.where(odd == 1, 1, jnp.where(kt == half, 0, 2))
                s = s + jnp.concatenate([t_ref[li, g], t_ref[ri, g]], axis=1)
            s = (s.reshape(n_per, QB, KT) + madd[:, pl.ds(k0, KT)][None]).reshape(rows, KT)
            m_prev = m_scr[...]
            m_new = jnp.maximum(m_prev, jnp.max(s, axis=1, keepdims=True))
            alpha = jnp.exp(m_prev - m_new)
            p = jnp.exp(s - m_new)
            l_scr[...] = alpha * l_scr[...] + jnp.sum(p, axis=1, keepdims=True)
            acc_scr[...] = alpha * acc_scr[...] + _dot(p.astype(BF16), v_ref[pl.ds(k0, KT), g * d:(g + 1) * d])
            m_scr[...] = m_new

        def far_body(kt, c):
            step(kt, False)
            return c

        def near_body(kt, c):
            step(kt, True)
            return c

        lax.fori_loop(0, near0, far_body, 0)
        lax.fori_loop(near0, n_t, near_body, 0)
        o = acc_scr[...] / l_scr[...]
        o_ref[:, g * n_per * d:(g + 1) * n_per * d] = _unstack_heads(o, n_per).astype(o_ref.dtype)


def _dsa_attention(g3, ik3, iw3, tdsa, q_blk, k_blk, v_blk, iq_blk):
    b, s, _ = g3.shape
    k_top = min(IDX_TOPK_MAX, s // 4)
    wq = B_HEADS * B_HEAD_DIM
    wk = B_KV_HEADS * B_HEAD_DIM
    wi = IDX_HEADS * IDX_DIM
    rows = (B_HEADS // B_KV_HEADS) * QB
    return pl.pallas_call(
        functools.partial(_dsa_kernel, s_len=s, k_top=k_top),
        out_shape=jax.ShapeDtypeStruct((b, s, wq), BF16),
        grid=(b, s // QB),
        in_specs=[
            pl.BlockSpec((None, QB, wq), lambda bb, i: (bb, i, q_blk)),
            pl.BlockSpec((None, s, wk), lambda bb, i: (bb, 0, k_blk)),
            pl.BlockSpec((None, s, wk), lambda bb, i: (bb, 0, v_blk)),
            pl.BlockSpec((None, QB, wi), lambda bb, i: (bb, i, iq_blk)),
            pl.BlockSpec((None, s, IDX_DIM), lambda bb, i: (bb, 0, 0)),
            pl.BlockSpec((None, QB, IDX_HEADS), lambda bb, i: (bb, i, 0)),
            pl.BlockSpec(tdsa.shape, lambda bb, i: (0, 0, 0, 0)),
        ],
        out_specs=pl.BlockSpec((None, QB, wq), lambda bb, i: (bb, i, 0)),
        scratch_shapes=[
            pltpu.VMEM((QB, s), F32),
            pltpu.VMEM((QB, s), F32),
            pltpu.VMEM((IDX_HEADS, QB, IDX_DIM), BF16),
            pltpu.VMEM((rows, 1), F32),
            pltpu.VMEM((rows, 1), F32),
            pltpu.VMEM((rows, B_HEAD_DIM), F32),
        ],
        compiler_params=_cp("parallel", "parallel"),
        name="dsa_attention",
    )(g3, g3, g3, g3, ik3, iw3, tdsa)


def _mla_layer(h, w_in, q_norm, w_uq, kv_norm, w_ukv, cos_t, sin_t, b, s):
    o_ckv = A_Q_LORA + A_KV_LORA
    o_mq = o_ckv + A_ROPE
    o_gate = o_mq + MEM_WIDTH
    w_f = w_in[:, :o_ckv].astype(BF16)
    w_kr = w_in[:, o_ckv:o_mq]
    w_kr2 = jnp.concatenate([w_kr, _swap_halves(w_kr)], axis=1).astype(BF16)
    w_g = jnp.concatenate([w_in[:, o_gate:], w_in[:, o_mq:o_gate]], axis=1).astype(BF16)
    cqkv = _matmul(h, w_f, F32)
    kr2 = _matmul(h, w_kr2, F32)
    g2 = _matmul(h, w_g, BF16)

    hd = A_NOPE + A_ROPE
    wq3 = w_uq.reshape(A_Q_LORA, A_HEADS, hd)
    w_r = wq3[:, :, A_NOPE:]
    w_uq2 = jnp.concatenate([
        wq3[:, :, :A_NOPE].reshape(A_Q_LORA, A_HEADS * A_NOPE),
        jnp.concatenate([w_r, _swap_halves(w_r)], axis=-1).reshape(A_Q_LORA, A_HEADS * 2 * A_ROPE),
    ], axis=1).astype(BF16)
    qp = _uq_proj(cqkv, q_norm, w_uq2, cos_t, sin_t, s)
    kv, krp = _ukv_proj(cqkv, kr2, kv_norm, w_ukv.astype(BF16), cos_t, sin_t, s)
    self_out = _mla_attention(qp.reshape(b, s, -1), kv.reshape(b, s, -1), krp.reshape(b, s, -1), b, s)
    return self_out.reshape(b * s, -1), g2, 3


def _dsa_layer(h, w_in, tdsa, b, s):
    wq = B_HEADS * B_HEAD_DIM
    wk = B_KV_HEADS * B_HEAD_DIM
    wi = IDX_HEADS * IDX_DIM
    o_k = wq
    o_v = o_k + wk
    o_iq = o_v + wk
    o_ik = o_iq + wi
    o_iw = o_ik + IDX_DIM
    o_mq = o_iw + IDX_HEADS
    o_gate = o_mq + MEM_WIDTH
    w_g = jnp.concatenate([w_in[:, o_gate:], w_in[:, o_mq:o_gate], w_in[:, :o_k], w_in[:, o_iq:o_ik],
                           w_in[:, o_k:o_v], w_in[:, o_v:o_iq]], axis=1).astype(BF16)
    w_i = w_in[:, o_ik:o_mq].astype(BF16)
    g2 = _matmul(h, w_g, BF16)
    ii = _matmul(h, w_i, F32)
    ik3 = ii[:, :IDX_DIM].astype(BF16).reshape(b, s, IDX_DIM)
    iw3 = ii[:, IDX_DIM:].reshape(b, s, IDX_HEADS)
    self_out = _dsa_attention(g2.reshape(b, s, -1), ik3, iw3, tdsa,
                              q_blk=2, k_blk=7168 // wk, v_blk=7424 // wk, iq_blk=6)
    return self_out.reshape(b * s, -1), g2, 3


def _swa_layer(h, w_in, sinks, tswa, b, s):
    wq = C_HEADS * C_HEAD_DIM
    wk = C_KV_HEADS * C_HEAD_DIM
    o_k = wq
    o_v = o_k + wk
    o_mq = o_v + wk
    o_gate = o_mq + MEM_WIDTH
    w_g = jnp.concatenate([w_in[:, o_gate:], w_in[:, o_mq:o_gate], w_in[:, :o_k],
                           w_in[:, o_k:o_v], w_in[:, o_v:o_mq]], axis=1).astype(BF16)
    g2 = _matmul(h, w_g, BF16)
    self_out = _swa_attention(g2.reshape(b, s, -1), sinks, tswa, q_blk=2, k_blk=6144 // wk, v_blk=6400 // wk)
    return self_out.reshape(b * s, -1), g2, 3


def kernel(x, mem, norm_in, final_norm, mem_norm, rel_bias, w_in_a, a_q_norm, w_uq, a_kv_norm, w_ukv,
           w_in_b, w_in_c, c_sinks, w_mem_kv, w_out):
    b, s, d = x.shape
    depth = norm_in.shape[0]
    ml = mem.shape[1]
    xf = x.reshape(b * s, d)
    mem_n = _rmsnorm(mem.reshape(b * ml, d), mem_norm, BF16)
    cos_t, sin_t = _rope_tables(s)
    tswa, tdsa = _bias_tables(rel_bias)
    for i in range(depth):
        h = _rmsnorm(xf, norm_in[i], BF16)
        kind, j = i % N_MIXERS, i // N_MIXERS
        if kind == 0:
            self_out, g2, mq_blk = _mla_layer(h, w_in_a[j], a_q_norm[j], w_uq[j], a_kv_norm[j], w_ukv[j],
                                              cos_t, sin_t, b, s)
        elif kind == 1:
            self_out, g2, mq_blk = _dsa_layer(h, w_in_b[j], tdsa, b, s)
        else:
            self_out, g2, mq_blk = _swa_layer(h, w_in_c[j], c_sinks[j], tswa, b, s)
        memkv = _matmul(mem_n, w_mem_kv[i].astype(BF16), BF16)
        mem_out = _mem_attention(g2.reshape(b, s, -1), mq_blk, memkv.reshape(b, ml, -1))
        xf = _out_proj(self_out, mem_out.reshape(b * s, -1), g2, xf, w_out[i].astype(BF16))
    return _rmsnorm(xf, final_norm, F32).reshape(b, s, d)
```

```python
import functools
import math

import jax
import jax.numpy as jnp
from jax import lax
from jax.experimental import pallas as pl
from jax.experimental.pallas import tpu as pltpu

F32 = jnp.float32
BF16 = jnp.bfloat16

EPS = 1e-6
NEG_INF = -1e30
N_MIXERS = 3

N_BUCKETS = 32
MAX_DISTANCE = 128
N_BIAS_HEADS = 32

MEM_HEADS = 4
MEM_HEAD_DIM = 256
MEM_WIDTH = MEM_HEADS * MEM_HEAD_DIM

A_HEADS = 16
A_Q_LORA = 1536
A_KV_LORA = 512
A_NOPE = 128
A_ROPE = 64
A_V = 128
ROPE_THETA = 10000.0

B_HEADS = 32
B_KV_HEADS = 4
B_HEAD_DIM = 64
IDX_HEADS = 16
IDX_DIM = 64
IDX_TOPK_MAX = 256

C_HEADS = 32
C_KV_HEADS = 4
C_HEAD_DIM = 64
WINDOW = 128

LANE = 128
QB = 128
KT = 256
VMEM_LIMIT = 56 * 1024 * 1024


def _cp(*sem):
    return pltpu.CompilerParams(dimension_semantics=sem, vmem_limit_bytes=VMEM_LIMIT)


def _tile(n, pref):
    t = min(n, pref)
    assert n % t == 0, (n, pref)
    return t


def _dot(a, b):
    return jnp.dot(a, b, preferred_element_type=F32)


def _dot_t(a, b):
    return lax.dot_general(a, b, (((1,), (1,)), ((), ())), preferred_element_type=F32)


def _rms_kernel(x_ref, g_ref, o_ref):
    x = x_ref[...].astype(F32)
    y = x * lax.rsqrt(jnp.mean(x * x, axis=-1, keepdims=True) + EPS)
    o_ref[...] = (y * g_ref[...]).astype(o_ref.dtype)


def _rmsnorm(x2d, g, out_dtype):
    m, d = x2d.shape
    tm = _tile(m, 512)
    return pl.pallas_call(
        _rms_kernel,
        out_shape=jax.ShapeDtypeStruct((m, d), out_dtype),
        grid=(m // tm,),
        in_specs=[pl.BlockSpec((tm, d), lambda i: (i, 0)), pl.BlockSpec((1, d), lambda i: (0, 0))],
        out_specs=pl.BlockSpec((tm, d), lambda i: (i, 0)),
        compiler_params=_cp("parallel"),
        name="rmsnorm",
    )(x2d, g.reshape(1, d).astype(F32))


def _mm_kernel(a_ref, w_ref, o_ref):
    o_ref[...] = _dot(a_ref[...], w_ref[...]).astype(o_ref.dtype)


def _matmul(a, w, out_dtype):
    m, k = a.shape
    n = w.shape[1]
    tm = _tile(m, 1024)
    tn = _tile(n, 512) if n % 512 == 0 else n
    return pl.pallas_call(
        _mm_kernel,
        out_shape=jax.ShapeDtypeStruct((m, n), out_dtype),
        grid=(m // tm, n // tn),
        in_specs=[pl.BlockSpec((tm, k), lambda i, j: (i, 0)), pl.BlockSpec((k, tn), lambda i, j: (0, j))],
        out_specs=pl.BlockSpec((tm, tn), lambda i, j: (i, j)),
        compiler_params=_cp("parallel", "arbitrary"),
        name="matmul",
    )(a, w)


def _rope_tables(s):
    d = A_ROPE
    inv = 1.0 / (ROPE_THETA ** (jnp.arange(0, d, 2, dtype=F32) / d))
    ang = jnp.arange(s, dtype=F32)[:, None] * inv[None, :]
    cos, sin = jnp.cos(ang), jnp.sin(ang)
    z = jnp.zeros((s, LANE - d), F32)
    return jnp.concatenate([cos, cos, z], axis=1), jnp.concatenate([-sin, sin, z], axis=1)


def _swap_halves(w):
    h = w.shape[-1] // 2
    return jnp.concatenate([w[..., h:], w[..., :h]], axis=-1)


def _uq_kernel(cq_ref, g_ref, w_ref, cos_ref, sin_ref, o_ref, *, n_nope):
    x = cq_ref[...]
    y = x * lax.rsqrt(jnp.mean(x * x, axis=-1, keepdims=True) + EPS)
    h = (y * g_ref[...]).astype(BF16)
    n = w_ref.shape[1]
    cw = 512
    for c in range(0, n_nope, cw):
        o_ref[:, c:c + cw] = _dot(h, w_ref[:, c:c + cw]).astype(BF16)
    cos = cos_ref[...]
    sin = sin_ref[...]
    for c in range(n_nope, n, cw):
        y = _dot(h, w_ref[:, c:c + cw])
        parts = []
        for k in range(cw // LANE):
            slab = y[:, k * LANE:(k + 1) * LANE]
            parts.append(slab * cos + pltpu.roll(slab, LANE // 2, 1) * sin)
        o_ref[:, c:c + cw] = jnp.concatenate(parts, axis=1).astype(BF16)


def _uq_proj(cqkv, q_norm, w_uq2, cos_t, sin_t, s):
    m = cqkv.shape[0]
    n = w_uq2.shape[1]
    tm = _tile(s, 512)
    nblk = s // tm
    return pl.pallas_call(
        functools.partial(_uq_kernel, n_nope=A_HEADS * A_NOPE),
        out_shape=jax.ShapeDtypeStruct((m, n), BF16),
        grid=(m // tm,),
        in_specs=[
            pl.BlockSpec((tm, A_Q_LORA), lambda i: (i, 0)),
            pl.BlockSpec((1, A_Q_LORA), lambda i: (0, 0)),
            pl.BlockSpec((A_Q_LORA, n), lambda i: (0, 0)),
            pl.BlockSpec((tm, LANE), lambda i: (i % nblk, 0)),
            pl.BlockSpec((tm, LANE), lambda i: (i % nblk, 0)),
        ],
        out_specs=pl.BlockSpec((tm, n), lambda i: (i, 0)),
        compiler_params=_cp("parallel"),
        name="mla_q_up",
    )(cqkv, q_norm.reshape(1, -1).astype(F32), w_uq2, cos_t, sin_t)


def _ukv_kernel(ckv_ref, kr_ref, g_ref, w_ref, cos_ref, sin_ref, kv_ref, krp_ref):
    x = ckv_ref[...]
    y = x * lax.rsqrt(jnp.mean(x * x, axis=-1, keepdims=True) + EPS)
    h = (y * g_ref[...]).astype(BF16)
    n = w_ref.shape[1]
    cw = 512
    for c in range(0, n, cw):
        kv_ref[:, c:c + cw] = _dot(h, w_ref[:, c:c + cw]).astype(BF16)
    slab = kr_ref[...]
    krp_ref[...] = (slab * cos_ref[...] + pltpu.roll(slab, LANE // 2, 1) * sin_ref[...]).astype(BF16)


def _ukv_proj(cqkv, kr2, kv_norm, w_ukv, cos_t, sin_t, s):
    m = cqkv.shape[0]
    n = w_ukv.shape[1]
    tm = _tile(s, 512)
    nblk = s // tm
    cblk = A_Q_LORA // A_KV_LORA
    return pl.pallas_call(
        _ukv_kernel,
        out_shape=(jax.ShapeDtypeStruct((m, n), BF16), jax.ShapeDtypeStruct((m, LANE), BF16)),
        grid=(m // tm,),
        in_specs=[
            pl.BlockSpec((tm, A_KV_LORA), lambda i: (i, cblk)),
            pl.BlockSpec((tm, LANE), lambda i: (i, 0)),
            pl.BlockSpec((1, A_KV_LORA), lambda i: (0, 0)),
            pl.BlockSpec((A_KV_LORA, n), lambda i: (0, 0)),
            pl.BlockSpec((tm, LANE), lambda i: (i % nblk, 0)),
            pl.BlockSpec((tm, LANE), lambda i: (i % nblk, 0)),
        ],
        out_specs=(pl.BlockSpec((tm, n), lambda i: (i, 0)), pl.BlockSpec((tm, LANE), lambda i: (i, 0))),
        compiler_params=_cp("parallel"),
        name="mla_kv_up",
    )(cqkv, kr2, kv_norm.reshape(1, -1).astype(F32), w_ukv, cos_t, sin_t)


def _rep2(x):
    return jnp.concatenate([x, x], axis=1)


def _mla_attn_kernel(qn_ref, qr_ref, kv_ref, kr_ref, o_ref, kcat, vaug, m_scr, acc_scr, *, tq, scale, hb):
    i = pl.program_id(2)

    @pl.when(i == 0)
    def _():
        ones = jnp.ones((kv_ref.shape[0], LANE), BF16)
        for hh in range(hb):
            kcat[hh, :, :LANE] = kv_ref[:, 2 * hh * LANE:(2 * hh + 1) * LANE]
            kcat[hh, :, LANE:] = kr_ref[...]
            vaug[hh, :, :LANE] = kv_ref[:, (2 * hh + 1) * LANE:(2 * hh + 2) * LANE]
            vaug[hh, :, LANE:] = ones

    m_scr[...] = jnp.full(m_scr.shape, -jnp.inf, F32)
    acc_scr[...] = jnp.zeros(acc_scr.shape, F32)
    qs = [jnp.concatenate([qn_ref[:, hh * LANE:(hh + 1) * LANE], qr_ref[:, hh * LANE:(hh + 1) * LANE]], axis=1)
          for hh in range(hb)]

    tk = 2 * tq
    n_full = lax.shift_right_logical(i, 1)

    def step(j, masked):
        k0 = pl.multiple_of(j * tk, tk)
        for hh in range(hb):
            s = _dot_t(qs[hh], kcat[hh, pl.ds(k0, tk), :]) * scale
            if masked:
                row = i * tq + lax.broadcasted_iota(jnp.int32, s.shape, 0)
                col = k0 + lax.broadcasted_iota(jnp.int32, s.shape, 1)
                s = jnp.where(row >= col, s, NEG_INF)
            m_prev = m_scr[hh]
            m_new = jnp.maximum(m_prev, jnp.max(s, axis=1, keepdims=True))
            alpha = jnp.exp(m_prev - m_new)
            p = jnp.exp(s - jnp.concatenate([m_new] * (tk // LANE), axis=1))
            acc_scr[hh] = _rep2(alpha) * acc_scr[hh] + _dot(p.astype(BF16), vaug[hh, pl.ds(k0, tk), :])
            m_scr[hh] = m_new

    def body(j, c):
        step(j, False)
        return c

    lax.fori_loop(0, n_full, body, 0)
    step(n_full, True)
    for hh in range(hb):
        acc = acc_scr[hh]
        o_ref[:, hh * LANE:(hh + 1) * LANE] = (acc[:, :LANE] / acc[:, LANE:]).astype(o_ref.dtype)


def _mla_attention(qp, kv, krp, b, s):
    tq = _tile(s, 256)
    hn = A_HEADS
    hb = 4
    ng = hn // hb
    return pl.pallas_call(
        functools.partial(_mla_attn_kernel, tq=tq, scale=(A_NOPE + A_ROPE) ** -0.5, hb=hb),
        out_shape=jax.ShapeDtypeStruct((b, s, hn * A_V), BF16),
        grid=(b, ng, s // tq),
        in_specs=[
            pl.BlockSpec((None, tq, hb * LANE), lambda bb, h, i: (bb, i, h)),
            pl.BlockSpec((None, tq, hb * LANE), lambda bb, h, i: (bb, i, ng + h)),
            pl.BlockSpec((None, s, 2 * hb * LANE), lambda bb, h, i: (bb, 0, h)),
            pl.BlockSpec((None, s, LANE), lambda bb, h, i: (bb, 0, 0)),
        ],
        out_specs=pl.BlockSpec((None, tq, hb * LANE), lambda bb, h, i: (bb, i, h)),
        scratch_shapes=[
            pltpu.VMEM((hb, s, 2 * LANE), BF16),
            pltpu.VMEM((hb, s, 2 * LANE), BF16),
            pltpu.VMEM((hb, tq, LANE), F32),
            pltpu.VMEM((hb, tq, 2 * LANE), F32),
        ],
        compiler_params=_cp("parallel", "parallel", "arbitrary"),
        name="mla_attention",
    )(qp, qp, kv, krp)


def _mem_attn_kernel(q_ref, kv_ref, o_ref):
    d = MEM_HEAD_DIM
    for h in range(MEM_HEADS):
        q = q_ref[:, h * d:(h + 1) * d]
        k = kv_ref[:, h * d:(h + 1) * d]
        v = kv_ref[:, MEM_WIDTH + h * d:MEM_WIDTH + (h + 1) * d]
        s = _dot_t(q, k) * (d ** -0.5)
        m = jnp.max(s, axis=1, keepdims=True)
        e = jnp.exp(s - m)
        p = e / jnp.sum(e, axis=1, keepdims=True)
        o_ref[:, h * d:(h + 1) * d] = _dot(p.astype(BF16), v).astype(o_ref.dtype)


def _mem_attention(g3, mq_blk, memkv3):
    b, s, _ = g3.shape
    ml = memkv3.shape[1]
    tq = _tile(s, 512)
    return pl.pallas_call(
        _mem_attn_kernel,
        out_shape=jax.ShapeDtypeStruct((b, s, MEM_WIDTH), BF16),
        grid=(b, s // tq),
        in_specs=[
            pl.BlockSpec((None, tq, MEM_WIDTH), lambda bb, i: (bb, i, mq_blk)),
            pl.BlockSpec((None, ml, 2 * MEM_WIDTH), lambda bb, i: (bb, 0, 0)),
        ],
        out_specs=pl.BlockSpec((None, tq, MEM_WIDTH), lambda bb, i: (bb, i, 0)),
        compiler_params=_cp("parallel", "parallel"),
        name="mem_attention",
    )(g3, memkv3)


def _out_kernel(so_ref, mo_ref, g_ref, x_ref, w_ref, o_ref, y_scr):
    @pl.when(pl.program_id(1) == 0)
    def _():
        ws = so_ref.shape[1]
        g = g_ref[...].astype(F32)
        sg = g * (1.0 / (1.0 + jnp.exp(-g)))
        y_scr[:, :ws] = (so_ref[...].astype(F32) * sg[:, :ws]).astype(BF16)
        y_scr[:, ws:] = (mo_ref[...].astype(F32) * sg[:, ws:]).astype(BF16)

    o_ref[...] = x_ref[...] + _dot(y_scr[...], w_ref[...])


def _out_proj(self_out, mem_out, g2, x2d, w_out):
    m, ws = self_out.shape
    wm = mem_out.shape[1]
    kk, n = w_out.shape
    tm = _tile(m, 512)
    tn = _tile(n, 1024)
    return pl.pallas_call(
        _out_kernel,
        out_shape=jax.ShapeDtypeStruct((m, n), F32),
        grid=(m // tm, n // tn),
        in_specs=[
            pl.BlockSpec((tm, ws), lambda i, j: (i, 0)),
            pl.BlockSpec((tm, wm), lambda i, j: (i, 0)),
            pl.BlockSpec((tm, kk), lambda i, j: (i, 0)),
            pl.BlockSpec((tm, tn), lambda i, j: (i, j)),
            pl.BlockSpec((kk, tn), lambda i, j: (0, j)),
        ],
        out_specs=pl.BlockSpec((tm, tn), lambda i, j: (i, j)),
        scratch_shapes=[pltpu.VMEM((tm, kk), BF16)],
        compiler_params=_cp("parallel", "arbitrary"),
        name="gated_out_proj",
    )(self_out, mem_out, g2, x2d, w_out)


def _t5_bucket(rel):
    n = jnp.maximum(rel, 0)
    max_exact = N_BUCKETS // 2
    nf = jnp.maximum(n, 1).astype(F32)
    large = max_exact + (jnp.log(nf / max_exact) / math.log(MAX_DISTANCE / max_exact)
                         * (N_BUCKETS - max_exact)).astype(jnp.int32)
    large = jnp.minimum(large, N_BUCKETS - 1)
    return jnp.where(n < max_exact, n, large)


def _bias_kernel(rb_ref, bd_ref, bp_ref, tswa_ref, tdsa_ref):
    bd = bd_ref[...]
    bp = bp_ref[...]
    for h in range(N_BIAS_HEADS):
        g, j = divmod(h, N_BIAS_HEADS // B_KV_HEADS)

        def body(b, c, h=h):
            d, p = c
            v = rb_ref[b, h]
            return jnp.where(bd == b, v, d), jnp.where(bp == b, v, p)

        z = jnp.zeros((QB, QB), F32)
        d, p = lax.fori_loop(0, N_BUCKETS, body, (z, z))
        far = rb_ref[N_BUCKETS - 1, h]
        rows = pl.ds(j * QB, QB)
        tswa_ref[0, g, rows, :] = d
        tswa_ref[1, g, rows, :] = p
        tdsa_ref[0, g, rows, :] = z
        tdsa_ref[1, g, rows, :] = d - far
        tdsa_ref[2, g, rows, :] = p - far


def _bias_tables(rel_bias):
    t = jnp.arange(QB)[:, None]
    c = jnp.arange(QB)[None, :]
    bd = _t5_bucket(t - c).astype(jnp.int32)
    bp = _t5_bucket(t + QB - c).astype(jnp.int32)
    rows = (N_BIAS_HEADS // B_KV_HEADS) * QB
    return pl.pallas_call(
        _bias_kernel,
        out_shape=(jax.ShapeDtypeStruct((2, B_KV_HEADS, rows, QB), F32),
                   jax.ShapeDtypeStruct((3, B_KV_HEADS, rows, QB), F32)),
        in_specs=[pl.BlockSpec(memory_space=pltpu.SMEM),
                  pl.BlockSpec(memory_space=pltpu.VMEM),
                  pl.BlockSpec(memory_space=pltpu.VMEM)],
        out_specs=(pl.BlockSpec(memory_space=pltpu.VMEM), pl.BlockSpec(memory_space=pltpu.VMEM)),
        compiler_params=pltpu.CompilerParams(vmem_limit_bytes=VMEM_LIMIT),
        name="t5_bias_tables",
    )(rel_bias.astype(F32), bd, bp)


def _stack_heads(ref, g, n_per, d):
    return jnp.concatenate([ref[:, (g * n_per + j) * d:(g * n_per + j + 1) * d] for j in range(n_per)], axis=0)


def _unstack_heads(o, n_per):
    rows = o.shape[0] // n_per
    return jnp.concatenate([o[j * rows:(j + 1) * rows] for j in range(n_per)], axis=1)


def _swa_kernel(sink_ref, q_ref, kp_ref, kc_ref, vp_ref, vc_ref, t_ref, o_ref):
    i = pl.program_id(1)
    n_per = C_HEADS // C_KV_HEADS
    d = C_HEAD_DIM
    rows = n_per * QB
    kb = jnp.concatenate([kp_ref[...], kc_ref[...]], axis=0)
    vb = jnp.concatenate([vp_ref[...], vc_ref[...]], axis=0)
    t_loc = lax.broadcasted_iota(jnp.int32, (rows, 2 * QB), 0) & (QB - 1)
    kj = lax.broadcasted_iota(jnp.int32, (rows, 2 * QB), 1)
    rel = t_loc + QB - kj
    valid = (rel >= 0) & (rel < WINDOW) & ((i * QB - QB + kj) >= 0)
    head = lax.shift_right_logical(lax.broadcasted_iota(jnp.int32, (rows, 1), 0), QB.bit_length() - 1)
    for g in range(C_KV_HEADS):
        qs = _stack_heads(q_ref, g, n_per, d)
        s = _dot_t(qs, kb[:, g * d:(g + 1) * d]) * (d ** -0.5)
        s = s + jnp.concatenate([t_ref[1, g], t_ref[0, g]], axis=1)
        s = jnp.where(valid, s, NEG_INF)
        sink = jnp.zeros((rows, 1), F32)
        for j in range(n_per):
            sink = jnp.where(head == j, sink_ref[0, g * n_per + j], sink)
        m = jnp.maximum(jnp.max(s, axis=1, keepdims=True), sink)
        e = jnp.exp(s - m)
        p = e / (jnp.sum(e, axis=1, keepdims=True) + jnp.exp(sink - m))
        o = _dot(p.astype(BF16), vb[:, g * d:(g + 1) * d])
        o_ref[:, g * n_per * d:(g + 1) * n_per * d] = _unstack_heads(o, n_per).astype(o_ref.dtype)


def _swa_attention(g3, sinks, tswa, q_blk, k_blk, v_blk):
    b, s, _ = g3.shape
    wq = C_HEADS * C_HEAD_DIM
    wk = C_KV_HEADS * C_HEAD_DIM
    prev = lambda i: jnp.maximum(i - 1, 0)
    return pl.pallas_call(
        _swa_kernel,
        out_shape=jax.ShapeDtypeStruct((b, s, wq), BF16),
        grid=(b, s // QB),
        in_specs=[
            pl.BlockSpec(memory_space=pltpu.SMEM),
            pl.BlockSpec((None, QB, wq), lambda bb, i: (bb, i, q_blk)),
            pl.BlockSpec((None, QB, wk), lambda bb, i: (bb, prev(i), k_blk)),
            pl.BlockSpec((None, QB, wk), lambda bb, i: (bb, i, k_blk)),
            pl.BlockSpec((None, QB, wk), lambda bb, i: (bb, prev(i), v_blk)),
            pl.BlockSpec((None, QB, wk), lambda bb, i: (bb, i, v_blk)),
            pl.BlockSpec(tswa.shape, lambda bb, i: (0, 0, 0, 0)),
        ],
        out_specs=pl.BlockSpec((None, QB, wq), lambda bb, i: (bb, i, 0)),
        compiler_params=_cp("parallel", "parallel"),
        name="swa_attention",
    )(sinks.reshape(1, -1).astype(F32), g3, g3, g3, g3, g3, tswa)


def _ordered_to_f32(key):
    bits = jnp.where(key >= 0, key, key ^ jnp.int32(0x7FFFFFFF))
    return lax.bitcast_convert_type(bits, F32)


def _dsa_kernel(q_ref, k_ref, v_ref, iq_ref, ik_ref, iw_ref, t_ref, o_ref,
                sc, madd, iqs, qs, ks, vaug, m_scr, acc_scr, *, s_len, k_top):
    i = pl.program_id(1)
    n_per = B_HEADS // B_KV_HEADS
    d = B_HEAD_DIM

    @pl.when(i == 0)
    def _():
        ones = jnp.ones((s_len, d), BF16)
        for g in range(B_KV_HEADS):
            ks[g] = k_ref[:, g * d:(g + 1) * d]
            vaug[g, :, :d] = v_ref[:, g * d:(g + 1) * d]
            vaug[g, :, d:] = ones

    half = lax.shift_right_logical(i, 1)
    odd = i & 1
    n_t = half + 1
    row = i * QB + lax.broadcasted_iota(jnp.int32, (QB, KT), 0)
    col0 = lax.broadcasted_iota(jnp.int32, (QB, KT), 1)

    for h in range(IDX_HEADS):
        iqs[h] = iq_ref[:, h * IDX_DIM:(h + 1) * IDX_DIM]
    w = iw_ref[...] * (IDX_DIM ** -0.5 * IDX_HEADS ** -0.5)
    wcols = [w[:, h:h + 1] for h in range(IDX_HEADS)]

    def score_tile(kt, c):
        k0 = pl.multiple_of(kt * KT, KT)
        ikt = ik_ref[pl.ds(k0, KT), :]
        acc = jnp.zeros((QB, KT), F32)
        for h in range(IDX_HEADS):
            acc = acc + jnp.maximum(_dot_t(iqs[h], ikt), 0.0) * wcols[h]
        sc[:, pl.ds(k0, KT)] = jnp.where(row >= k0 + col0, acc, NEG_INF)
        return c

    lax.fori_loop(0, n_t, score_tile, 0)

    tail = (s_len - n_t * KT).astype(F32)
    int_min = jnp.int32(-2 ** 31)

    def bisect(b, prefix):
        cand = prefix + lax.shift_left(jnp.int32(1), 31 - b)
        cand_f = _ordered_to_f32(cand)

        def count_tile(kt, c):
            k0 = pl.multiple_of(kt * KT, KT)
            ge = jnp.where(sc[:, pl.ds(k0, KT)] >= cand_f, 1.0, 0.0)
            return c + ge[:, :LANE] + ge[:, LANE:]

        cnt = lax.fori_loop(0, n_t, count_tile, jnp.zeros((QB, LANE), F32))
        total = jnp.sum(cnt, axis=1, keepdims=True) + jnp.where(cand_f <= NEG_INF, tail, 0.0)
        return jnp.where(total >= k_top, cand, prefix)

    thr = _ordered_to_f32(lax.fori_loop(0, 32, bisect, jnp.full((QB, 1), int_min, jnp.int32)))

    def mask_tile(kt, c):
        k0 = pl.multiple_of(kt * KT, KT)
        sel = (sc[:, pl.ds(k0, KT)] >= thr) & (row >= k0 + col0)
        madd[:, pl.ds(k0, KT)] = jnp.where(sel, 0.0, NEG_INF)
        return c

    lax.fori_loop(0, n_t, mask_tile, 0)

    near0 = jnp.where(odd == 1, half, jnp.maximum(half - 1, 0))
    for h in range(B_HEADS):
        qs[h] = q_ref[:, h * d:(h + 1) * d] * (d ** -0.5)
    m_scr[...] = jnp.full(m_scr.shape, -jnp.inf, F32)
    acc_scr[...] = jnp.zeros(acc_scr.shape, F32)

    def step(kt, near):
        k0 = pl.multiple_of(kt * KT, KT)
        mt = madd[:, pl.ds(k0, KT)]
        if near:
            li = jnp.where(odd == 1, 2, jnp.where(kt == half, 1, 0))
            ri = jnp.where(odd == 1, 1, jnp.where(kt == half, 0, 2))
        for g in range(B_KV_HEADS):
            kg = ks[g, pl.ds(k0, KT), :]
            vg = vaug[g, pl.ds(k0, KT), :]
            for j in range(n_per):
                h = g * n_per + j
                s = _dot_t(qs[h], kg) + mt
                if near:
                    rows = pl.ds(j * QB, QB)
                    s = s + jnp.concatenate([t_ref[li, g, rows, :], t_ref[ri, g, rows, :]], axis=1)
                m_prev = m_scr[h]
                m_new = jnp.maximum(m_prev, jnp.max(s, axis=1, keepdims=True))
                alpha = jnp.exp(m_prev - m_new)
                p = jnp.exp(s - _rep2(m_new))
                acc_scr[h] = alpha * acc_scr[h] + _dot(p.astype(BF16), vg)
                m_scr[h] = m_new

    def far_body(kt, c):
        step(kt, False)
        return c

    def near_body(kt, c):
        step(kt, True)
        return c

    lax.fori_loop(0, near0, far_body, 0)
    lax.fori_loop(near0, n_t, near_body, 0)
    outs = []
    for h in range(B_HEADS):
        acc = acc_scr[h]
        outs.append(acc[:, :d] / acc[:, d:])
    o_ref[...] = jnp.concatenate(outs, axis=1).astype(o_ref.dtype)


def _dsa_attention(g3, ik3, iw3, tdsa, q_blk, k_blk, v_blk, iq_blk):
    b, s, _ = g3.shape
    k_top = min(IDX_TOPK_MAX, s // 4)
    wq = B_HEADS * B_HEAD_DIM
    wk = B_KV_HEADS * B_HEAD_DIM
    wi = IDX_HEADS * IDX_DIM
    rows = (B_HEADS // B_KV_HEADS) * QB
    return pl.pallas_call(
        functools.partial(_dsa_kernel, s_len=s, k_top=k_top),
        out_shape=jax.ShapeDtypeStruct((b, s, wq), BF16),
        grid=(b, s // QB),
        in_specs=[
            pl.BlockSpec((None, QB, wq), lambda bb, i: (bb, i, q_blk)),
            pl.BlockSpec((None, s, wk), lambda bb, i: (bb, 0, k_blk)),
            pl.BlockSpec((None, s, wk), lambda bb, i: (bb, 0, v_blk)),
            pl.BlockSpec((None, QB, wi), lambda bb, i: (bb, i, iq_blk)),
            pl.BlockSpec((None, s, IDX_DIM), lambda bb, i: (bb, 0, 0)),
            pl.BlockSpec((None, QB, IDX_HEADS), lambda bb, i: (bb, i, 0)),
            pl.BlockSpec(tdsa.shape, lambda bb, i: (0, 0, 0, 0)),
        ],
        out_specs=pl.BlockSpec((None, QB, wq), lambda bb, i: (bb, i, 0)),
        scratch_shapes=[
            pltpu.VMEM((QB, s), F32),
            pltpu.VMEM((QB, s), F32),
            pltpu.VMEM((IDX_HEADS, QB, IDX_DIM), BF16),
            pltpu.VMEM((B_HEADS, QB, B_HEAD_DIM), BF16),
            pltpu.VMEM((B_KV_HEADS, s, B_HEAD_DIM), BF16),
            pltpu.VMEM((B_KV_HEADS, s, 2 * B_HEAD_DIM), BF16),
            pltpu.VMEM((B_HEADS, QB, LANE), F32),
            pltpu.VMEM((B_HEADS, QB, 2 * B_HEAD_DIM), F32),
        ],
        compiler_params=_cp("parallel", "arbitrary"),
        name="dsa_attention",
    )(g3, g3, g3, g3, ik3, iw3, tdsa)


def _mla_layer(h, w_in, q_norm, w_uq, kv_norm, w_ukv, cos_t, sin_t, b, s):
    o_ckv = A_Q_LORA + A_KV_LORA
    o_mq = o_ckv + A_ROPE
    o_gate = o_mq + MEM_WIDTH
    w_f = w_in[:, :o_ckv].astype(BF16)
    w_kr = w_in[:, o_ckv:o_mq]
    w_kr2 = jnp.concatenate([w_kr, _swap_halves(w_kr)], axis=1).astype(BF16)
    w_g = jnp.concatenate([w_in[:, o_gate:], w_in[:, o_mq:o_gate]], axis=1).astype(BF16)
    cqkv = _matmul(h, w_f, F32)
    kr2 = _matmul(h, w_kr2, F32)
    g2 = _matmul(h, w_g, BF16)

    hd = A_NOPE + A_ROPE
    wq3 = w_uq.reshape(A_Q_LORA, A_HEADS, hd)
    w_r = wq3[:, :, A_NOPE:]
    w_uq2 = jnp.concatenate([
        wq3[:, :, :A_NOPE].reshape(A_Q_LORA, A_HEADS * A_NOPE),
        jnp.concatenate([w_r, _swap_halves(w_r)], axis=-1).reshape(A_Q_LORA, A_HEADS * 2 * A_ROPE),
    ], axis=1).astype(BF16)
    qp = _uq_proj(cqkv, q_norm, w_uq2, cos_t, sin_t, s)
    kv, krp = _ukv_proj(cqkv, kr2, kv_norm, w_ukv.astype(BF16), cos_t, sin_t, s)
    self_out = _mla_attention(qp.reshape(b, s, -1), kv.reshape(b, s, -1), krp.reshape(b, s, -1), b, s)
    return self_out.reshape(b * s, -1), g2, 3


def _dsa_layer(h, w_in, tdsa, b, s):
    wq = B_HEADS * B_HEAD_DIM
    wk = B_KV_HEADS * B_HEAD_DIM
    wi = IDX_HEADS * IDX_DIM
    o_k = wq
    o_v = o_k + wk
    o_iq = o_v + wk
    o_ik = o_iq + wi
    o_iw = o_ik + IDX_DIM
    o_mq = o_iw + IDX_HEADS
    o_gate = o_mq + MEM_WIDTH
    w_g = jnp.concatenate([w_in[:, o_gate:], w_in[:, o_mq:o_gate], w_in[:, :o_k], w_in[:, o_iq:o_ik],
                           w_in[:, o_k:o_v], w_in[:, o_v:o_iq]], axis=1).astype(BF16)
    w_i = w_in[:, o_ik:o_mq].astype(BF16)
    g2 = _matmul(h, w_g, BF16)
    ii = _matmul(h, w_i, F32)
    ik3 = ii[:, :IDX_DIM].astype(BF16).reshape(b, s, IDX_DIM)
    iw3 = ii[:, IDX_DIM:].reshape(b, s, IDX_HEADS)
    self_out = _dsa_attention(g2.reshape(b, s, -1), ik3, iw3, tdsa,
                              q_blk=2, k_blk=7168 // wk, v_blk=7424 // wk, iq_blk=6)
    return self_out.reshape(b * s, -1), g2, 3


def _swa_layer(h, w_in, sinks, tswa, b, s):
    wq = C_HEADS * C_HEAD_DIM
    wk = C_KV_HEADS * C_HEAD_DIM
    o_k = wq
    o_v = o_k + wk
    o_mq = o_v + wk
    o_gate = o_mq + MEM_WIDTH
    w_g = jnp.concatenate([w_in[:, o_gate:], w_in[:, o_mq:o_gate], w_in[:, :o_k],
                           w_in[:, o_k:o_v], w_in[:, o_v:o_mq]], axis=1).astype(BF16)
    g2 = _matmul(h, w_g, BF16)
    self_out = _swa_attention(g2.reshape(b, s, -1), sinks, tswa, q_blk=2, k_blk=6144 // wk, v_blk=6400 // wk)
    return self_out.reshape(b * s, -1), g2, 3


def kernel(x, mem, norm_in, final_norm, mem_norm, rel_bias, w_in_a, a_q_norm, w_uq, a_kv_norm, w_ukv,
           w_in_b, w_in_c, c_sinks, w_mem_kv, w_out):
    b, s, d = x.shape
    depth = norm_in.shape[0]
    ml = mem.shape[1]
    xf = x.reshape(b * s, d)
    mem_n = _rmsnorm(mem.reshape(b * ml, d), mem_norm, BF16)
    cos_t, sin_t = _rope_tables(s)
    tswa, tdsa = _bias_tables(rel_bias)
    for i in range(depth):
        h = _rmsnorm(xf, norm_in[i], BF16)
        kind, j = i % N_MIXERS, i // N_MIXERS
        if kind == 0:
            self_out, g2, mq_blk = _mla_layer(h, w_in_a[j], a_q_norm[j], w_uq[j], a_kv_norm[j], w_ukv[j],
                                              cos_t, sin_t, b, s)
        elif kind == 1:
            self_out, g2, mq_blk = _dsa_layer(h, w_in_b[j], tdsa, b, s)
        else:
            self_out, g2, mq_blk = _swa_layer(h, w_in_c[j], c_sinks[j], tswa, b, s)
        memkv = _matmul(mem_n, w_mem_kv[i].astype(BF16), BF16)
        mem_out = _mem_attention(g2.reshape(b, s, -1), mq_blk, memkv.reshape(b, ml, -1))
        xf = _out_proj(self_out, mem_out.reshape(b * s, -1), g2, xf, w_out[i].astype(BF16))
    return _rmsnorm(xf, final_norm, F32).reshape(b, s, d)
```

```python
import functools
import math

import jax
import jax.numpy as jnp
from jax import lax
from jax.experimental import pallas as pl
from jax.experimental.pallas import tpu as pltpu

F32 = jnp.float32
BF16 = jnp.bfloat16

EPS = 1e-6
NEG_INF = -1e30
N_MIXERS = 3

N_BUCKETS = 32
MAX_DISTANCE = 128
N_BIAS_HEADS = 32

MEM_HEADS = 4
MEM_HEAD_DIM = 256
MEM_WIDTH = MEM_HEADS * MEM_HEAD_DIM

A_HEADS = 16
A_Q_LORA = 1536
A_KV_LORA = 512
A_NOPE = 128
A_ROPE = 64
A_V = 128
ROPE_THETA = 10000.0

B_HEADS = 32
B_KV_HEADS = 4
B_HEAD_DIM = 64
IDX_HEADS = 16
IDX_DIM = 64
IDX_TOPK_MAX = 256

C_HEADS = 32
C_KV_HEADS = 4
C_HEAD_DIM = 64
WINDOW = 128

LANE = 128
QB = 128
KT = 256
VMEM_LIMIT = 56 * 1024 * 1024


def _cp(*sem):
    return pltpu.CompilerParams(dimension_semantics=sem, vmem_limit_bytes=VMEM_LIMIT)


def _tile(n, pref):
    t = min(n, pref)
    assert n % t == 0, (n, pref)
    return t


def _dot(a, b):
    return jnp.dot(a, b, preferred_element_type=F32)


def _dot_t(a, b):
    return lax.dot_general(a, b, (((1,), (1,)), ((), ())), preferred_element_type=F32)


def _rms_kernel(x_ref, g_ref, o_ref):
    x = x_ref[...].astype(F32)
    y = x * lax.rsqrt(jnp.mean(x * x, axis=-1, keepdims=True) + EPS)
    o_ref[...] = (y * g_ref[...]).astype(o_ref.dtype)


def _rmsnorm(x2d, g, out_dtype):
    m, d = x2d.shape
    tm = _tile(m, 512)
    return pl.pallas_call(
        _rms_kernel,
        out_shape=jax.ShapeDtypeStruct((m, d), out_dtype),
        grid=(m // tm,),
        in_specs=[pl.BlockSpec((tm, d), lambda i: (i, 0)), pl.BlockSpec((1, d), lambda i: (0, 0))],
        out_specs=pl.BlockSpec((tm, d), lambda i: (i, 0)),
        compiler_params=_cp("parallel"),
        name="rmsnorm",
    )(x2d, g.reshape(1, d).astype(F32))


def _mm_kernel(a_ref, w_ref, o_ref):
    o_ref[...] = _dot(a_ref[...], w_ref[...]).astype(o_ref.dtype)


def _matmul(a, w, out_dtype):
    m, k = a.shape
    n = w.shape[1]
    tm = _tile(m, 1024)
    tn = _tile(n, 512) if n % 512 == 0 else n
    return pl.pallas_call(
        _mm_kernel,
        out_shape=jax.ShapeDtypeStruct((m, n), out_dtype),
        grid=(m // tm, n // tn),
        in_specs=[pl.BlockSpec((tm, k), lambda i, j: (i, 0)), pl.BlockSpec((k, tn), lambda i, j: (0, j))],
        out_specs=pl.BlockSpec((tm, tn), lambda i, j: (i, j)),
        compiler_params=_cp("parallel", "arbitrary"),
        name="matmul",
    )(a, w)


def _rope_tables(s):
    d = A_ROPE
    inv = 1.0 / (ROPE_THETA ** (jnp.arange(0, d, 2, dtype=F32) / d))
    ang = jnp.arange(s, dtype=F32)[:, None] * inv[None, :]
    cos, sin = jnp.cos(ang), jnp.sin(ang)
    z = jnp.zeros((s, LANE - d), F32)
    return jnp.concatenate([cos, cos, z], axis=1), jnp.concatenate([-sin, sin, z], axis=1)


def _swap_halves(w):
    h = w.shape[-1] // 2
    return jnp.concatenate([w[..., h:], w[..., :h]], axis=-1)


def _uq_kernel(cq_ref, g_ref, w_ref, cos_ref, sin_ref, o_ref, *, n_nope):
    x = cq_ref[...]
    y = x * lax.rsqrt(jnp.mean(x * x, axis=-1, keepdims=True) + EPS)
    h = (y * g_ref[...]).astype(BF16)
    n = w_ref.shape[1]
    cw = 512
    for c in range(0, n_nope, cw):
        o_ref[:, c:c + cw] = _dot(h, w_ref[:, c:c + cw]).astype(BF16)
    cos = cos_ref[...]
    sin = sin_ref[...]
    for c in range(n_nope, n, cw):
        y = _dot(h, w_ref[:, c:c + cw])
        parts = []
        for k in range(cw // LANE):
            slab = y[:, k * LANE:(k + 1) * LANE]
            parts.append(slab * cos + pltpu.roll(slab, LANE // 2, 1) * sin)
        o_ref[:, c:c + cw] = jnp.concatenate(parts, axis=1).astype(BF16)


def _uq_proj(cqkv, q_norm, w_uq2, cos_t, sin_t, s):
    m = cqkv.shape[0]
    n = w_uq2.shape[1]
    tm = _tile(s, 512)
    nblk = s // tm
    return pl.pallas_call(
        functools.partial(_uq_kernel, n_nope=A_HEADS * A_NOPE),
        out_shape=jax.ShapeDtypeStruct((m, n), BF16),
        grid=(m // tm,),
        in_specs=[
            pl.BlockSpec((tm, A_Q_LORA), lambda i: (i, 0)),
            pl.BlockSpec((1, A_Q_LORA), lambda i: (0, 0)),
            pl.BlockSpec((A_Q_LORA, n), lambda i: (0, 0)),
            pl.BlockSpec((tm, LANE), lambda i: (i % nblk, 0)),
            pl.BlockSpec((tm, LANE), lambda i: (i % nblk, 0)),
        ],
        out_specs=pl.BlockSpec((tm, n), lambda i: (i, 0)),
        compiler_params=_cp("parallel"),
        name="mla_q_up",
    )(cqkv, q_norm.reshape(1, -1).astype(F32), w_uq2, cos_t, sin_t)


def _ukv_kernel(ckv_ref, kr_ref, g_ref, w_ref, cos_ref, sin_ref, kv_ref, krp_ref):
    x = ckv_ref[...]
    y = x * lax.rsqrt(jnp.mean(x * x, axis=-1, keepdims=True) + EPS)
    h = (y * g_ref[...]).astype(BF16)
    n = w_ref.shape[1]
    cw = 512
    for c in range(0, n, cw):
        kv_ref[:, c:c + cw] = _dot(h, w_ref[:, c:c + cw]).astype(BF16)
    slab = kr_ref[...]
    krp_ref[...] = (slab * cos_ref[...] + pltpu.roll(slab, LANE // 2, 1) * sin_ref[...]).astype(BF16)


def _ukv_proj(cqkv, kr2, kv_norm, w_ukv, cos_t, sin_t, s):
    m = cqkv.shape[0]
    n = w_ukv.shape[1]
    tm = _tile(s, 512)
    nblk = s // tm
    cblk = A_Q_LORA // A_KV_LORA
    return pl.pallas_call(
        _ukv_kernel,
        out_shape=(jax.ShapeDtypeStruct((m, n), BF16), jax.ShapeDtypeStruct((m, LANE), BF16)),
        grid=(m // tm,),
        in_specs=[
            pl.BlockSpec((tm, A_KV_LORA), lambda i: (i, cblk)),
            pl.BlockSpec((tm, LANE), lambda i: (i, 0)),
            pl.BlockSpec((1, A_KV_LORA), lambda i: (0, 0)),
            pl.BlockSpec((A_KV_LORA, n), lambda i: (0, 0)),
            pl.BlockSpec((tm, LANE), lambda i: (i % nblk, 0)),
            pl.BlockSpec((tm, LANE), lambda i: (i % nblk, 0)),
        ],
        out_specs=(pl.BlockSpec((tm, n), lambda i: (i, 0)), pl.BlockSpec((tm, LANE), lambda i: (i, 0))),
        compiler_params=_cp("parallel"),
        name="mla_kv_up",
    )(cqkv, kr2, kv_norm.reshape(1, -1).astype(F32), w_ukv, cos_t, sin_t)


def _rep2(x):
    return jnp.concatenate([x, x], axis=1)


def _silu(g):
    return g * (1.0 / (1.0 + jnp.exp(-g)))


def _mla_attn_kernel(qn_ref, qr_ref, kv_ref, kr_ref, g_ref, o_ref, kcat, vaug, m_scr, acc_scr, *, tq, scale, hb):
    i = pl.program_id(2)

    @pl.when(i == 0)
    def _():
        ones = jnp.ones((kv_ref.shape[0], LANE), BF16)
        for hh in range(hb):
            kcat[hh, :, :LANE] = kv_ref[:, 2 * hh * LANE:(2 * hh + 1) * LANE]
            kcat[hh, :, LANE:] = kr_ref[...]
            vaug[hh, :, :LANE] = kv_ref[:, (2 * hh + 1) * LANE:(2 * hh + 2) * LANE]
            vaug[hh, :, LANE:] = ones

    m_scr[...] = jnp.full(m_scr.shape, -jnp.inf, F32)
    acc_scr[...] = jnp.zeros(acc_scr.shape, F32)
    qs = [jnp.concatenate([qn_ref[:, hh * LANE:(hh + 1) * LANE], qr_ref[:, hh * LANE:(hh + 1) * LANE]], axis=1)
          for hh in range(hb)]

    n_wide = lax.shift_right_logical(i, 1)

    def step(k0, tk, masked):
        for hh in range(hb):
            s = _dot_t(qs[hh], kcat[hh, pl.ds(k0, tk), :]) * scale
            if masked:
                row = lax.broadcasted_iota(jnp.int32, s.shape, 0)
                col = lax.broadcasted_iota(jnp.int32, s.shape, 1)
                s = jnp.where(row >= col, s, NEG_INF)
            m_prev = m_scr[hh]
            m_new = jnp.maximum(m_prev, jnp.max(s, axis=1, keepdims=True))
            alpha = jnp.exp(m_prev - m_new)
            p = jnp.exp(s - jnp.concatenate([m_new] * (tk // LANE), axis=1))
            acc_scr[hh] = _rep2(alpha) * acc_scr[hh] + _dot(p.astype(BF16), vaug[hh, pl.ds(k0, tk), :])
            m_scr[hh] = m_new

    def body(j, c):
        step(pl.multiple_of(j * 2 * tq, 2 * tq), 2 * tq, False)
        return c

    lax.fori_loop(0, n_wide, body, 0)

    @pl.when((i & 1) == 1)
    def _():
        step(pl.multiple_of((i - 1) * tq, tq), tq, False)

    step(pl.multiple_of(i * tq, tq), tq, True)
    for hh in range(hb):
        acc = acc_scr[hh]
        cols = slice(hh * LANE, (hh + 1) * LANE)
        gate = _silu(g_ref[:, cols].astype(F32))
        o_ref[:, cols] = (acc[:, :LANE] / acc[:, LANE:] * gate).astype(o_ref.dtype)


def _mla_attention(qp, kv, krp, g3, b, s):
    tq = _tile(s, 256)
    hn = A_HEADS
    hb = 4
    ng = hn // hb
    return pl.pallas_call(
        functools.partial(_mla_attn_kernel, tq=tq, scale=(A_NOPE + A_ROPE) ** -0.5, hb=hb),
        out_shape=jax.ShapeDtypeStruct((b, s, hn * A_V), BF16),
        grid=(b, ng, s // tq),
        in_specs=[
            pl.BlockSpec((None, tq, hb * LANE), lambda bb, h, i: (bb, i, h)),
            pl.BlockSpec((None, tq, hb * LANE), lambda bb, h, i: (bb, i, ng + h)),
            pl.BlockSpec((None, s, 2 * hb * LANE), lambda bb, h, i: (bb, 0, h)),
            pl.BlockSpec((None, s, LANE), lambda bb, h, i: (bb, 0, 0)),
            pl.BlockSpec((None, tq, hb * LANE), lambda bb, h, i: (bb, i, h)),
        ],
        out_specs=pl.BlockSpec((None, tq, hb * LANE), lambda bb, h, i: (bb, i, h)),
        scratch_shapes=[
            pltpu.VMEM((hb, s, 2 * LANE), BF16),
            pltpu.VMEM((hb, s, 2 * LANE), BF16),
            pltpu.VMEM((hb, tq, LANE), F32),
            pltpu.VMEM((hb, tq, 2 * LANE), F32),
        ],
        compiler_params=_cp("parallel", "parallel", "arbitrary"),
        name="mla_attention",
    )(qp, qp, kv, krp, g3)


def _mem_attn_kernel(q_ref, kv_ref, g_ref, o_ref):
    d = MEM_HEAD_DIM
    for h in range(MEM_HEADS):
        cols = slice(h * d, (h + 1) * d)
        k = kv_ref[:, cols]
        v = kv_ref[:, MEM_WIDTH + h * d:MEM_WIDTH + (h + 1) * d]
        s = _dot_t(q_ref[:, cols], k) * (d ** -0.5)
        m = jnp.max(s, axis=1, keepdims=True)
        e = jnp.exp(s - m)
        p = e / jnp.sum(e, axis=1, keepdims=True)
        o = _dot(p.astype(BF16), v) * _silu(g_ref[:, cols].astype(F32))
        o_ref[:, cols] = o.astype(o_ref.dtype)


def _mem_attention(g3, mq_blk, gate_blk, memkv3):
    b, s, _ = g3.shape
    ml = memkv3.shape[1]
    tq = _tile(s, 512)
    return pl.pallas_call(
        _mem_attn_kernel,
        out_shape=jax.ShapeDtypeStruct((b, s, MEM_WIDTH), BF16),
        grid=(b, s // tq),
        in_specs=[
            pl.BlockSpec((None, tq, MEM_WIDTH), lambda bb, i: (bb, i, mq_blk)),
            pl.BlockSpec((None, ml, 2 * MEM_WIDTH), lambda bb, i: (bb, 0, 0)),
            pl.BlockSpec((None, tq, MEM_WIDTH), lambda bb, i: (bb, i, gate_blk)),
        ],
        out_specs=pl.BlockSpec((None, tq, MEM_WIDTH), lambda bb, i: (bb, i, 0)),
        compiler_params=_cp("parallel", "parallel"),
        name="mem_attention",
    )(g3, memkv3, g3)


def _out_kernel(ys_ref, ym_ref, x_ref, w_ref, g_ref, *refs, keep_x):
    if keep_x:
        xo_ref, h_ref, xrow = refs
    else:
        h_ref, xrow = refs
    j = pl.program_id(1)
    ws = ys_ref.shape[1]
    tn = x_ref.shape[1]
    xn = x_ref[...] + _dot(ys_ref[...], w_ref[:ws, :]) + _dot(ym_ref[...], w_ref[ws:, :])
    xrow[:, pl.ds(pl.multiple_of(j * tn, tn), tn)] = xn
    if keep_x:
        xo_ref[...] = xn

    @pl.when(j == pl.num_programs(1) - 1)
    def _():
        x = xrow[...]
        y = x * lax.rsqrt(jnp.mean(x * x, axis=-1, keepdims=True) + EPS)
        h_ref[...] = (y * g_ref[...]).astype(h_ref.dtype)


def _out_proj(ys, ym, x2d, w_out, g_next, keep_x):
    m, ws = ys.shape
    wm = ym.shape[1]
    kk, n = w_out.shape
    tm = _tile(m, 1024)
    tn = _tile(n, 512)
    h_dtype = BF16 if keep_x else F32
    row_spec = pl.BlockSpec((tm, n), lambda i, j: (i, 0))
    tile_spec = pl.BlockSpec((tm, tn), lambda i, j: (i, j))
    out_shape = [jax.ShapeDtypeStruct((m, n), h_dtype)]
    out_specs = [row_spec]
    if keep_x:
        out_shape.insert(0, jax.ShapeDtypeStruct((m, n), F32))
        out_specs.insert(0, tile_spec)
    return pl.pallas_call(
        functools.partial(_out_kernel, keep_x=keep_x),
        out_shape=tuple(out_shape),
        grid=(m // tm, n // tn),
        in_specs=[
            pl.BlockSpec((tm, ws), lambda i, j: (i, 0)),
            pl.BlockSpec((tm, wm), lambda i, j: (i, 0)),
            tile_spec,
            pl.BlockSpec((kk, tn), lambda i, j: (0, j)),
            pl.BlockSpec((1, n), lambda i, j: (0, 0)),
        ],
        out_specs=tuple(out_specs),
        scratch_shapes=[pltpu.VMEM((tm, n), F32)],
        compiler_params=_cp("parallel", "arbitrary"),
        name="out_proj_norm",
    )(ys, ym, x2d, w_out, g_next.reshape(1, n).astype(F32))


def _t5_bucket(rel):
    n = jnp.maximum(rel, 0)
    max_exact = N_BUCKETS // 2
    nf = jnp.maximum(n, 1).astype(F32)
    large = max_exact + (jnp.log(nf / max_exact) / math.log(MAX_DISTANCE / max_exact)
                         * (N_BUCKETS - max_exact)).astype(jnp.int32)
    large = jnp.minimum(large, N_BUCKETS - 1)
    return jnp.where(n < max_exact, n, large)


def _bias_kernel(rb_ref, bd_ref, bp_ref, tswa_ref, tdsa_ref):
    bd = bd_ref[...]
    bp = bp_ref[...]
    for h in range(N_BIAS_HEADS):
        g, j = divmod(h, N_BIAS_HEADS // B_KV_HEADS)

        def body(b, c, h=h):
            d, p = c
            v = rb_ref[b, h]
            return jnp.where(bd == b, v, d), jnp.where(bp == b, v, p)

        z = jnp.zeros((QB, QB), F32)
        d, p = lax.fori_loop(0, N_BUCKETS, body, (z, z))
        far = rb_ref[N_BUCKETS - 1, h]
        rows = pl.ds(j * QB, QB)
        t = lax.broadcasted_iota(jnp.int32, (QB, QB), 0)
        c = lax.broadcasted_iota(jnp.int32, (QB, QB), 1)
        tswa_ref[0, g, rows, :] = jnp.where(t >= c, d, NEG_INF)
        tswa_ref[1, g, rows, :] = jnp.where(t + QB - c < WINDOW, p, NEG_INF)
        tdsa_ref[0, g, rows, :] = z
        tdsa_ref[1, g, rows, :] = d - far
        tdsa_ref[2, g, rows, :] = p - far


def _bias_tables(rel_bias):
    t = jnp.arange(QB)[:, None]
    c = jnp.arange(QB)[None, :]
    bd = _t5_bucket(t - c).astype(jnp.int32)
    bp = _t5_bucket(t + QB - c).astype(jnp.int32)
    rows = (N_BIAS_HEADS // B_KV_HEADS) * QB
    return pl.pallas_call(
        _bias_kernel,
        out_shape=(jax.ShapeDtypeStruct((2, B_KV_HEADS, rows, QB), F32),
                   jax.ShapeDtypeStruct((3, B_KV_HEADS, rows, QB), F32)),
        in_specs=[pl.BlockSpec(memory_space=pltpu.SMEM),
                  pl.BlockSpec(memory_space=pltpu.VMEM),
                  pl.BlockSpec(memory_space=pltpu.VMEM)],
        out_specs=(pl.BlockSpec(memory_space=pltpu.VMEM), pl.BlockSpec(memory_space=pltpu.VMEM)),
        compiler_params=pltpu.CompilerParams(vmem_limit_bytes=VMEM_LIMIT),
        name="t5_bias_tables",
    )(rel_bias.astype(F32), bd, bp)


def _swa_kernel(sink_ref, q_ref, kp_ref, kc_ref, vp_ref, vc_ref, t_ref, g_ref, o_ref):
    i = pl.program_id(1)
    n_per = C_HEADS // C_KV_HEADS
    d = C_HEAD_DIM
    kb = jnp.concatenate([kp_ref[...], kc_ref[...]], axis=0)
    vb = jnp.concatenate([vp_ref[...], vc_ref[...]], axis=0)
    ones = jnp.ones((2 * QB, d), BF16)
    no_prev = jnp.where(i == 0, NEG_INF, 0.0).astype(F32)
    low = lax.broadcasted_iota(jnp.int32, (QB, LANE), 1) < d
    outs = []
    for g in range(C_KV_HEADS):
        kg = kb[:, g * d:(g + 1) * d]
        vg = vb[:, g * d:(g + 1) * d]
        vaug = (jnp.concatenate([vg, ones, ones, vg], axis=1), jnp.concatenate([ones, vg, vg, ones], axis=1))
        zk = jnp.zeros_like(kg)
        kpar = (jnp.concatenate([kg, zk], axis=1), jnp.concatenate([zk, kg], axis=1))
        for jp in range(n_per // 2):
            pair = []
            q2 = q_ref[:, (g * n_per + 2 * jp) * d:(g * n_per + 2 * jp + 2) * d]
            for par in range(2):
                j = 2 * jp + par
                h = g * n_per + j
                rows = pl.ds(j * QB, QB)
                s = _dot_t(q2, kpar[par]) * (d ** -0.5)
                s = s + jnp.concatenate([t_ref[1, g, rows, :] + no_prev, t_ref[0, g, rows, :]], axis=1)
                sink = sink_ref[0, h]
                m = jnp.maximum(jnp.broadcast_to(jnp.max(s, axis=1, keepdims=True), (QB, LANE)), sink)
                e = jnp.exp(s - _rep2(m))
                acc = _dot(e.astype(BF16), vaug[par])
                pair.append(acc[:, :LANE] / (acc[:, LANE:] + jnp.exp(sink - m)))
            outs.append(jnp.where(low, pair[0], pair[1]))
    o = jnp.concatenate(outs, axis=1) * _silu(g_ref[...].astype(F32))
    o_ref[...] = o.astype(o_ref.dtype)


def _swa_attention(g3, sinks, tswa, q_blk, k_blk, v_blk):
    b, s, _ = g3.shape
    wq = C_HEADS * C_HEAD_DIM
    wk = C_KV_HEADS * C_HEAD_DIM
    prev = lambda i: jnp.maximum(i - 1, 0)
    return pl.pallas_call(
        _swa_kernel,
        out_shape=jax.ShapeDtypeStruct((b, s, wq), BF16),
        grid=(b, s // QB),
        in_specs=[
            pl.BlockSpec(memory_space=pltpu.SMEM),
            pl.BlockSpec((None, QB, wq), lambda bb, i: (bb, i, q_blk)),
            pl.BlockSpec((None, QB, wk), lambda bb, i: (bb, prev(i), k_blk)),
            pl.BlockSpec((None, QB, wk), lambda bb, i: (bb, i, k_blk)),
            pl.BlockSpec((None, QB, wk), lambda bb, i: (bb, prev(i), v_blk)),
            pl.BlockSpec((None, QB, wk), lambda bb, i: (bb, i, v_blk)),
            pl.BlockSpec(tswa.shape, lambda bb, i: (0, 0, 0, 0)),
            pl.BlockSpec((None, QB, wq), lambda bb, i: (bb, i, 0)),
        ],
        out_specs=pl.BlockSpec((None, QB, wq), lambda bb, i: (bb, i, 0)),
        compiler_params=_cp("parallel", "parallel"),
        name="swa_attention",
    )(sinks.reshape(1, -1).astype(F32), g3, g3, g3, g3, g3, tswa, g3)


def _ordered_to_f32(key):
    bits = jnp.where(key >= 0, key, key ^ jnp.int32(0x7FFFFFFF))
    return lax.bitcast_convert_type(bits, F32)


def _dsa_kernel(q_ref, k_ref, v_ref, iq_ref, ik_ref, iw_ref, t_ref, g_ref, o_ref,
                sc, madd, iqs, qs, ks, vaug, m_scr, acc_scr, *, s_len, k_top):
    i = pl.program_id(1)
    n_per = B_HEADS // B_KV_HEADS
    d = B_HEAD_DIM

    @pl.when(i == 0)
    def _():
        ones = jnp.ones((s_len, d), BF16)
        for g in range(B_KV_HEADS):
            ks[g] = k_ref[:, g * d:(g + 1) * d]
            vaug[g, :, :d] = v_ref[:, g * d:(g + 1) * d]
            vaug[g, :, d:] = ones

    half = lax.shift_right_logical(i, 1)
    odd = i & 1
    n_t = half + 1
    row = i * QB + lax.broadcasted_iota(jnp.int32, (QB, KT), 0)
    col0 = lax.broadcasted_iota(jnp.int32, (QB, KT), 1)

    for h in range(IDX_HEADS):
        iqs[h] = iq_ref[:, h * IDX_DIM:(h + 1) * IDX_DIM]
    w = iw_ref[...] * (IDX_DIM ** -0.5 * IDX_HEADS ** -0.5)
    wcols = [w[:, h:h + 1] for h in range(IDX_HEADS)]

    def score_tile(kt, c):
        k0 = pl.multiple_of(kt * KT, KT)
        ikt = ik_ref[pl.ds(k0, KT), :]
        acc = jnp.zeros((QB, KT), F32)
        for h in range(IDX_HEADS):
            acc = acc + jnp.maximum(_dot_t(iqs[h], ikt), 0.0) * wcols[h]
        sc[:, pl.ds(k0, KT)] = jnp.where(row >= k0 + col0, acc, NEG_INF)
        return c

    lax.fori_loop(0, n_t, score_tile, 0)

    tail = (s_len - n_t * KT).astype(F32)
    int_min = jnp.int32(-2 ** 31)

    def bisect(b, prefix):
        cand = prefix + lax.shift_left(jnp.int32(1), 31 - b)
        cand_f = _ordered_to_f32(cand)

        def count_tile(kt, c):
            k0 = pl.multiple_of(kt * KT, KT)
            ge = jnp.where(sc[:, pl.ds(k0, KT)] >= cand_f, 1.0, 0.0)
            return c + ge[:, :LANE] + ge[:, LANE:]

        cnt = lax.fori_loop(0, n_t, count_tile, jnp.zeros((QB, LANE), F32))
        total = jnp.sum(cnt, axis=1, keepdims=True) + jnp.where(cand_f <= NEG_INF, tail, 0.0)
        return jnp.where(total >= k_top, cand, prefix)

    thr = _ordered_to_f32(lax.fori_loop(0, 32, bisect, jnp.full((QB, 1), int_min, jnp.int32)))

    def mask_tile(kt, c):
        k0 = pl.multiple_of(kt * KT, KT)
        sel = (sc[:, pl.ds(k0, KT)] >= thr) & (row >= k0 + col0)
        madd[:, pl.ds(k0, KT)] = jnp.where(sel, 0.0, NEG_INF)
        return c

    lax.fori_loop(0, n_t, mask_tile, 0)

    near0 = jnp.where(odd == 1, half, jnp.maximum(half - 1, 0))
    for h in range(B_HEADS):
        qs[h] = q_ref[:, h * d:(h + 1) * d] * (d ** -0.5)
    m_scr[...] = jnp.full(m_scr.shape, -jnp.inf, F32)
    acc_scr[...] = jnp.zeros(acc_scr.shape, F32)

    def step(kt, near):
        k0 = pl.multiple_of(kt * KT, KT)
        mt = madd[:, pl.ds(k0, KT)]
        if near:
            li = jnp.where(odd == 1, 2, jnp.where(kt == half, 1, 0))
            ri = jnp.where(odd == 1, 1, jnp.where(kt == half, 0, 2))
        for g in range(B_KV_HEADS):
            kg = ks[g, pl.ds(k0, KT), :]
            vg = vaug[g, pl.ds(k0, KT), :]
            for j in range(n_per):
                h = g * n_per + j
                s = _dot_t(qs[h], kg) + mt
                if near:
                    rows = pl.ds(j * QB, QB)
                    s = s + jnp.concatenate([t_ref[li, g, rows, :], t_ref[ri, g, rows, :]], axis=1)
                m_prev = m_scr[h]
                m_new = jnp.maximum(m_prev, jnp.max(s, axis=1, keepdims=True))
                alpha = jnp.exp(m_prev - m_new)
                p = jnp.exp(s - _rep2(m_new))
                acc_scr[h] = alpha * acc_scr[h] + _dot(p.astype(BF16), vg)
                m_scr[h] = m_new

    def far_body(kt, c):
        step(kt, False)
        return c

    def near_body(kt, c):
        step(kt, True)
        return c

    lax.fori_loop(0, near0, far_body, 0)
    lax.fori_loop(near0, n_t, near_body, 0)
    outs = []
    for h in range(B_HEADS):
        acc = acc_scr[h]
        outs.append(acc[:, :d] / acc[:, d:])
    o = jnp.concatenate(outs, axis=1) * _silu(g_ref[...].astype(F32))
    o_ref[...] = o.astype(o_ref.dtype)


def _dsa_attention(g3, ik3, iw3, tdsa, q_blk, k_blk, v_blk, iq_blk):
    b, s, _ = g3.shape
    k_top = min(IDX_TOPK_MAX, s // 4)
    wq = B_HEADS * B_HEAD_DIM
    wk = B_KV_HEADS * B_HEAD_DIM
    wi = IDX_HEADS * IDX_DIM
    rows = (B_HEADS // B_KV_HEADS) * QB
    return pl.pallas_call(
        functools.partial(_dsa_kernel, s_len=s, k_top=k_top),
        out_shape=jax.ShapeDtypeStruct((b, s, wq), BF16),
        grid=(b, s // QB),
        in_specs=[
            pl.BlockSpec((None, QB, wq), lambda bb, i: (bb, i, q_blk)),
            pl.BlockSpec((None, s, wk), lambda bb, i: (bb, 0, k_blk)),
            pl.BlockSpec((None, s, wk), lambda bb, i: (bb, 0, v_blk)),
            pl.BlockSpec((None, QB, wi), lambda bb, i: (bb, i, iq_blk)),
            pl.BlockSpec((None, s, IDX_DIM), lambda bb, i: (bb, 0, 0)),
            pl.BlockSpec((None, QB, IDX_HEADS), lambda bb, i: (bb, i, 0)),
            pl.BlockSpec(tdsa.shape, lambda bb, i: (0, 0, 0, 0)),
            pl.BlockSpec((None, QB, wq), lambda bb, i: (bb, i, 0)),
        ],
        out_specs=pl.BlockSpec((None, QB, wq), lambda bb, i: (bb, i, 0)),
        scratch_shapes=[
            pltpu.VMEM((QB, s), F32),
            pltpu.VMEM((QB, s), F32),
            pltpu.VMEM((IDX_HEADS, QB, IDX_DIM), BF16),
            pltpu.VMEM((B_HEADS, QB, B_HEAD_DIM), BF16),
            pltpu.VMEM((B_KV_HEADS, s, B_HEAD_DIM), BF16),
            pltpu.VMEM((B_KV_HEADS, s, 2 * B_HEAD_DIM), BF16),
            pltpu.VMEM((B_HEADS, QB, LANE), F32),
            pltpu.VMEM((B_HEADS, QB, 2 * B_HEAD_DIM), F32),
        ],
        compiler_params=_cp("parallel", "arbitrary"),
        name="dsa_attention",
    )(g3, g3, g3, g3, ik3, iw3, tdsa, g3)


def _mla_layer(h, w_in, q_norm, w_uq, kv_norm, w_ukv, cos_t, sin_t, b, s):
    o_ckv = A_Q_LORA + A_KV_LORA
    o_mq = o_ckv + A_ROPE
    o_gate = o_mq + MEM_WIDTH
    w_f = w_in[:, :o_ckv].astype(BF16)
    w_kr = w_in[:, o_ckv:o_mq]
    w_kr2 = jnp.concatenate([w_kr, _swap_halves(w_kr)], axis=1).astype(BF16)
    w_g = jnp.concatenate([w_in[:, o_gate:], w_in[:, o_mq:o_gate]], axis=1).astype(BF16)
    cqkv = _matmul(h, w_f, F32)
    kr2 = _matmul(h, w_kr2, F32)
    g2 = _matmul(h, w_g, BF16)

    hd = A_NOPE + A_ROPE
    wq3 = w_uq.reshape(A_Q_LORA, A_HEADS, hd)
    w_r = wq3[:, :, A_NOPE:]
    w_uq2 = jnp.concatenate([
        wq3[:, :, :A_NOPE].reshape(A_Q_LORA, A_HEADS * A_NOPE),
        jnp.concatenate([w_r, _swap_halves(w_r)], axis=-1).reshape(A_Q_LORA, A_HEADS * 2 * A_ROPE),
    ], axis=1).astype(BF16)
    qp = _uq_proj(cqkv, q_norm, w_uq2, cos_t, sin_t, s)
    kv, krp = _ukv_proj(cqkv, kr2, kv_norm, w_ukv.astype(BF16), cos_t, sin_t, s)
    self_out = _mla_attention(qp.reshape(b, s, -1), kv.reshape(b, s, -1), krp.reshape(b, s, -1),
                              g2.reshape(b, s, -1), b, s)
    return self_out.reshape(b * s, -1), g2


def _dsa_layer(h, w_in, tdsa, b, s):
    wq = B_HEADS * B_HEAD_DIM
    wk = B_KV_HEADS * B_HEAD_DIM
    wi = IDX_HEADS * IDX_DIM
    o_k = wq
    o_v = o_k + wk
    o_iq = o_v + wk
    o_ik = o_iq + wi
    o_iw = o_ik + IDX_DIM
    o_mq = o_iw + IDX_HEADS
    o_gate = o_mq + MEM_WIDTH
    w_g = jnp.concatenate([w_in[:, o_gate:], w_in[:, o_mq:o_gate], w_in[:, :o_k], w_in[:, o_iq:o_ik],
                           w_in[:, o_k:o_v], w_in[:, o_v:o_iq]], axis=1).astype(BF16)
    w_i = w_in[:, o_ik:o_mq].astype(BF16)
    g2 = _matmul(h, w_g, BF16)
    ii = _matmul(h, w_i, F32)
    ik3 = ii[:, :IDX_DIM].astype(BF16).reshape(b, s, IDX_DIM)
    iw3 = ii[:, IDX_DIM:].reshape(b, s, IDX_HEADS)
    self_out = _dsa_attention(g2.reshape(b, s, -1), ik3, iw3, tdsa,
                              q_blk=2, k_blk=7168 // wk, v_blk=7424 // wk, iq_blk=6)
    return self_out.reshape(b * s, -1), g2


def _swa_layer(h, w_in, sinks, tswa, b, s):
    wq = C_HEADS * C_HEAD_DIM
    wk = C_KV_HEADS * C_HEAD_DIM
    o_k = wq
    o_v = o_k + wk
    o_mq = o_v + wk
    o_gate = o_mq + MEM_WIDTH
    w_g = jnp.concatenate([w_in[:, o_gate:], w_in[:, o_mq:o_gate], w_in[:, :o_k],
                           w_in[:, o_k:o_v], w_in[:, o_v:o_mq]], axis=1).astype(BF16)
    g2 = _matmul(h, w_g, BF16)
    self_out = _swa_attention(g2.reshape(b, s, -1), sinks, tswa, q_blk=2, k_blk=6144 // wk, v_blk=6400 // wk)
    return self_out.reshape(b * s, -1), g2


def kernel(x, mem, norm_in, final_norm, mem_norm, rel_bias, w_in_a, a_q_norm, w_uq, a_kv_norm, w_ukv,
           w_in_b, w_in_c, c_sinks, w_mem_kv, w_out):
    b, s, d = x.shape
    depth = norm_in.shape[0]
    ml = mem.shape[1]
    xf = x.reshape(b * s, d)
    mem_n = _rmsnorm(mem.reshape(b * ml, d), mem_norm, BF16)
    cos_t, sin_t = _rope_tables(s)
    tswa, tdsa = _bias_tables(rel_bias)
    h = _rmsnorm(xf, norm_in[0], BF16)
    for i in range(depth):
        kind, j = i % N_MIXERS, i // N_MIXERS
        if kind == 0:
            ys, g2 = _mla_layer(h, w_in_a[j], a_q_norm[j], w_uq[j], a_kv_norm[j], w_ukv[j], cos_t, sin_t, b, s)
        elif kind == 1:
            ys, g2 = _dsa_layer(h, w_in_b[j], tdsa, b, s)
        else:
            ys, g2 = _swa_layer(h, w_in_c[j], c_sinks[j], tswa, b, s)
        memkv = _matmul(mem_n, w_mem_kv[i].astype(BF16), BF16)
        ym = _mem_attention(g2.reshape(b, s, -1), 3, 2, memkv.reshape(b, ml, -1)).reshape(b * s, -1)
        last = i == depth - 1
        res = _out_proj(ys, ym, xf, w_out[i].astype(BF16), final_norm if last else norm_in[i + 1], not last)
        if last:
            return res[0].reshape(b, s, d)
        xf, h = res
```

```python
import functools
import math

import jax
import jax.numpy as jnp
from jax import lax
from jax.experimental import pallas as pl
from jax.experimental.pallas import tpu as pltpu

F32 = jnp.float32
BF16 = jnp.bfloat16

EPS = 1e-6
NEG_INF = -1e30
N_MIXERS = 3

N_BUCKETS = 32
MAX_DISTANCE = 128
N_BIAS_HEADS = 32

MEM_HEADS = 4
MEM_HEAD_DIM = 256
MEM_WIDTH = MEM_HEADS * MEM_HEAD_DIM

A_HEADS = 16
A_Q_LORA = 1536
A_KV_LORA = 512
A_NOPE = 128
A_ROPE = 64
A_V = 128
ROPE_THETA = 10000.0

B_HEADS = 32
B_KV_HEADS = 4
B_HEAD_DIM = 64
IDX_HEADS = 16
IDX_DIM = 64
IDX_TOPK_MAX = 256

C_HEADS = 32
C_KV_HEADS = 4
C_HEAD_DIM = 64
WINDOW = 128

LANE = 128
QB = 128
KT = 256
VMEM_LIMIT = 56 * 1024 * 1024
LOG2E = math.log2(math.e)
MM_MAX_TN = 1664


def _cp(*sem):
    return pltpu.CompilerParams(dimension_semantics=sem, vmem_limit_bytes=VMEM_LIMIT)


def _tile(n, pref):
    t = min(n, pref)
    assert n % t == 0, (n, pref)
    return t


def _dot(a, b):
    return jnp.dot(a, b, preferred_element_type=F32)


def _dot_t(a, b):
    return lax.dot_general(a, b, (((1,), (1,)), ((), ())), preferred_element_type=F32)


def _rms_kernel(x_ref, g_ref, o_ref):
    x = x_ref[...].astype(F32)
    y = x * lax.rsqrt(jnp.mean(x * x, axis=-1, keepdims=True) + EPS)
    o_ref[...] = (y * g_ref[...]).astype(o_ref.dtype)


def _rmsnorm(x2d, g, out_dtype):
    m, d = x2d.shape
    tm = _tile(m, 512)
    return pl.pallas_call(
        _rms_kernel,
        out_shape=jax.ShapeDtypeStruct((m, d), out_dtype),
        grid=(m // tm,),
        in_specs=[pl.BlockSpec((tm, d), lambda i: (i, 0)), pl.BlockSpec((1, d), lambda i: (0, 0))],
        out_specs=pl.BlockSpec((tm, d), lambda i: (i, 0)),
        compiler_params=_cp("parallel"),
        name="rmsnorm",
    )(x2d, g.reshape(1, d).astype(F32))


def _mm_kernel(a_ref, w_ref, o_ref):
    o_ref[...] = _dot(a_ref[...], w_ref[...]).astype(o_ref.dtype)


def _matmul(a, w, out_dtype):
    m, k = a.shape
    n = w.shape[1]
    tm = _tile(m, 1024)
    tn = max([t for t in range(LANE, min(n, MM_MAX_TN) + 1, LANE) if n % t == 0], default=n)
    return pl.pallas_call(
        _mm_kernel,
        out_shape=jax.ShapeDtypeStruct((m, n), out_dtype),
        grid=(m // tm, n // tn),
        in_specs=[pl.BlockSpec((tm, k), lambda i, j: (i, 0)), pl.BlockSpec((k, tn), lambda i, j: (0, j))],
        out_specs=pl.BlockSpec((tm, tn), lambda i, j: (i, j)),
        compiler_params=_cp("parallel", "arbitrary"),
        name="matmul",
    )(a, w)


def _rope_tables(s):
    d = A_ROPE
    inv = 1.0 / (ROPE_THETA ** (jnp.arange(0, d, 2, dtype=F32) / d))
    ang = jnp.arange(s, dtype=F32)[:, None] * inv[None, :]
    cos, sin = jnp.cos(ang), jnp.sin(ang)
    z = jnp.zeros((s, LANE - d), F32)
    return jnp.concatenate([cos, cos, z], axis=1), jnp.concatenate([-sin, sin, z], axis=1)


def _swap_halves(w):
    h = w.shape[-1] // 2
    return jnp.concatenate([w[..., h:], w[..., :h]], axis=-1)


def _uq_kernel(cq_ref, g_ref, w_ref, cos_ref, sin_ref, o_ref, *, n_nope):
    x = cq_ref[...]
    y = x * lax.rsqrt(jnp.mean(x * x, axis=-1, keepdims=True) + EPS)
    h = (y * g_ref[...]).astype(BF16)
    n = w_ref.shape[1]
    cw = 512
    for c in range(0, n_nope, cw):
        o_ref[:, c:c + cw] = _dot(h, w_ref[:, c:c + cw]).astype(BF16)
    cos = cos_ref[...]
    sin = sin_ref[...]
    for c in range(n_nope, n, cw):
        y = _dot(h, w_ref[:, c:c + cw])
        parts = []
        for k in range(cw // LANE):
            slab = y[:, k * LANE:(k + 1) * LANE]
            parts.append(slab * cos + pltpu.roll(slab, LANE // 2, 1) * sin)
        o_ref[:, c:c + cw] = jnp.concatenate(parts, axis=1).astype(BF16)


def _uq_proj(cqkv, q_norm, w_uq2, cos_t, sin_t, s):
    m = cqkv.shape[0]
    n = w_uq2.shape[1]
    tm = _tile(s, 512)
    nblk = s // tm
    return pl.pallas_call(
        functools.partial(_uq_kernel, n_nope=A_HEADS * A_NOPE),
        out_shape=jax.ShapeDtypeStruct((m, n), BF16),
        grid=(m // tm,),
        in_specs=[
            pl.BlockSpec((tm, A_Q_LORA), lambda i: (i, 0)),
            pl.BlockSpec((1, A_Q_LORA), lambda i: (0, 0)),
            pl.BlockSpec((A_Q_LORA, n), lambda i: (0, 0)),
            pl.BlockSpec((tm, LANE), lambda i: (i % nblk, 0)),
            pl.BlockSpec((tm, LANE), lambda i: (i % nblk, 0)),
        ],
        out_specs=pl.BlockSpec((tm, n), lambda i: (i, 0)),
        compiler_params=_cp("parallel"),
        name="mla_q_up",
    )(cqkv, q_norm.reshape(1, -1).astype(F32), w_uq2, cos_t, sin_t)


def _ukv_kernel(ckv_ref, kr_ref, g_ref, w_ref, cos_ref, sin_ref, kv_ref, krp_ref):
    x = ckv_ref[...]
    y = x * lax.rsqrt(jnp.mean(x * x, axis=-1, keepdims=True) + EPS)
    h = (y * g_ref[...]).astype(BF16)
    n = w_ref.shape[1]
    cw = 512
    for c in range(0, n, cw):
        kv_ref[:, c:c + cw] = _dot(h, w_ref[:, c:c + cw]).astype(BF16)
    slab = kr_ref[...]
    krp_ref[...] = (slab * cos_ref[...] + pltpu.roll(slab, LANE // 2, 1) * sin_ref[...]).astype(BF16)


def _ukv_proj(cqkv, kr2, kv_norm, w_ukv, cos_t, sin_t, s):
    m = cqkv.shape[0]
    n = w_ukv.shape[1]
    tm = _tile(s, 512)
    nblk = s // tm
    cblk = A_Q_LORA // A_KV_LORA
    return pl.pallas_call(
        _ukv_kernel,
        out_shape=(jax.ShapeDtypeStruct((m, n), BF16), jax.ShapeDtypeStruct((m, LANE), BF16)),
        grid=(m // tm,),
        in_specs=[
            pl.BlockSpec((tm, A_KV_LORA), lambda i: (i, cblk)),
            pl.BlockSpec((tm, LANE), lambda i: (i, 0)),
            pl.BlockSpec((1, A_KV_LORA), lambda i: (0, 0)),
            pl.BlockSpec((A_KV_LORA, n), lambda i: (0, 0)),
            pl.BlockSpec((tm, LANE), lambda i: (i % nblk, 0)),
            pl.BlockSpec((tm, LANE), lambda i: (i % nblk, 0)),
        ],
        out_specs=(pl.BlockSpec((tm, n), lambda i: (i, 0)), pl.BlockSpec((tm, LANE), lambda i: (i, 0))),
        compiler_params=_cp("parallel"),
        name="mla_kv_up",
    )(cqkv, kr2, kv_norm.reshape(1, -1).astype(F32), w_ukv, cos_t, sin_t)


def _rep2(x):
    return jnp.concatenate([x, x], axis=1)


def _silu(g):
    return g * (1.0 / (1.0 + jnp.exp(-g)))


def _mla_attn_kernel(qn_ref, qr_ref, kv_ref, kr_ref, g_ref, o_ref, kcat, vaug, m_scr, acc_scr, *, tq, hb):
    i = pl.program_id(2)

    @pl.when(i == 0)
    def _():
        ones = jnp.ones((kv_ref.shape[0], LANE), BF16)
        for hh in range(hb):
            kcat[hh, :, :LANE] = kv_ref[:, 2 * hh * LANE:(2 * hh + 1) * LANE]
            kcat[hh, :, LANE:] = kr_ref[...]
            vaug[hh, :, :LANE] = kv_ref[:, (2 * hh + 1) * LANE:(2 * hh + 2) * LANE]
            vaug[hh, :, LANE:] = ones

    m_scr[...] = jnp.full(m_scr.shape, -jnp.inf, F32)
    acc_scr[...] = jnp.zeros(acc_scr.shape, F32)
    qs = [jnp.concatenate([qn_ref[:, hh * LANE:(hh + 1) * LANE], qr_ref[:, hh * LANE:(hh + 1) * LANE]], axis=1)
          for hh in range(hb)]

    n_wide = lax.shift_right_logical(i, 1)

    def step(k0, tk, masked):
        for hh in range(hb):
            s = _dot_t(qs[hh], kcat[hh, pl.ds(k0, tk), :])
            if masked:
                row = lax.broadcasted_iota(jnp.int32, s.shape, 0)
                col = lax.broadcasted_iota(jnp.int32, s.shape, 1)
                s = jnp.where(row >= col, s, NEG_INF)
            m_prev = m_scr[hh]
            m_new = jnp.maximum(m_prev, jnp.max(s, axis=1, keepdims=True))
            alpha = jnp.exp2(m_prev - m_new)
            p = jnp.exp2(s - jnp.concatenate([m_new] * (tk // LANE), axis=1))
            acc_scr[hh] = _rep2(alpha) * acc_scr[hh] + _dot(p.astype(BF16), vaug[hh, pl.ds(k0, tk), :])
            m_scr[hh] = m_new

    def body(j, c):
        step(pl.multiple_of(j * 2 * tq, 2 * tq), 2 * tq, False)
        return c

    lax.fori_loop(0, n_wide, body, 0)

    @pl.when((i & 1) == 1)
    def _():
        step(pl.multiple_of((i - 1) * tq, tq), tq, False)

    step(pl.multiple_of(i * tq, tq), tq, True)
    for hh in range(hb):
        acc = acc_scr[hh]
        cols = slice(hh * LANE, (hh + 1) * LANE)
        gate = _silu(g_ref[:, cols].astype(F32))
        o_ref[:, cols] = (acc[:, :LANE] / acc[:, LANE:] * gate).astype(o_ref.dtype)


def _mla_attention(qp, kv, krp, g3, b, s):
    tq = _tile(s, 256)
    hn = A_HEADS
    hb = 8
    ng = hn // hb
    return pl.pallas_call(
        functools.partial(_mla_attn_kernel, tq=tq, hb=hb),
        out_shape=jax.ShapeDtypeStruct((b, s, hn * A_V), BF16),
        grid=(b, ng, s // tq),
        in_specs=[
            pl.BlockSpec((None, tq, hb * LANE), lambda bb, h, i: (bb, i, h)),
            pl.BlockSpec((None, tq, hb * LANE), lambda bb, h, i: (bb, i, ng + h)),
            pl.BlockSpec((None, s, 2 * hb * LANE), lambda bb, h, i: (bb, 0, h)),
            pl.BlockSpec((None, s, LANE), lambda bb, h, i: (bb, 0, 0)),
            pl.BlockSpec((None, tq, hb * LANE), lambda bb, h, i: (bb, i, h)),
        ],
        out_specs=pl.BlockSpec((None, tq, hb * LANE), lambda bb, h, i: (bb, i, h)),
        scratch_shapes=[
            pltpu.VMEM((hb, s, 2 * LANE), BF16),
            pltpu.VMEM((hb, s, 2 * LANE), BF16),
            pltpu.VMEM((hb, tq, LANE), F32),
            pltpu.VMEM((hb, tq, 2 * LANE), F32),
        ],
        compiler_params=_cp("parallel", "parallel", "arbitrary"),
        name="mla_attention",
    )(qp, qp, kv, krp, g3)


def _mem_attn_kernel(q_ref, kv_ref, g_ref, o_ref):
    d = MEM_HEAD_DIM
    for h in range(MEM_HEADS):
        cols = slice(h * d, (h + 1) * d)
        k = kv_ref[:, cols]
        v = kv_ref[:, MEM_WIDTH + h * d:MEM_WIDTH + (h + 1) * d]
        s = _dot_t(q_ref[:, cols], k) * (d ** -0.5)
        m = jnp.max(s, axis=1, keepdims=True)
        e = jnp.exp(s - m)
        p = e / jnp.sum(e, axis=1, keepdims=True)
        o = _dot(p.astype(BF16), v) * _silu(g_ref[:, cols].astype(F32))
        o_ref[:, cols] = o.astype(o_ref.dtype)


def _mem_attention(g3, mq_blk, gate_blk, memkv3):
    b, s, _ = g3.shape
    ml = memkv3.shape[1]
    tq = _tile(s, 512)
    return pl.pallas_call(
        _mem_attn_kernel,
        out_shape=jax.ShapeDtypeStruct((b, s, MEM_WIDTH), BF16),
        grid=(b, s // tq),
        in_specs=[
            pl.BlockSpec((None, tq, MEM_WIDTH), lambda bb, i: (bb, i, mq_blk)),
            pl.BlockSpec((None, ml, 2 * MEM_WIDTH), lambda bb, i: (bb, 0, 0)),
            pl.BlockSpec((None, tq, MEM_WIDTH), lambda bb, i: (bb, i, gate_blk)),
        ],
        out_specs=pl.BlockSpec((None, tq, MEM_WIDTH), lambda bb, i: (bb, i, 0)),
        compiler_params=_cp("parallel", "parallel"),
        name="mem_attention",
    )(g3, memkv3, g3)


def _out_kernel(ys_ref, ym_ref, x_ref, w_ref, g_ref, *refs, keep_x):
    if keep_x:
        xo_ref, h_ref, xrow = refs
    else:
        h_ref, xrow = refs
    j = pl.program_id(1)
    ws = ys_ref.shape[1]
    tn = x_ref.shape[1]
    xn = x_ref[...] + _dot(ys_ref[...], w_ref[:ws, :]) + _dot(ym_ref[...], w_ref[ws:, :])
    xrow[:, pl.ds(pl.multiple_of(j * tn, tn), tn)] = xn
    if keep_x:
        xo_ref[...] = xn

    @pl.when(j == pl.num_programs(1) - 1)
    def _():
        x = xrow[...]
        y = x * lax.rsqrt(jnp.mean(x * x, axis=-1, keepdims=True) + EPS)
        h_ref[...] = (y * g_ref[...]).astype(h_ref.dtype)


def _out_proj(ys, ym, x2d, w_out, g_next, keep_x):
    m, ws = ys.shape
    wm = ym.shape[1]
    kk, n = w_out.shape
    tm = _tile(m, 1024)
    tn = _tile(n, 512)
    h_dtype = BF16 if keep_x else F32
    row_spec = pl.BlockSpec((tm, n), lambda i, j: (i, 0))
    tile_spec = pl.BlockSpec((tm, tn), lambda i, j: (i, j))
    out_shape = [jax.ShapeDtypeStruct((m, n), h_dtype)]
    out_specs = [row_spec]
    if keep_x:
        out_shape.insert(0, jax.ShapeDtypeStruct((m, n), F32))
        out_specs.insert(0, tile_spec)
    return pl.pallas_call(
        functools.partial(_out_kernel, keep_x=keep_x),
        out_shape=tuple(out_shape),
        grid=(m // tm, n // tn),
        in_specs=[
            pl.BlockSpec((tm, ws), lambda i, j: (i, 0)),
            pl.BlockSpec((tm, wm), lambda i, j: (i, 0)),
            tile_spec,
            pl.BlockSpec((kk, tn), lambda i, j: (0, j)),
            pl.BlockSpec((1, n), lambda i, j: (0, 0)),
        ],
        out_specs=tuple(out_specs),
        scratch_shapes=[pltpu.VMEM((tm, n), F32)],
        compiler_params=_cp("parallel", "arbitrary"),
        name="out_proj_norm",
    )(ys, ym, x2d, w_out, g_next.reshape(1, n).astype(F32))


def _t5_bucket(rel):
    n = jnp.maximum(rel, 0)
    max_exact = N_BUCKETS // 2
    nf = jnp.maximum(n, 1).astype(F32)
    large = max_exact + (jnp.log(nf / max_exact) / math.log(MAX_DISTANCE / max_exact)
                         * (N_BUCKETS - max_exact)).astype(jnp.int32)
    large = jnp.minimum(large, N_BUCKETS - 1)
    return jnp.where(n < max_exact, n, large)


def _bias_kernel(rb_ref, bd_ref, bp_ref, tswa_ref, tdsa_ref):
    bd = bd_ref[...]
    bp = bp_ref[...]
    for h in range(N_BIAS_HEADS):
        g, j = divmod(h, N_BIAS_HEADS // B_KV_HEADS)

        def body(b, c, h=h):
            d, p = c
            v = rb_ref[b, h]
            return jnp.where(bd == b, v, d), jnp.where(bp == b, v, p)

        z = jnp.zeros((QB, QB), F32)
        d, p = lax.fori_loop(0, N_BUCKETS, body, (z, z))
        far = rb_ref[N_BUCKETS - 1, h]
        rows = pl.ds(j * QB, QB)
        t = lax.broadcasted_iota(jnp.int32, (QB, QB), 0)
        c = lax.broadcasted_iota(jnp.int32, (QB, QB), 1)
        tswa_ref[0, g, rows, :] = jnp.where(t >= c, d, NEG_INF)
        tswa_ref[1, g, rows, :] = jnp.where(t + QB - c < WINDOW, p, NEG_INF)
        tdsa_ref[0, g, rows, :] = z
        tdsa_ref[1, g, rows, :] = (d - far) * LOG2E
        tdsa_ref[2, g, rows, :] = (p - far) * LOG2E


def _bias_tables(rel_bias):
    t = jnp.arange(QB)[:, None]
    c = jnp.arange(QB)[None, :]
    bd = _t5_bucket(t - c).astype(jnp.int32)
    bp = _t5_bucket(t + QB - c).astype(jnp.int32)
    rows = (N_BIAS_HEADS // B_KV_HEADS) * QB
    return pl.pallas_call(
        _bias_kernel,
        out_shape=(jax.ShapeDtypeStruct((2, B_KV_HEADS, rows, QB), F32),
                   jax.ShapeDtypeStruct((3, B_KV_HEADS, rows, QB), F32)),
        in_specs=[pl.BlockSpec(memory_space=pltpu.SMEM),
                  pl.BlockSpec(memory_space=pltpu.VMEM),
                  pl.BlockSpec(memory_space=pltpu.VMEM)],
        out_specs=(pl.BlockSpec(memory_space=pltpu.VMEM), pl.BlockSpec(memory_space=pltpu.VMEM)),
        compiler_params=pltpu.CompilerParams(vmem_limit_bytes=VMEM_LIMIT),
        name="t5_bias_tables",
    )(rel_bias.astype(F32), bd, bp)


def _swa_kernel(sink_ref, q_ref, kp_ref, kc_ref, vp_ref, vc_ref, t_ref, g_ref, o_ref):
    i = pl.program_id(1)
    n_per = C_HEADS // C_KV_HEADS
    d = C_HEAD_DIM
    kb = jnp.concatenate([kp_ref[...], kc_ref[...]], axis=0)
    vb = jnp.concatenate([vp_ref[...], vc_ref[...]], axis=0)
    ones = jnp.ones((2 * QB, d), BF16)
    no_prev = jnp.where(i == 0, NEG_INF, 0.0).astype(F32)
    low = lax.broadcasted_iota(jnp.int32, (QB, LANE), 1) < d
    outs = []
    for g in range(C_KV_HEADS):
        kg = kb[:, g * d:(g + 1) * d]
        vg = vb[:, g * d:(g + 1) * d]
        vaug = (jnp.concatenate([vg, ones, ones, vg], axis=1), jnp.concatenate([ones, vg, vg, ones], axis=1))
        zk = jnp.zeros_like(kg)
        kpar = (jnp.concatenate([kg, zk], axis=1), jnp.concatenate([zk, kg], axis=1))
        for jp in range(n_per // 2):
            pair = []
            q2 = q_ref[:, (g * n_per + 2 * jp) * d:(g * n_per + 2 * jp + 2) * d]
            for par in range(2):
                j = 2 * jp + par
                h = g * n_per + j
                rows = pl.ds(j * QB, QB)
                s = _dot_t(q2, kpar[par]) * (d ** -0.5)
                s = s + jnp.concatenate([t_ref[1, g, rows, :] + no_prev, t_ref[0, g, rows, :]], axis=1)
                sink = sink_ref[0, h]
                m = jnp.maximum(jnp.broadcast_to(jnp.max(s, axis=1, keepdims=True), (QB, LANE)), sink)
                e = jnp.exp(s - _rep2(m))
                acc = _dot(e.astype(BF16), vaug[par])
                pair.append(acc[:, :LANE] / (acc[:, LANE:] + jnp.exp(sink - m)))
            outs.append(jnp.where(low, pair[0], pair[1]))
    o = jnp.concatenate(outs, axis=1) * _silu(g_ref[...].astype(F32))
    o_ref[...] = o.astype(o_ref.dtype)


def _swa_attention(g3, sinks, tswa, q_blk, k_blk, v_blk):
    b, s, _ = g3.shape
    wq = C_HEADS * C_HEAD_DIM
    wk = C_KV_HEADS * C_HEAD_DIM
    prev = lambda i: jnp.maximum(i - 1, 0)
    return pl.pallas_call(
        _swa_kernel,
        out_shape=jax.ShapeDtypeStruct((b, s, wq), BF16),
        grid=(b, s // QB),
        in_specs=[
            pl.BlockSpec(memory_space=pltpu.SMEM),
            pl.BlockSpec((None, QB, wq), lambda bb, i: (bb, i, q_blk)),
            pl.BlockSpec((None, QB, wk), lambda bb, i: (bb, prev(i), k_blk)),
            pl.BlockSpec((None, QB, wk), lambda bb, i: (bb, i, k_blk)),
            pl.BlockSpec((None, QB, wk), lambda bb, i: (bb, prev(i), v_blk)),
            pl.BlockSpec((None, QB, wk), lambda bb, i: (bb, i, v_blk)),
            pl.BlockSpec(tswa.shape, lambda bb, i: (0, 0, 0, 0)),
            pl.BlockSpec((None, QB, wq), lambda bb, i: (bb, i, 0)),
        ],
        out_specs=pl.BlockSpec((None, QB, wq), lambda bb, i: (bb, i, 0)),
        compiler_params=_cp("parallel", "parallel"),
        name="swa_attention",
    )(sinks.reshape(1, -1).astype(F32), g3, g3, g3, g3, g3, tswa, g3)


def _ordered_to_f32(key):
    bits = jnp.where(key >= 0, key, key ^ jnp.int32(0x7FFFFFFF))
    return lax.bitcast_convert_type(bits, F32)


def _dsa_kernel(q_ref, k_ref, v_ref, iq_ref, ik_ref, iw_ref, t_ref, g_ref, o_ref,
                sc, madd, iqs, qs, ks, vaug, m_scr, acc_scr, *, s_len, k_top):
    i = pl.program_id(1)
    n_per = B_HEADS // B_KV_HEADS
    d = B_HEAD_DIM

    @pl.when(i == 0)
    def _():
        ones = jnp.ones((s_len, d), BF16)
        for g in range(B_KV_HEADS):
            ks[g] = k_ref[:, g * d:(g + 1) * d]
            vaug[g, :, :d] = v_ref[:, g * d:(g + 1) * d]
            vaug[g, :, d:] = ones

    half = lax.shift_right_logical(i, 1)
    odd = i & 1
    n_t = half + 1
    sub = 8
    key_i = lax.broadcasted_iota(jnp.int32, (KT, QB), 0)
    qry_i = i * QB + lax.broadcasted_iota(jnp.int32, (KT, QB), 1)

    for hp in range(IDX_HEADS // 2):
        for par in range(2):
            h = 2 * hp + par
            iqs[hp, par * QB:(par + 1) * QB, :] = iq_ref[:, h * IDX_DIM:(h + 1) * IDX_DIM]
    w = iw_ref[...] * (IDX_DIM ** -0.5 * IDX_HEADS ** -0.5)
    wrows = [w[h:h + 1, :] for h in range(IDX_HEADS)]

    def score_tile(kt, c):
        k0 = pl.multiple_of(kt * KT, KT)
        ikt = ik_ref[pl.ds(k0, KT), :]
        acc = jnp.zeros((KT, QB), F32)
        for hp in range(IDX_HEADS // 2):
            dots = _dot_t(ikt, iqs[hp])
            acc = acc + jnp.maximum(dots[:, :QB], 0.0) * wrows[2 * hp]
            acc = acc + jnp.maximum(dots[:, QB:], 0.0) * wrows[2 * hp + 1]
        sc[pl.ds(k0, KT), :] = jnp.where(qry_i >= k0 + key_i, acc, NEG_INF)
        return c

    lax.fori_loop(0, n_t, score_tile, 0)

    tail = (s_len - n_t * KT).astype(F32)
    int_min = jnp.int32(-2 ** 31)

    def bisect(b, prefix):
        cand = prefix + lax.shift_left(jnp.int32(1), 31 - b)
        cand_f = _ordered_to_f32(cand)

        def count_tile(kt, c):
            k0 = pl.multiple_of(kt * KT, KT)
            t = sc[pl.ds(k0, KT), :].reshape(4, KT // (4 * sub), sub, QB)
            part = jnp.sum(jnp.where(t >= cand_f[None, None], 1.0, 0.0), axis=1)
            return c + ((part[0] + part[1]) + (part[2] + part[3]))

        cnt = lax.fori_loop(0, n_t, count_tile, jnp.zeros((sub, QB), F32))
        total = jnp.sum(cnt, axis=0, keepdims=True) + jnp.where(cand_f <= NEG_INF, tail, 0.0)
        return jnp.where(total >= k_top, cand, prefix)

    thr = _ordered_to_f32(lax.fori_loop(0, 32, bisect, jnp.full((sub, QB), int_min, jnp.int32)))[:1]

    def mask_tile(kt, c):
        k0 = pl.multiple_of(kt * KT, KT)
        sel = (sc[pl.ds(k0, KT), :] >= thr) & (qry_i >= k0 + key_i)
        madd[:, pl.ds(k0, KT)] = jnp.where(sel, 0.0, NEG_INF).T
        return c

    lax.fori_loop(0, n_t, mask_tile, 0)

    near0 = jnp.where(odd == 1, half, jnp.maximum(half - 1, 0))
    for h in range(B_HEADS):
        qs[h] = q_ref[:, h * d:(h + 1) * d]
    m_scr[...] = jnp.full(m_scr.shape, -jnp.inf, F32)
    acc_scr[...] = jnp.zeros(acc_scr.shape, F32)

    def step(kt, near):
        k0 = pl.multiple_of(kt * KT, KT)
        mt = madd[:, pl.ds(k0, KT)]
        if near:
            li = jnp.where(odd == 1, 2, jnp.where(kt == half, 1, 0))
            ri = jnp.where(odd == 1, 1, jnp.where(kt == half, 0, 2))
        for g in range(B_KV_HEADS):
            kg = ks[g, pl.ds(k0, KT), :]
            vg = vaug[g, pl.ds(k0, KT), :]
            for j in range(n_per):
                h = g * n_per + j
                s = _dot_t(qs[h], kg) + mt
                if near:
                    rows = pl.ds(j * QB, QB)
                    s = s + jnp.concatenate([t_ref[li, g, rows, :], t_ref[ri, g, rows, :]], axis=1)
                m_prev = m_scr[h]
                m_new = jnp.maximum(m_prev, jnp.max(s, axis=1, keepdims=True))
                alpha = jnp.exp2(m_prev - m_new)
                p = jnp.exp2(s - _rep2(m_new))
                acc_scr[h] = alpha * acc_scr[h] + _dot(p.astype(BF16), vg)
                m_scr[h] = m_new

    def far_body(kt, c):
        step(kt, False)
        return c

    def near_body(kt, c):
        step(kt, True)
        return c

    lax.fori_loop(0, near0, far_body, 0)
    lax.fori_loop(near0, n_t, near_body, 0)
    outs = []
    for h in range(B_HEADS):
        acc = acc_scr[h]
        outs.append(acc[:, :d] / acc[:, d:])
    o = jnp.concatenate(outs, axis=1) * _silu(g_ref[...].astype(F32))
    o_ref[...] = o.astype(o_ref.dtype)


def _dsa_attention(g3, ik3, iwt3, tdsa, q_blk, k_blk, v_blk, iq_blk):
    b, s, _ = g3.shape
    k_top = min(IDX_TOPK_MAX, s // 4)
    wq = B_HEADS * B_HEAD_DIM
    wk = B_KV_HEADS * B_HEAD_DIM
    wi = IDX_HEADS * IDX_DIM
    return pl.pallas_call(
        functools.partial(_dsa_kernel, s_len=s, k_top=k_top),
        out_shape=jax.ShapeDtypeStruct((b, s, wq), BF16),
        grid=(b, s // QB),
        in_specs=[
            pl.BlockSpec((None, QB, wq), lambda bb, i: (bb, i, q_blk)),
            pl.BlockSpec((None, s, wk), lambda bb, i: (bb, 0, k_blk)),
            pl.BlockSpec((None, s, wk), lambda bb, i: (bb, 0, v_blk)),
            pl.BlockSpec((None, QB, wi), lambda bb, i: (bb, i, iq_blk)),
            pl.BlockSpec((None, s, IDX_DIM), lambda bb, i: (bb, 0, 0)),
            pl.BlockSpec((None, IDX_HEADS, QB), lambda bb, i: (bb, 0, i)),
            pl.BlockSpec(tdsa.shape, lambda bb, i: (0, 0, 0, 0)),
            pl.BlockSpec((None, QB, wq), lambda bb, i: (bb, i, 0)),
        ],
        out_specs=pl.BlockSpec((None, QB, wq), lambda bb, i: (bb, i, 0)),
        scratch_shapes=[
            pltpu.VMEM((s, QB), F32),
            pltpu.VMEM((QB, s), F32),
            pltpu.VMEM((IDX_HEADS // 2, 2 * QB, IDX_DIM), BF16),
            pltpu.VMEM((B_HEADS, QB, B_HEAD_DIM), BF16),
            pltpu.VMEM((B_KV_HEADS, s, B_HEAD_DIM), BF16),
            pltpu.VMEM((B_KV_HEADS, s, 2 * B_HEAD_DIM), BF16),
            pltpu.VMEM((B_HEADS, QB, LANE), F32),
            pltpu.VMEM((B_HEADS, QB, 2 * B_HEAD_DIM), F32),
        ],
        compiler_params=_cp("parallel", "arbitrary"),
        name="dsa_attention",
    )(g3, g3, g3, g3, ik3, iwt3, tdsa, g3)


def _mla_layer(h, w_in, q_norm, w_uq, kv_norm, w_ukv, cos_t, sin_t, b, s):
    o_ckv = A_Q_LORA + A_KV_LORA
    o_mq = o_ckv + A_ROPE
    o_gate = o_mq + MEM_WIDTH
    w_f = w_in[:, :o_ckv].astype(BF16)
    w_kr = w_in[:, o_ckv:o_mq]
    w_kr2 = jnp.concatenate([w_kr, _swap_halves(w_kr)], axis=1).astype(BF16)
    w_g = jnp.concatenate([w_in[:, o_gate:], w_in[:, o_mq:o_gate]], axis=1).astype(BF16)
    cqkv = _matmul(h, w_f, F32)
    kr2 = _matmul(h, w_kr2, F32)
    g2 = _matmul(h, w_g, BF16)

    hd = A_NOPE + A_ROPE
    wq3 = w_uq.reshape(A_Q_LORA, A_HEADS, hd)
    w_r = wq3[:, :, A_NOPE:]
    w_uq2 = (jnp.concatenate([
        wq3[:, :, :A_NOPE].reshape(A_Q_LORA, A_HEADS * A_NOPE),
        jnp.concatenate([w_r, _swap_halves(w_r)], axis=-1).reshape(A_Q_LORA, A_HEADS * 2 * A_ROPE),
    ], axis=1) * (hd ** -0.5 * LOG2E)).astype(BF16)
    qp = _uq_proj(cqkv, q_norm, w_uq2, cos_t, sin_t, s)
    kv, krp = _ukv_proj(cqkv, kr2, kv_norm, w_ukv.astype(BF16), cos_t, sin_t, s)
    self_out = _mla_attention(qp.reshape(b, s, -1), kv.reshape(b, s, -1), krp.reshape(b, s, -1),
                              g2.reshape(b, s, -1), b, s)
    return self_out.reshape(b * s, -1), g2


def _dsa_layer(h, w_in, tdsa, b, s):
    wq = B_HEADS * B_HEAD_DIM
    wk = B_KV_HEADS * B_HEAD_DIM
    wi = IDX_HEADS * IDX_DIM
    o_k = wq
    o_v = o_k + wk
    o_iq = o_v + wk
    o_ik = o_iq + wi
    o_iw = o_ik + IDX_DIM
    o_mq = o_iw + IDX_HEADS
    o_gate = o_mq + MEM_WIDTH
    w_g = jnp.concatenate([w_in[:, o_gate:], w_in[:, o_mq:o_gate], w_in[:, :o_k] * (B_HEAD_DIM ** -0.5 * LOG2E),
                           w_in[:, o_iq:o_ik], w_in[:, o_k:o_v], w_in[:, o_v:o_iq]], axis=1).astype(BF16)
    w_i = w_in[:, o_ik:o_mq].astype(BF16)
    g2 = _matmul(h, w_g, BF16)
    ii = _matmul(h, w_i, F32)
    ik3 = ii[:, :IDX_DIM].astype(BF16).reshape(b, s, IDX_DIM)
    iwt3 = ii[:, IDX_DIM:].reshape(b, s, IDX_HEADS).transpose(0, 2, 1)
    self_out = _dsa_attention(g2.reshape(b, s, -1), ik3, iwt3, tdsa,
                              q_blk=2, k_blk=7168 // wk, v_blk=7424 // wk, iq_blk=6)
    return self_out.reshape(b * s, -1), g2


def _swa_layer(h, w_in, sinks, tswa, b, s):
    wq = C_HEADS * C_HEAD_DIM
    wk = C_KV_HEADS * C_HEAD_DIM
    o_k = wq
    o_v = o_k + wk
    o_mq = o_v + wk
    o_gate = o_mq + MEM_WIDTH
    w_g = jnp.concatenate([w_in[:, o_gate:], w_in[:, o_mq:o_gate], w_in[:, :o_k],
                           w_in[:, o_k:o_v], w_in[:, o_v:o_mq]], axis=1).astype(BF16)
    g2 = _matmul(h, w_g, BF16)
    self_out = _swa_attention(g2.reshape(b, s, -1), sinks, tswa, q_blk=2, k_blk=6144 // wk, v_blk=6400 // wk)
    return self_out.reshape(b * s, -1), g2


def kernel(x, mem, norm_in, final_norm, mem_norm, rel_bias, w_in_a, a_q_norm, w_uq, a_kv_norm, w_ukv,
           w_in_b, w_in_c, c_sinks, w_mem_kv, w_out):
    b, s, d = x.shape
    depth = norm_in.shape[0]
    ml = mem.shape[1]
    xf = x.reshape(b * s, d)
    mem_n = _rmsnorm(mem.reshape(b * ml, d), mem_norm, BF16)
    cos_t, sin_t = _rope_tables(s)
    tswa, tdsa = _bias_tables(rel_bias)
    h = _rmsnorm(xf, norm_in[0], BF16)
    for i in range(depth):
        kind, j = i % N_MIXERS, i // N_MIXERS
        if kind == 0:
            ys, g2 = _mla_layer(h, w_in_a[j], a_q_norm[j], w_uq[j], a_kv_norm[j], w_ukv[j], cos_t, sin_t, b, s)
        elif kind == 1:
            ys, g2 = _dsa_layer(h, w_in_b[j], tdsa, b, s)
        else:
            ys, g2 = _swa_layer(h, w_in_c[j], c_sinks[j], tswa, b, s)
        memkv = _matmul(mem_n, w_mem_kv[i].astype(BF16), BF16)
        ym = _mem_attention(g2.reshape(b, s, -1), 3, 2, memkv.reshape(b, ml, -1)).reshape(b * s, -1)
        last = i == depth - 1
        res = _out_proj(ys, ym, xf, w_out[i].astype(BF16), final_norm if last else norm_in[i + 1], not last)
        if last:
            return res[0].reshape(b, s, d)
        xf, h = res
```

```python
import functools
import math

import jax
import jax.numpy as jnp
from jax import lax
from jax.experimental import pallas as pl
from jax.experimental.pallas import tpu as pltpu

F32 = jnp.float32
BF16 = jnp.bfloat16

EPS = 1e-6
NEG_INF = -1e30
N_MIXERS = 3

N_BUCKETS = 32
MAX_DISTANCE = 128
N_BIAS_HEADS = 32

MEM_HEADS = 4
MEM_HEAD_DIM = 256
MEM_WIDTH = MEM_HEADS * MEM_HEAD_DIM

A_HEADS = 16
A_Q_LORA = 1536
A_KV_LORA = 512
A_NOPE = 128
A_ROPE = 64
A_V = 128
ROPE_THETA = 10000.0

B_HEADS = 32
B_KV_HEADS = 4
B_HEAD_DIM = 64
IDX_HEADS = 16
IDX_DIM = 64
IDX_TOPK_MAX = 256

C_HEADS = 32
C_KV_HEADS = 4
C_HEAD_DIM = 64
WINDOW = 128

LANE = 128
QB = 128
KT = 256
VMEM_LIMIT = 56 * 1024 * 1024
LOG2E = math.log2(math.e)
MM_MAX_TN = 1664


def _cp(*sem):
    return pltpu.CompilerParams(dimension_semantics=sem, vmem_limit_bytes=VMEM_LIMIT)


def _tile(n, pref):
    t = min(n, pref)
    assert n % t == 0, (n, pref)
    return t


def _dot(a, b):
    return jnp.dot(a, b, preferred_element_type=F32)


def _dot_t(a, b):
    return lax.dot_general(a, b, (((1,), (1,)), ((), ())), preferred_element_type=F32)


def _rms_kernel(x_ref, g_ref, o_ref):
    x = x_ref[...].astype(F32)
    y = x * lax.rsqrt(jnp.mean(x * x, axis=-1, keepdims=True) + EPS)
    o_ref[...] = (y * g_ref[...]).astype(o_ref.dtype)


def _rmsnorm(x2d, g, out_dtype):
    m, d = x2d.shape
    tm = _tile(m, 512)
    return pl.pallas_call(
        _rms_kernel,
        out_shape=jax.ShapeDtypeStruct((m, d), out_dtype),
        grid=(m // tm,),
        in_specs=[pl.BlockSpec((tm, d), lambda i: (i, 0)), pl.BlockSpec((1, d), lambda i: (0, 0))],
        out_specs=pl.BlockSpec((tm, d), lambda i: (i, 0)),
        compiler_params=_cp("parallel"),
        name="rmsnorm",
    )(x2d, g.reshape(1, d).astype(F32))


def _mm_kernel(a_ref, w_ref, o_ref):
    o_ref[...] = _dot(a_ref[...], w_ref[...]).astype(o_ref.dtype)


def _matmul(a, w, out_dtype):
    m, k = a.shape
    n = w.shape[1]
    tm = _tile(m, 1024)
    tn = max([t for t in range(LANE, min(n, MM_MAX_TN) + 1, LANE) if n % t == 0], default=n)
    return pl.pallas_call(
        _mm_kernel,
        out_shape=jax.ShapeDtypeStruct((m, n), out_dtype),
        grid=(m // tm, n // tn),
        in_specs=[pl.BlockSpec((tm, k), lambda i, j: (i, 0)), pl.BlockSpec((k, tn), lambda i, j: (0, j))],
        out_specs=pl.BlockSpec((tm, tn), lambda i, j: (i, j)),
        compiler_params=_cp("parallel", "arbitrary"),
        name="matmul",
    )(a, w)


def _repack_kernel(w_ref, *o_refs, groups):
    for o_ref, pieces in zip(o_refs, groups):
        for src, width, dst, scale in pieces:
            v = w_ref[:, src:src + width]
            if scale != 1.0:
                v = v * scale
            o_ref[:, dst:dst + width] = v.astype(o_ref.dtype)


def _repack(w, groups):
    k, n = w.shape
    tk = _tile(k, 256)
    widths = [sum(p[1] for p in pieces) for pieces in groups]
    return pl.pallas_call(
        functools.partial(_repack_kernel, groups=groups),
        out_shape=tuple(jax.ShapeDtypeStruct((k, wd), BF16) for wd in widths),
        grid=(k // tk,),
        in_specs=[pl.BlockSpec((tk, n), lambda i: (i, 0))],
        out_specs=tuple(pl.BlockSpec((tk, wd), lambda i: (i, 0)) for wd in widths),
        compiler_params=_cp("parallel"),
        name="weight_repack",
    )(w)


def _pieces(*src_width_scale):
    out, dst = [], 0
    for p in src_width_scale:
        src, width = p[0], p[1]
        out.append((src, width, dst, p[2] if len(p) > 2 else 1.0))
        dst += width
    return tuple(out)


def _rope_tables(s):
    d = A_ROPE
    inv = 1.0 / (ROPE_THETA ** (jnp.arange(0, d, 2, dtype=F32) / d))
    ang = jnp.arange(s, dtype=F32)[:, None] * inv[None, :]
    cos, sin = jnp.cos(ang), jnp.sin(ang)
    z = jnp.zeros((s, LANE - d), F32)
    one = (jnp.concatenate([cos, cos, z], axis=1), jnp.concatenate([-sin, sin, z], axis=1))
    two = (jnp.concatenate([cos] * 4, axis=1), jnp.concatenate([-sin, sin] * 2, axis=1))
    return one, two


def _uq_kernel(cq_ref, g_ref, w_ref, cos_ref, sin_ref, o_ref, *, n_nope):
    x = cq_ref[...]
    y = x * lax.rsqrt(jnp.mean(x * x, axis=-1, keepdims=True) + EPS)
    h = (y * g_ref[...]).astype(BF16)
    n = w_ref.shape[1]
    cw = 512
    for c in range(0, n_nope, cw):
        o_ref[:, c:c + cw] = _dot(h, w_ref[:, c:c + cw]).astype(BF16)
    cos = cos_ref[...]
    sin = sin_ref[...]
    lane = lax.broadcasted_iota(jnp.int32, (x.shape[0], LANE), 1)
    first_half = (lane & (A_ROPE - 1)) < A_ROPE // 2
    low = lane < A_ROPE
    for c in range(n_nope, n, cw):
        y = _dot(h, w_ref[:, c:c + cw])
        parts = []
        for k in range(cw // LANE):
            slab = y[:, k * LANE:(k + 1) * LANE]
            partner = jnp.where(first_half, pltpu.roll(slab, LANE - A_ROPE // 2, 1), pltpu.roll(slab, A_ROPE // 2, 1))
            r = slab * cos + partner * sin
            parts.append(jnp.where(low, r, 0.0))
            parts.append(jnp.where(low, pltpu.roll(r, A_ROPE, 1), 0.0))
        d0 = n_nope + 2 * (c - n_nope)
        o_ref[:, d0:d0 + 2 * cw] = jnp.concatenate(parts, axis=1).astype(BF16)


def _uq_proj(cqkv, q_norm, w_uq2, cos_t, sin_t, s):
    m = cqkv.shape[0]
    n = w_uq2.shape[1]
    n_nope = A_HEADS * A_NOPE
    n_out = n_nope + 2 * (n - n_nope)
    tm = _tile(s, 512)
    nblk = s // tm
    return pl.pallas_call(
        functools.partial(_uq_kernel, n_nope=n_nope),
        out_shape=jax.ShapeDtypeStruct((m, n_out), BF16),
        grid=(m // tm,),
        in_specs=[
            pl.BlockSpec((tm, A_Q_LORA), lambda i: (i, 0)),
            pl.BlockSpec((1, A_Q_LORA), lambda i: (0, 0)),
            pl.BlockSpec((A_Q_LORA, n), lambda i: (0, 0)),
            pl.BlockSpec((tm, LANE), lambda i: (i % nblk, 0)),
            pl.BlockSpec((tm, LANE), lambda i: (i % nblk, 0)),
        ],
        out_specs=pl.BlockSpec((tm, n_out), lambda i: (i, 0)),
        compiler_params=_cp("parallel"),
        name="mla_q_up",
    )(cqkv, q_norm.reshape(1, -1).astype(F32), w_uq2, cos_t, sin_t)


def _ukv_kernel(ckv_ref, kr_ref, g_ref, w_ref, cos_ref, sin_ref, kv_ref, krp_ref):
    x = ckv_ref[...]
    y = x * lax.rsqrt(jnp.mean(x * x, axis=-1, keepdims=True) + EPS)
    h = (y * g_ref[...]).astype(BF16)
    n = w_ref.shape[1]
    cw = 512
    for c in range(0, n, cw):
        kv_ref[:, c:c + cw] = _dot(h, w_ref[:, c:c + cw]).astype(BF16)
    slab = kr_ref[...]
    krp_ref[...] = (slab * cos_ref[...] + pltpu.roll(slab, LANE // 2, 1) * sin_ref[...]).astype(BF16)


def _ukv_proj(cqkv, kr2, kv_norm, w_ukv, cos_t, sin_t, s):
    m = cqkv.shape[0]
    n = w_ukv.shape[1]
    tm = _tile(s, 512)
    nblk = s // tm
    cblk = A_Q_LORA // A_KV_LORA
    return pl.pallas_call(
        _ukv_kernel,
        out_shape=(jax.ShapeDtypeStruct((m, n), BF16), jax.ShapeDtypeStruct((m, LANE), BF16)),
        grid=(m // tm,),
        in_specs=[
            pl.BlockSpec((tm, A_KV_LORA), lambda i: (i, cblk)),
            pl.BlockSpec((tm, LANE), lambda i: (i, 0)),
            pl.BlockSpec((1, A_KV_LORA), lambda i: (0, 0)),
            pl.BlockSpec((A_KV_LORA, n), lambda i: (0, 0)),
            pl.BlockSpec((tm, LANE), lambda i: (i % nblk, 0)),
            pl.BlockSpec((tm, LANE), lambda i: (i % nblk, 0)),
        ],
        out_specs=(pl.BlockSpec((tm, n), lambda i: (i, 0)), pl.BlockSpec((tm, LANE), lambda i: (i, 0))),
        compiler_params=_cp("parallel"),
        name="mla_kv_up",
    )(cqkv, kr2, kv_norm.reshape(1, -1).astype(F32), w_ukv, cos_t, sin_t)


def _rep2(x):
    return jnp.concatenate([x, x], axis=1)


def _silu(g):
    return g * (1.0 / (1.0 + jnp.exp(-g)))


def _mla_attn_kernel(qn_ref, qr_ref, kv_ref, kr_ref, g_ref, o_ref, kcat, vaug, m_scr, acc_scr, *, tq, hb):
    i = pl.program_id(2)

    @pl.when(i == 0)
    def _():
        ones = jnp.ones((kv_ref.shape[0], LANE), BF16)
        for hh in range(hb):
            kcat[hh, :, :LANE] = kv_ref[:, 2 * hh * LANE:(2 * hh + 1) * LANE]
            kcat[hh, :, LANE:] = kr_ref[...]
            vaug[hh, :, :LANE] = kv_ref[:, (2 * hh + 1) * LANE:(2 * hh + 2) * LANE]
            vaug[hh, :, LANE:] = ones

    m_scr[...] = jnp.full(m_scr.shape, -jnp.inf, F32)
    acc_scr[...] = jnp.zeros(acc_scr.shape, F32)
    qs = [jnp.concatenate([qn_ref[:, hh * LANE:(hh + 1) * LANE], qr_ref[:, hh * LANE:(hh + 1) * LANE]], axis=1)
          for hh in range(hb)]

    n_wide = lax.shift_right_logical(i, 1)

    def step(k0, tk, masked):
        for hh in range(hb):
            s = _dot_t(qs[hh], kcat[hh, pl.ds(k0, tk), :])
            if masked:
                row = lax.broadcasted_iota(jnp.int32, s.shape, 0)
                col = lax.broadcasted_iota(jnp.int32, s.shape, 1)
                s = jnp.where(row >= col, s, NEG_INF)
            m_prev = m_scr[hh]
            m_new = jnp.maximum(m_prev, jnp.max(s, axis=1, keepdims=True))
            alpha = jnp.exp2(m_prev - m_new)
            p = jnp.exp2(s - jnp.concatenate([m_new] * (tk // LANE), axis=1))
            acc_scr[hh] = _rep2(alpha) * acc_scr[hh] + _dot(p.astype(BF16), vaug[hh, pl.ds(k0, tk), :])
            m_scr[hh] = m_new

    def body(j, c):
        step(pl.multiple_of(j * 2 * tq, 2 * tq), 2 * tq, False)
        return c

    lax.fori_loop(0, n_wide, body, 0)

    @pl.when((i & 1) == 1)
    def _():
        step(pl.multiple_of((i - 1) * tq, tq), tq, False)

    step(pl.multiple_of(i * tq, tq), tq, True)
    for hh in range(hb):
        acc = acc_scr[hh]
        cols = slice(hh * LANE, (hh + 1) * LANE)
        gate = _silu(g_ref[:, cols].astype(F32))
        o_ref[:, cols] = (acc[:, :LANE] / acc[:, LANE:] * gate).astype(o_ref.dtype)


def _mla_attention(qp, kv, krp, g3, b, s):
    tq = _tile(s, 256)
    hn = A_HEADS
    hb = 8
    ng = hn // hb
    return pl.pallas_call(
        functools.partial(_mla_attn_kernel, tq=tq, hb=hb),
        out_shape=jax.ShapeDtypeStruct((b, s, hn * A_V), BF16),
        grid=(b, ng, s // tq),
        in_specs=[
            pl.BlockSpec((None, tq, hb * LANE), lambda bb, h, i: (bb, i, h)),
            pl.BlockSpec((None, tq, hb * LANE), lambda bb, h, i: (bb, i, ng + h)),
            pl.BlockSpec((None, s, 2 * hb * LANE), lambda bb, h, i: (bb, 0, h)),
            pl.BlockSpec((None, s, LANE), lambda bb, h, i: (bb, 0, 0)),
            pl.BlockSpec((None, tq, hb * LANE), lambda bb, h, i: (bb, i, h)),
        ],
        out_specs=pl.BlockSpec((None, tq, hb * LANE), lambda bb, h, i: (bb, i, h)),
        scratch_shapes=[
            pltpu.VMEM((hb, s, 2 * LANE), BF16),
            pltpu.VMEM((hb, s, 2 * LANE), BF16),
            pltpu.VMEM((hb, tq, LANE), F32),
            pltpu.VMEM((hb, tq, 2 * LANE), F32),
        ],
        compiler_params=_cp("parallel", "parallel", "arbitrary"),
        name="mla_attention",
    )(qp, qp, kv, krp, g3)


def _mem_attn_kernel(q_ref, kv_ref, g_ref, o_ref):
    d = MEM_HEAD_DIM
    for h in range(MEM_HEADS):
        cols = slice(h * d, (h + 1) * d)
        k = kv_ref[:, cols]
        v = kv_ref[:, MEM_WIDTH + h * d:MEM_WIDTH + (h + 1) * d]
        s = _dot_t(q_ref[:, cols], k) * (d ** -0.5)
        m = jnp.max(s, axis=1, keepdims=True)
        e = jnp.exp(s - m)
        p = e / jnp.sum(e, axis=1, keepdims=True)
        o = _dot(p.astype(BF16), v) * _silu(g_ref[:, cols].astype(F32))
        o_ref[:, cols] = o.astype(o_ref.dtype)


def _mem_attention(g3, mq_blk, gate_blk, memkv3):
    b, s, _ = g3.shape
    ml = memkv3.shape[1]
    tq = _tile(s, 512)
    return pl.pallas_call(
        _mem_attn_kernel,
        out_shape=jax.ShapeDtypeStruct((b, s, MEM_WIDTH), BF16),
        grid=(b, s // tq),
        in_specs=[
            pl.BlockSpec((None, tq, MEM_WIDTH), lambda bb, i: (bb, i, mq_blk)),
            pl.BlockSpec((None, ml, 2 * MEM_WIDTH), lambda bb, i: (bb, 0, 0)),
            pl.BlockSpec((None, tq, MEM_WIDTH), lambda bb, i: (bb, i, gate_blk)),
        ],
        out_specs=pl.BlockSpec((None, tq, MEM_WIDTH), lambda bb, i: (bb, i, 0)),
        compiler_params=_cp("parallel", "parallel"),
        name="mem_attention",
    )(g3, memkv3, g3)


def _out_kernel(ys_ref, ym_ref, x_ref, w_ref, g_ref, *refs, keep_x):
    if keep_x:
        xo_ref, h_ref, xrow = refs
    else:
        h_ref, xrow = refs
    j = pl.program_id(1)
    ws = ys_ref.shape[1]
    tn = x_ref.shape[1]
    xn = x_ref[...] + _dot(ys_ref[...], w_ref[:ws, :]) + _dot(ym_ref[...], w_ref[ws:, :])
    xrow[:, pl.ds(pl.multiple_of(j * tn, tn), tn)] = xn
    if keep_x:
        xo_ref[...] = xn

    @pl.when(j == pl.num_programs(1) - 1)
    def _():
        x = xrow[...]
        y = x * lax.rsqrt(jnp.mean(x * x, axis=-1, keepdims=True) + EPS)
        h_ref[...] = (y * g_ref[...]).astype(h_ref.dtype)


def _out_proj(ys, ym, x2d, w_out, g_next, keep_x):
    m, ws = ys.shape
    wm = ym.shape[1]
    kk, n = w_out.shape
    tm = _tile(m, 1024)
    tn = _tile(n, 512)
    h_dtype = BF16 if keep_x else F32
    row_spec = pl.BlockSpec((tm, n), lambda i, j: (i, 0))
    tile_spec = pl.BlockSpec((tm, tn), lambda i, j: (i, j))
    out_shape = [jax.ShapeDtypeStruct((m, n), h_dtype)]
    out_specs = [row_spec]
    if keep_x:
        out_shape.insert(0, jax.ShapeDtypeStruct((m, n), F32))
        out_specs.insert(0, tile_spec)
    return pl.pallas_call(
        functools.partial(_out_kernel, keep_x=keep_x),
        out_shape=tuple(out_shape),
        grid=(m // tm, n // tn),
        in_specs=[
            pl.BlockSpec((tm, ws), lambda i, j: (i, 0)),
            pl.BlockSpec((tm, wm), lambda i, j: (i, 0)),
            tile_spec,
            pl.BlockSpec((kk, tn), lambda i, j: (0, j)),
            pl.BlockSpec((1, n), lambda i, j: (0, 0)),
        ],
        out_specs=tuple(out_specs),
        scratch_shapes=[pltpu.VMEM((tm, n), F32)],
        compiler_params=_cp("parallel", "arbitrary"),
        name="out_proj_norm",
    )(ys, ym, x2d, w_out, g_next.reshape(1, n).astype(F32))


def _t5_bucket(rel):
    n = jnp.maximum(rel, 0)
    max_exact = N_BUCKETS // 2
    nf = jnp.maximum(n, 1).astype(F32)
    large = max_exact + (jnp.log(nf / max_exact) / math.log(MAX_DISTANCE / max_exact)
                         * (N_BUCKETS - max_exact)).astype(jnp.int32)
    large = jnp.minimum(large, N_BUCKETS - 1)
    return jnp.where(n < max_exact, n, large)


def _bias_kernel(rb_ref, bd_ref, bp_ref, tswa_ref, tdsa_ref):
    bd = bd_ref[...]
    bp = bp_ref[...]
    for h in range(N_BIAS_HEADS):
        g, j = divmod(h, N_BIAS_HEADS // B_KV_HEADS)

        def body(b, c, h=h):
            d, p = c
            v = rb_ref[b, h]
            return jnp.where(bd == b, v, d), jnp.where(bp == b, v, p)

        z = jnp.zeros((QB, QB), F32)
        d, p = lax.fori_loop(0, N_BUCKETS, body, (z, z))
        far = rb_ref[N_BUCKETS - 1, h]
        rows = pl.ds(j * QB, QB)
        t = lax.broadcasted_iota(jnp.int32, (QB, QB), 0)
        c = lax.broadcasted_iota(jnp.int32, (QB, QB), 1)
        tswa_ref[0, g, rows, :] = jnp.where(t >= c, d, NEG_INF)
        tswa_ref[1, g, rows, :] = jnp.where(t + QB - c < WINDOW, p, NEG_INF)
        tdsa_ref[0, g, rows, :] = z
        tdsa_ref[1, g, rows, :] = (d - far) * LOG2E
        tdsa_ref[2, g, rows, :] = (p - far) * LOG2E


def _bias_tables(rel_bias):
    t = jnp.arange(QB)[:, None]
    c = jnp.arange(QB)[None, :]
    bd = _t5_bucket(t - c).astype(jnp.int32)
    bp = _t5_bucket(t + QB - c).astype(jnp.int32)
    rows = (N_BIAS_HEADS // B_KV_HEADS) * QB
    return pl.pallas_call(
        _bias_kernel,
        out_shape=(jax.ShapeDtypeStruct((2, B_KV_HEADS, rows, QB), F32),
                   jax.ShapeDtypeStruct((3, B_KV_HEADS, rows, QB), F32)),
        in_specs=[pl.BlockSpec(memory_space=pltpu.SMEM),
                  pl.BlockSpec(memory_space=pltpu.VMEM),
                  pl.BlockSpec(memory_space=pltpu.VMEM)],
        out_specs=(pl.BlockSpec(memory_space=pltpu.VMEM), pl.BlockSpec(memory_space=pltpu.VMEM)),
        compiler_params=pltpu.CompilerParams(vmem_limit_bytes=VMEM_LIMIT),
        name="t5_bias_tables",
    )(rel_bias.astype(F32), bd, bp)


def _swa_kernel(sink_ref, q_ref, kp_ref, kc_ref, vp_ref, vc_ref, t_ref, g_ref, o_ref):
    i = pl.program_id(1)
    n_per = C_HEADS // C_KV_HEADS
    d = C_HEAD_DIM
    kb = jnp.concatenate([kp_ref[...], kc_ref[...]], axis=0)
    vb = jnp.concatenate([vp_ref[...], vc_ref[...]], axis=0)
    ones = jnp.ones((2 * QB, d), BF16)
    no_prev = jnp.where(i == 0, NEG_INF, 0.0).astype(F32)
    low = lax.broadcasted_iota(jnp.int32, (QB, LANE), 1) < d
    outs = []
    for g in range(C_KV_HEADS):
        kg = kb[:, g * d:(g + 1) * d]
        vg = vb[:, g * d:(g + 1) * d]
        vaug = (jnp.concatenate([vg, ones, ones, vg], axis=1), jnp.concatenate([ones, vg, vg, ones], axis=1))
        zk = jnp.zeros_like(kg)
        kpar = (jnp.concatenate([kg, zk], axis=1), jnp.concatenate([zk, kg], axis=1))
        for jp in range(n_per // 2):
            pair = []
            q2 = q_ref[:, (g * n_per + 2 * jp) * d:(g * n_per + 2 * jp + 2) * d]
            for par in range(2):
                j = 2 * jp + par
                h = g * n_per + j
                rows = pl.ds(j * QB, QB)
                s = _dot_t(q2, kpar[par]) * (d ** -0.5)
                s = s + jnp.concatenate([t_ref[1, g, rows, :] + no_prev, t_ref[0, g, rows, :]], axis=1)
                sink = sink_ref[0, h]
                m = jnp.maximum(jnp.broadcast_to(jnp.max(s, axis=1, keepdims=True), (QB, LANE)), sink)
                e = jnp.exp(s - _rep2(m))
                acc = _dot(e.astype(BF16), vaug[par])
                pair.append(acc[:, :LANE] / (acc[:, LANE:] + jnp.exp(sink - m)))
            outs.append(jnp.where(low, pair[0], pair[1]))
    o = jnp.concatenate(outs, axis=1) * _silu(g_ref[...].astype(F32))
    o_ref[...] = o.astype(o_ref.dtype)


def _swa_attention(g3, sinks, tswa, q_blk, k_blk, v_blk):
    b, s, _ = g3.shape
    wq = C_HEADS * C_HEAD_DIM
    wk = C_KV_HEADS * C_HEAD_DIM
    prev = lambda i: jnp.maximum(i - 1, 0)
    return pl.pallas_call(
        _swa_kernel,
        out_shape=jax.ShapeDtypeStruct((b, s, wq), BF16),
        grid=(b, s // QB),
        in_specs=[
            pl.BlockSpec(memory_space=pltpu.SMEM),
            pl.BlockSpec((None, QB, wq), lambda bb, i: (bb, i, q_blk)),
            pl.BlockSpec((None, QB, wk), lambda bb, i: (bb, prev(i), k_blk)),
            pl.BlockSpec((None, QB, wk), lambda bb, i: (bb, i, k_blk)),
            pl.BlockSpec((None, QB, wk), lambda bb, i: (bb, prev(i), v_blk)),
            pl.BlockSpec((None, QB, wk), lambda bb, i: (bb, i, v_blk)),
            pl.BlockSpec(tswa.shape, lambda bb, i: (0, 0, 0, 0)),
            pl.BlockSpec((None, QB, wq), lambda bb, i: (bb, i, 0)),
        ],
        out_specs=pl.BlockSpec((None, QB, wq), lambda bb, i: (bb, i, 0)),
        compiler_params=_cp("parallel", "parallel"),
        name="swa_attention",
    )(sinks.reshape(1, -1).astype(F32), g3, g3, g3, g3, g3, tswa, g3)


def _ordered_to_f32(key):
    bits = jnp.where(key >= 0, key, key ^ jnp.int32(0x7FFFFFFF))
    return lax.bitcast_convert_type(bits, F32)


def _dsa_kernel(q_ref, k_ref, v_ref, iq_ref, ik_ref, iw_ref, t_ref, g_ref, o_ref,
                sc, madd, iqs, ks, vaug, m_scr, acc_scr, *, s_len, k_top):
    i = pl.program_id(1)
    n_per = B_HEADS // B_KV_HEADS
    d = B_HEAD_DIM

    @pl.when(i == 0)
    def _():
        ones = jnp.ones((s_len, d), BF16)
        zeros = jnp.zeros((s_len, d), BF16)
        for g in range(B_KV_HEADS):
            kg = k_ref[:, g * d:(g + 1) * d]
            vg = v_ref[:, g * d:(g + 1) * d]
            ks[g, 0] = jnp.concatenate([kg, zeros], axis=1)
            ks[g, 1] = jnp.concatenate([zeros, kg], axis=1)
            vaug[g, 0] = jnp.concatenate([vg, ones], axis=1)
            vaug[g, 1] = jnp.concatenate([ones, vg], axis=1)

    half = lax.shift_right_logical(i, 1)
    odd = i & 1
    n_t = half + 1
    sub = 8
    key_i = lax.broadcasted_iota(jnp.int32, (KT, QB), 0)
    qry_i = i * QB + lax.broadcasted_iota(jnp.int32, (KT, QB), 1)

    for hp in range(IDX_HEADS // 2):
        for par in range(2):
            h = 2 * hp + par
            iqs[hp, par * QB:(par + 1) * QB, :] = iq_ref[:, h * IDX_DIM:(h + 1) * IDX_DIM]
    w = iw_ref[...] * (IDX_DIM ** -0.5 * IDX_HEADS ** -0.5)
    wrows = [w[h:h + 1, :] for h in range(IDX_HEADS)]

    def score_tile(kt, c):
        k0 = pl.multiple_of(kt * KT, KT)
        ikt = ik_ref[pl.ds(k0, KT), :]
        acc = jnp.zeros((KT, QB), F32)
        for hp in range(IDX_HEADS // 2):
            dots = _dot_t(ikt, iqs[hp])
            acc = acc + jnp.maximum(dots[:, :QB], 0.0) * wrows[2 * hp]
            acc = acc + jnp.maximum(dots[:, QB:], 0.0) * wrows[2 * hp + 1]
        sc[pl.ds(k0, KT), :] = jnp.where(qry_i >= k0 + key_i, acc, NEG_INF)
        return c

    lax.fori_loop(0, n_t, score_tile, 0)

    tail = (s_len - n_t * KT).astype(F32)
    int_min = jnp.int32(-2 ** 31)

    def bisect(b, prefix):
        cand = prefix + lax.shift_left(jnp.int32(1), 31 - b)
        cand_f = _ordered_to_f32(cand)

        def count_tile(kt, c):
            k0 = pl.multiple_of(kt * KT, KT)
            t = sc[pl.ds(k0, KT), :].reshape(4, KT // (4 * sub), sub, QB)
            part = jnp.sum(jnp.where(t >= cand_f[None, None], 1.0, 0.0), axis=1)
            return c + ((part[0] + part[1]) + (part[2] + part[3]))

        cnt = lax.fori_loop(0, n_t, count_tile, jnp.zeros((sub, QB), F32))
        total = jnp.sum(cnt, axis=0, keepdims=True) + jnp.where(cand_f <= NEG_INF, tail, 0.0)
        return jnp.where(total >= k_top, cand, prefix)

    thr = _ordered_to_f32(lax.fori_loop(0, 32, bisect, jnp.full((sub, QB), int_min, jnp.int32)))[:1]

    def mask_tile(kt, c):
        k0 = pl.multiple_of(kt * KT, KT)
        sel = (sc[pl.ds(k0, KT), :] >= thr) & (qry_i >= k0 + key_i)
        madd[:, pl.ds(k0, KT)] = jnp.where(sel, 0.0, NEG_INF).T
        return c

    lax.fori_loop(0, n_t, mask_tile, 0)

    near0 = jnp.where(odd == 1, half, jnp.maximum(half - 1, 0))
    m_scr[...] = jnp.full(m_scr.shape, -jnp.inf, F32)
    acc_scr[...] = jnp.zeros(acc_scr.shape, F32)

    def step(kt, near):
        k0 = pl.multiple_of(kt * KT, KT)
        mt = madd[:, pl.ds(k0, KT)]
        if near:
            li = jnp.where(odd == 1, 2, jnp.where(kt == half, 1, 0))
            ri = jnp.where(odd == 1, 1, jnp.where(kt == half, 0, 2))
        for g in range(B_KV_HEADS):
            kg = (ks[g, 0, pl.ds(k0, KT), :], ks[g, 1, pl.ds(k0, KT), :])
            vg = (vaug[g, 0, pl.ds(k0, KT), :], vaug[g, 1, pl.ds(k0, KT), :])
            for j in range(n_per):
                h = g * n_per + j
                q2 = q_ref[:, (h // 2) * LANE:(h // 2 + 1) * LANE]
                s = _dot_t(q2, kg[h % 2]) + mt
                if near:
                    rows = pl.ds(j * QB, QB)
                    s = s + jnp.concatenate([t_ref[li, g, rows, :], t_ref[ri, g, rows, :]], axis=1)
                m_prev = m_scr[h]
                m_new = jnp.maximum(m_prev, jnp.max(s, axis=1, keepdims=True))
                alpha = jnp.exp2(m_prev - m_new)
                p = jnp.exp2(s - _rep2(m_new))
                acc_scr[h] = alpha * acc_scr[h] + _dot(p.astype(BF16), vg[h % 2])
                m_scr[h] = m_new

    def far_body(kt, c):
        step(kt, False)
        return c

    def near_body(kt, c):
        step(kt, True)
        return c

    lax.fori_loop(0, near0, far_body, 0)
    lax.fori_loop(near0, n_t, near_body, 0)
    low = lax.broadcasted_iota(jnp.int32, (QB, LANE), 1) < d
    outs = []
    for hp in range(B_HEADS // 2):
        acc_e = acc_scr[2 * hp]
        acc_o = acc_scr[2 * hp + 1]
        num = jnp.where(low, acc_e, acc_o)
        den = pltpu.roll(jnp.where(low, acc_o, acc_e), d, 1)
        outs.append(num / den)
    o = jnp.concatenate(outs, axis=1) * _silu(g_ref[...].astype(F32))
    o_ref[...] = o.astype(o_ref.dtype)


def _dsa_attention(g3, ik3, iwt3, tdsa, q_blk, k_blk, v_blk, iq_blk):
    b, s, _ = g3.shape
    k_top = min(IDX_TOPK_MAX, s // 4)
    wq = B_HEADS * B_HEAD_DIM
    wk = B_KV_HEADS * B_HEAD_DIM
    wi = IDX_HEADS * IDX_DIM
    return pl.pallas_call(
        functools.partial(_dsa_kernel, s_len=s, k_top=k_top),
        out_shape=jax.ShapeDtypeStruct((b, s, wq), BF16),
        grid=(b, s // QB),
        in_specs=[
            pl.BlockSpec((None, QB, wq), lambda bb, i: (bb, i, q_blk)),
            pl.BlockSpec((None, s, wk), lambda bb, i: (bb, 0, k_blk)),
            pl.BlockSpec((None, s, wk), lambda bb, i: (bb, 0, v_blk)),
            pl.BlockSpec((None, QB, wi), lambda bb, i: (bb, i, iq_blk)),
            pl.BlockSpec((None, s, IDX_DIM), lambda bb, i: (bb, 0, 0)),
            pl.BlockSpec((None, IDX_HEADS, QB), lambda bb, i: (bb, 0, i)),
            pl.BlockSpec(tdsa.shape, lambda bb, i: (0, 0, 0, 0)),
            pl.BlockSpec((None, QB, wq), lambda bb, i: (bb, i, 0)),
        ],
        out_specs=pl.BlockSpec((None, QB, wq), lambda bb, i: (bb, i, 0)),
        scratch_shapes=[
            pltpu.VMEM((s, QB), F32),
            pltpu.VMEM((QB, s), F32),
            pltpu.VMEM((IDX_HEADS // 2, 2 * QB, IDX_DIM), BF16),
            pltpu.VMEM((B_KV_HEADS, 2, s, 2 * B_HEAD_DIM), BF16),
            pltpu.VMEM((B_KV_HEADS, 2, s, 2 * B_HEAD_DIM), BF16),
            pltpu.VMEM((B_HEADS, QB, LANE), F32),
            pltpu.VMEM((B_HEADS, QB, 2 * B_HEAD_DIM), F32),
        ],
        compiler_params=_cp("parallel", "arbitrary"),
        name="dsa_attention",
    )(g3, g3, g3, g3, ik3, iwt3, tdsa, g3)


def _mla_layer(h, w_in, q_norm, w_uq, kv_norm, w_ukv, rope1, rope2, b, s):
    o_ckv = A_Q_LORA + A_KV_LORA
    o_mq = o_ckv + A_ROPE
    o_gate = o_mq + MEM_WIDTH
    half = A_ROPE // 2
    w_f, w_kr2, w_g = _repack(w_in, (
        _pieces((0, o_ckv)),
        _pieces((o_ckv, A_ROPE), (o_ckv + half, half), (o_ckv, half)),
        _pieces((o_gate, A_HEADS * A_V + MEM_WIDTH), (o_mq, MEM_WIDTH)),
    ))
    cqkv = _matmul(h, w_f, F32)
    kr2 = _matmul(h, w_kr2, F32)
    g2 = _matmul(h, w_g, BF16)

    hd = A_NOPE + A_ROPE
    (w_uq2,) = _repack(w_uq, (
        _pieces(*[(hh * hd, A_NOPE, hd ** -0.5 * LOG2E) for hh in range(A_HEADS)],
                *[(hh * hd + A_NOPE, A_ROPE, hd ** -0.5 * LOG2E) for hh in range(A_HEADS)]),
    ))
    qp = _uq_proj(cqkv, q_norm, w_uq2, rope2[0], rope2[1], s)
    kv, krp = _ukv_proj(cqkv, kr2, kv_norm, w_ukv.astype(BF16), rope1[0], rope1[1], s)
    self_out = _mla_attention(qp.reshape(b, s, -1), kv.reshape(b, s, -1), krp.reshape(b, s, -1),
                              g2.reshape(b, s, -1), b, s)
    return self_out.reshape(b * s, -1), g2


def _dsa_layer(h, w_in, tdsa, b, s):
    wq = B_HEADS * B_HEAD_DIM
    wk = B_KV_HEADS * B_HEAD_DIM
    wi = IDX_HEADS * IDX_DIM
    o_k = wq
    o_v = o_k + wk
    o_iq = o_v + wk
    o_ik = o_iq + wi
    o_iw = o_ik + IDX_DIM
    o_mq = o_iw + IDX_HEADS
    o_gate = o_mq + MEM_WIDTH
    w_g, w_i = _repack(w_in, (
        _pieces((o_gate, wq + MEM_WIDTH), (o_mq, MEM_WIDTH), (0, wq, B_HEAD_DIM ** -0.5 * LOG2E),
                (o_iq, wi), (o_k, wk), (o_v, wk)),
        _pieces((o_ik, IDX_DIM + IDX_HEADS)),
    ))
    g2 = _matmul(h, w_g, BF16)
    ii = _matmul(h, w_i, F32)
    ik3 = ii[:, :IDX_DIM].astype(BF16).reshape(b, s, IDX_DIM)
    iwt3 = ii[:, IDX_DIM:].reshape(b, s, IDX_HEADS).transpose(0, 2, 1)
    self_out = _dsa_attention(g2.reshape(b, s, -1), ik3, iwt3, tdsa,
                              q_blk=2, k_blk=7168 // wk, v_blk=7424 // wk, iq_blk=6)
    return self_out.reshape(b * s, -1), g2


def _swa_layer(h, w_in, sinks, tswa, b, s):
    wq = C_HEADS * C_HEAD_DIM
    wk = C_KV_HEADS * C_HEAD_DIM
    o_k = wq
    o_v = o_k + wk
    o_mq = o_v + wk
    o_gate = o_mq + MEM_WIDTH
    (w_g,) = _repack(w_in, (
        _pieces((o_gate, wq + MEM_WIDTH), (o_mq, MEM_WIDTH), (0, wq), (o_k, wk), (o_v, wk)),
    ))
    g2 = _matmul(h, w_g, BF16)
    self_out = _swa_attention(g2.reshape(b, s, -1), sinks, tswa, q_blk=2, k_blk=6144 // wk, v_blk=6400 // wk)
    return self_out.reshape(b * s, -1), g2


def kernel(x, mem, norm_in, final_norm, mem_norm, rel_bias, w_in_a, a_q_norm, w_uq, a_kv_norm, w_ukv,
           w_in_b, w_in_c, c_sinks, w_mem_kv, w_out):
    b, s, d = x.shape
    depth = norm_in.shape[0]
    ml = mem.shape[1]
    xf = x.reshape(b * s, d)
    mem_n = _rmsnorm(mem.reshape(b * ml, d), mem_norm, BF16)
    rope1, rope2 = _rope_tables(s)
    tswa, tdsa = _bias_tables(rel_bias)
    h = _rmsnorm(xf, norm_in[0], BF16)
    for i in range(depth):
        kind, j = i % N_MIXERS, i // N_MIXERS
        if kind == 0:
            ys, g2 = _mla_layer(h, w_in_a[j], a_q_norm[j], w_uq[j], a_kv_norm[j], w_ukv[j], rope1, rope2, b, s)
        elif kind == 1:
            ys, g2 = _dsa_layer(h, w_in_b[j], tdsa, b, s)
        else:
            ys, g2 = _swa_layer(h, w_in_c[j], c_sinks[j], tswa, b, s)
        memkv = _matmul(mem_n, w_mem_kv[i].astype(BF16), BF16)
        ym = _mem_attention(g2.reshape(b, s, -1), 3, 2, memkv.reshape(b, ml, -1)).reshape(b * s, -1)
        last = i == depth - 1
        res = _out_proj(ys, ym, xf, w_out[i].astype(BF16), final_norm if last else norm_in[i + 1], not last)
        if last:
            return res[0].reshape(b, s, d)
        xf, h = res
```

```python
import functools
import math

import jax
import jax.numpy as jnp
from jax import lax
from jax.experimental import pallas as pl
from jax.experimental.pallas import tpu as pltpu

F32 = jnp.float32
BF16 = jnp.bfloat16

EPS = 1e-6
NEG_INF = -1e30
N_MIXERS = 3

N_BUCKETS = 32
MAX_DISTANCE = 128
N_BIAS_HEADS = 32

MEM_HEADS = 4
MEM_HEAD_DIM = 256
MEM_WIDTH = MEM_HEADS * MEM_HEAD_DIM

A_HEADS = 16
A_Q_LORA = 1536
A_KV_LORA = 512
A_NOPE = 128
A_ROPE = 64
A_V = 128
ROPE_THETA = 10000.0

B_HEADS = 32
B_KV_HEADS = 4
B_HEAD_DIM = 64
IDX_HEADS = 16
IDX_DIM = 64
IDX_TOPK_MAX = 256

C_HEADS = 32
C_KV_HEADS = 4
C_HEAD_DIM = 64
WINDOW = 128

LANE = 128
QB = 128
KT = 256
VMEM_LIMIT = 56 * 1024 * 1024
LOG2E = math.log2(math.e)
MM_MAX_TN = 1664


def _cp(*sem):
    return pltpu.CompilerParams(dimension_semantics=sem, vmem_limit_bytes=VMEM_LIMIT)


def _tile(n, pref):
    t = min(n, pref)
    assert n % t == 0, (n, pref)
    return t


def _dot(a, b):
    return jnp.dot(a, b, preferred_element_type=F32)


def _dot_t(a, b):
    return lax.dot_general(a, b, (((1,), (1,)), ((), ())), preferred_element_type=F32)


def _rms_kernel(x_ref, g_ref, o_ref):
    x = x_ref[...].astype(F32)
    y = x * lax.rsqrt(jnp.mean(x * x, axis=-1, keepdims=True) + EPS)
    o_ref[...] = (y * g_ref[...]).astype(o_ref.dtype)


def _rmsnorm(x2d, g, out_dtype):
    m, d = x2d.shape
    tm = _tile(m, 512)
    return pl.pallas_call(
        _rms_kernel,
        out_shape=jax.ShapeDtypeStruct((m, d), out_dtype),
        grid=(m // tm,),
        in_specs=[pl.BlockSpec((tm, d), lambda i: (i, 0)), pl.BlockSpec((1, d), lambda i: (0, 0))],
        out_specs=pl.BlockSpec((tm, d), lambda i: (i, 0)),
        compiler_params=_cp("parallel"),
        name="rmsnorm",
    )(x2d, g.reshape(1, d).astype(F32))


def _mm_kernel(a_ref, w_ref, o_ref):
    o_ref[...] = _dot(a_ref[...], w_ref[...]).astype(o_ref.dtype)


def _layer_spec(w, block, index_map):
    if isinstance(w, tuple):
        stack, layer = w
        return stack, pl.BlockSpec((None,) + block, lambda *g: (layer,) + index_map(*g))
    return w, pl.BlockSpec(block, index_map)


def _w_shape(w):
    return w[0].shape[1:] if isinstance(w, tuple) else w.shape


def _matmul(a, w, out_dtype):
    m, k = a.shape
    n = _w_shape(w)[1]
    tm = _tile(m, 1024)
    tn = max([t for t in range(LANE, min(n, MM_MAX_TN) + 1, LANE) if n % t == 0], default=n)
    w_arr, w_spec = _layer_spec(w, (k, tn), lambda i, j: (0, j))
    return pl.pallas_call(
        _mm_kernel,
        out_shape=jax.ShapeDtypeStruct((m, n), out_dtype),
        grid=(m // tm, n // tn),
        in_specs=[pl.BlockSpec((tm, k), lambda i, j: (i, 0)), w_spec],
        out_specs=pl.BlockSpec((tm, tn), lambda i, j: (i, j)),
        compiler_params=_cp("parallel", "arbitrary"),
        name="matmul",
    )(a, w_arr)


def _repack_kernel(w_ref, *o_refs, groups):
    for o_ref, pieces in zip(o_refs, groups):
        for src, width, dst, scale in pieces:
            v = w_ref[:, src:src + width]
            if scale != 1.0:
                v = v * scale
            o_ref[:, dst:dst + width] = v.astype(o_ref.dtype)


def _repack(w, groups):
    k, n = _w_shape(w)
    tk = _tile(k, 256)
    widths = [sum(p[1] for p in pieces) for pieces in groups]
    w_arr, w_spec = _layer_spec(w, (tk, n), lambda i: (i, 0))
    return pl.pallas_call(
        functools.partial(_repack_kernel, groups=groups),
        out_shape=tuple(jax.ShapeDtypeStruct((k, wd), BF16) for wd in widths),
        grid=(k // tk,),
        in_specs=[w_spec],
        out_specs=tuple(pl.BlockSpec((tk, wd), lambda i: (i, 0)) for wd in widths),
        compiler_params=_cp("parallel"),
        name="weight_repack",
    )(w_arr)


def _pieces(*src_width_scale):
    out, dst = [], 0
    for p in src_width_scale:
        src, width = p[0], p[1]
        out.append((src, width, dst, p[2] if len(p) > 2 else 1.0))
        dst += width
    return tuple(out)


def _rope_tables(s):
    d = A_ROPE
    inv = 1.0 / (ROPE_THETA ** (jnp.arange(0, d, 2, dtype=F32) / d))
    ang = jnp.arange(s, dtype=F32)[:, None] * inv[None, :]
    cos, sin = jnp.cos(ang), jnp.sin(ang)
    z = jnp.zeros((s, LANE - d), F32)
    one = (jnp.concatenate([cos, cos, z], axis=1), jnp.concatenate([-sin, sin, z], axis=1))
    two = (jnp.concatenate([cos] * 4, axis=1), jnp.concatenate([-sin, sin] * 2, axis=1))
    return one, two


def _uq_kernel(cq_ref, g_ref, w_ref, cos_ref, sin_ref, o_ref, *, n_nope):
    x = cq_ref[...]
    y = x * lax.rsqrt(jnp.mean(x * x, axis=-1, keepdims=True) + EPS)
    h = (y * g_ref[...]).astype(BF16)
    n = w_ref.shape[1]
    cw = 512
    for c in range(0, n_nope, cw):
        o_ref[:, c:c + cw] = _dot(h, w_ref[:, c:c + cw]).astype(BF16)
    cos = cos_ref[...]
    sin = sin_ref[...]
    lane = lax.broadcasted_iota(jnp.int32, (x.shape[0], LANE), 1)
    first_half = (lane & (A_ROPE - 1)) < A_ROPE // 2
    low = lane < A_ROPE
    for c in range(n_nope, n, cw):
        y = _dot(h, w_ref[:, c:c + cw])
        parts = []
        for k in range(cw // LANE):
            slab = y[:, k * LANE:(k + 1) * LANE]
            partner = jnp.where(first_half, pltpu.roll(slab, LANE - A_ROPE // 2, 1), pltpu.roll(slab, A_ROPE // 2, 1))
            r = slab * cos + partner * sin
            parts.append(jnp.where(low, r, 0.0))
            parts.append(jnp.where(low, pltpu.roll(r, A_ROPE, 1), 0.0))
        d0 = n_nope + 2 * (c - n_nope)
        o_ref[:, d0:d0 + 2 * cw] = jnp.concatenate(parts, axis=1).astype(BF16)


def _uq_proj(cqkv, q_norm, w_uq2, cos_t, sin_t, s):
    m = cqkv.shape[0]
    n = w_uq2.shape[1]
    n_nope = A_HEADS * A_NOPE
    n_out = n_nope + 2 * (n - n_nope)
    tm = _tile(s, 512)
    nblk = s // tm
    return pl.pallas_call(
        functools.partial(_uq_kernel, n_nope=n_nope),
        out_shape=jax.ShapeDtypeStruct((m, n_out), BF16),
        grid=(m // tm,),
        in_specs=[
            pl.BlockSpec((tm, A_Q_LORA), lambda i: (i, 0)),
            pl.BlockSpec((1, A_Q_LORA), lambda i: (0, 0)),
            pl.BlockSpec((A_Q_LORA, n), lambda i: (0, 0)),
            pl.BlockSpec((tm, LANE), lambda i: (i % nblk, 0)),
            pl.BlockSpec((tm, LANE), lambda i: (i % nblk, 0)),
        ],
        out_specs=pl.BlockSpec((tm, n_out), lambda i: (i, 0)),
        compiler_params=_cp("parallel"),
        name="mla_q_up",
    )(cqkv, q_norm.reshape(1, -1).astype(F32), w_uq2, cos_t, sin_t)


def _ukv_kernel(ckv_ref, kr_ref, g_ref, w_ref, cos_ref, sin_ref, kv_ref, krp_ref):
    x = ckv_ref[...]
    y = x * lax.rsqrt(jnp.mean(x * x, axis=-1, keepdims=True) + EPS)
    h = (y * g_ref[...]).astype(BF16)
    n = w_ref.shape[1]
    cw = 512
    for c in range(0, n, cw):
        kv_ref[:, c:c + cw] = _dot(h, w_ref[:, c:c + cw]).astype(BF16)
    slab = kr_ref[...]
    krp_ref[...] = (slab * cos_ref[...] + pltpu.roll(slab, LANE // 2, 1) * sin_ref[...]).astype(BF16)


def _ukv_proj(cqkv, kr2, kv_norm, w_ukv, cos_t, sin_t, s):
    m = cqkv.shape[0]
    n = _w_shape(w_ukv)[1]
    tm = _tile(s, 512)
    nblk = s // tm
    cblk = A_Q_LORA // A_KV_LORA
    w_arr, w_spec = _layer_spec(w_ukv, (A_KV_LORA, n), lambda i: (0, 0))
    return pl.pallas_call(
        _ukv_kernel,
        out_shape=(jax.ShapeDtypeStruct((m, n), BF16), jax.ShapeDtypeStruct((m, LANE), BF16)),
        grid=(m // tm,),
        in_specs=[
            pl.BlockSpec((tm, A_KV_LORA), lambda i: (i, cblk)),
            pl.BlockSpec((tm, LANE), lambda i: (i, 0)),
            pl.BlockSpec((1, A_KV_LORA), lambda i: (0, 0)),
            w_spec,
            pl.BlockSpec((tm, LANE), lambda i: (i % nblk, 0)),
            pl.BlockSpec((tm, LANE), lambda i: (i % nblk, 0)),
        ],
        out_specs=(pl.BlockSpec((tm, n), lambda i: (i, 0)), pl.BlockSpec((tm, LANE), lambda i: (i, 0))),
        compiler_params=_cp("parallel"),
        name="mla_kv_up",
    )(cqkv, kr2, kv_norm.reshape(1, -1).astype(F32), w_arr, cos_t, sin_t)


def _rep2(x):
    return jnp.concatenate([x, x], axis=1)


def _silu(g):
    return g * (1.0 / (1.0 + jnp.exp(-g)))


def _mla_attn_kernel(qn_ref, qr_ref, kv_ref, kr_ref, g_ref, o_ref, kcat, vaug, m_scr, acc_scr, *, tq, hb):
    i = pl.program_id(2)

    @pl.when(i == 0)
    def _():
        ones = jnp.ones((kv_ref.shape[0], LANE), BF16)
        for hh in range(hb):
            kcat[hh, :, :LANE] = kv_ref[:, 2 * hh * LANE:(2 * hh + 1) * LANE]
            kcat[hh, :, LANE:] = kr_ref[...]
            vaug[hh, :, :LANE] = kv_ref[:, (2 * hh + 1) * LANE:(2 * hh + 2) * LANE]
            vaug[hh, :, LANE:] = ones

    m_scr[...] = jnp.full(m_scr.shape, -jnp.inf, F32)
    acc_scr[...] = jnp.zeros(acc_scr.shape, F32)
    qs = [jnp.concatenate([qn_ref[:, hh * LANE:(hh + 1) * LANE], qr_ref[:, hh * LANE:(hh + 1) * LANE]], axis=1)
          for hh in range(hb)]

    n_wide = lax.shift_right_logical(i, 1)

    def step(k0, tk, masked):
        for hh in range(hb):
            s = _dot_t(qs[hh], kcat[hh, pl.ds(k0, tk), :])
            if masked:
                row = lax.broadcasted_iota(jnp.int32, s.shape, 0)
                col = lax.broadcasted_iota(jnp.int32, s.shape, 1)
                s = jnp.where(row >= col, s, NEG_INF)
            m_prev = m_scr[hh]
            m_new = jnp.maximum(m_prev, jnp.max(s, axis=1, keepdims=True))
            alpha = jnp.exp2(m_prev - m_new)
            p = jnp.exp2(s - jnp.concatenate([m_new] * (tk // LANE), axis=1))
            acc_scr[hh] = _rep2(alpha) * acc_scr[hh] + _dot(p.astype(BF16), vaug[hh, pl.ds(k0, tk), :])
            m_scr[hh] = m_new

    def body(j, c):
        step(pl.multiple_of(j * 2 * tq, 2 * tq), 2 * tq, False)
        return c

    lax.fori_loop(0, n_wide, body, 0)

    @pl.when((i & 1) == 1)
    def _():
        step(pl.multiple_of((i - 1) * tq, tq), tq, False)

    step(pl.multiple_of(i * tq, tq), tq, True)
    for hh in range(hb):
        acc = acc_scr[hh]
        cols = slice(hh * LANE, (hh + 1) * LANE)
        gate = _silu(g_ref[:, cols].astype(F32))
        o_ref[:, cols] = (acc[:, :LANE] / acc[:, LANE:] * gate).astype(o_ref.dtype)


def _mla_attention(qp, kv, krp, g3, b, s):
    tq = _tile(s, 256)
    hn = A_HEADS
    hb = 8
    ng = hn // hb
    return pl.pallas_call(
        functools.partial(_mla_attn_kernel, tq=tq, hb=hb),
        out_shape=jax.ShapeDtypeStruct((b, s, hn * A_V), BF16),
        grid=(b, ng, s // tq),
        in_specs=[
            pl.BlockSpec((None, tq, hb * LANE), lambda bb, h, i: (bb, i, h)),
            pl.BlockSpec((None, tq, hb * LANE), lambda bb, h, i: (bb, i, ng + h)),
            pl.BlockSpec((None, s, 2 * hb * LANE), lambda bb, h, i: (bb, 0, h)),
            pl.BlockSpec((None, s, LANE), lambda bb, h, i: (bb, 0, 0)),
            pl.BlockSpec((None, tq, hb * LANE), lambda bb, h, i: (bb, i, h)),
        ],
        out_specs=pl.BlockSpec((None, tq, hb * LANE), lambda bb, h, i: (bb, i, h)),
        scratch_shapes=[
            pltpu.VMEM((hb, s, 2 * LANE), BF16),
            pltpu.VMEM((hb, s, 2 * LANE), BF16),
            pltpu.VMEM((hb, tq, LANE), F32),
            pltpu.VMEM((hb, tq, 2 * LANE), F32),
        ],
        compiler_params=_cp("parallel", "parallel", "arbitrary"),
        name="mla_attention",
    )(qp, qp, kv, krp, g3)


def _mem_attn_kernel(q_ref, kv_ref, g_ref, o_ref):
    d = MEM_HEAD_DIM
    for h in range(MEM_HEADS):
        cols = slice(h * d, (h + 1) * d)
        k = kv_ref[:, cols]
        v = kv_ref[:, MEM_WIDTH + h * d:MEM_WIDTH + (h + 1) * d]
        s = _dot_t(q_ref[:, cols], k) * (d ** -0.5)
        m = jnp.max(s, axis=1, keepdims=True)
        e = jnp.exp(s - m)
        p = e / jnp.sum(e, axis=1, keepdims=True)
        o = _dot(p.astype(BF16), v) * _silu(g_ref[:, cols].astype(F32))
        o_ref[:, cols] = o.astype(o_ref.dtype)


def _mem_attention(g3, mq_blk, gate_blk, memkv3):
    b, s, _ = g3.shape
    ml = memkv3.shape[1]
    tq = _tile(s, 512)
    return pl.pallas_call(
        _mem_attn_kernel,
        out_shape=jax.ShapeDtypeStruct((b, s, MEM_WIDTH), BF16),
        grid=(b, s // tq),
        in_specs=[
            pl.BlockSpec((None, tq, MEM_WIDTH), lambda bb, i: (bb, i, mq_blk)),
            pl.BlockSpec((None, ml, 2 * MEM_WIDTH), lambda bb, i: (bb, 0, 0)),
            pl.BlockSpec((None, tq, MEM_WIDTH), lambda bb, i: (bb, i, gate_blk)),
        ],
        out_specs=pl.BlockSpec((None, tq, MEM_WIDTH), lambda bb, i: (bb, i, 0)),
        compiler_params=_cp("parallel", "parallel"),
        name="mem_attention",
    )(g3, memkv3, g3)


def _out_kernel(ys_ref, ym_ref, x_ref, w_ref, g_ref, *refs, keep_x):
    if keep_x:
        xo_ref, h_ref, xrow = refs
    else:
        h_ref, xrow = refs
    j = pl.program_id(1)
    ws = ys_ref.shape[1]
    tn = x_ref.shape[1]
    xn = x_ref[...] + _dot(ys_ref[...], w_ref[:ws, :]) + _dot(ym_ref[...], w_ref[ws:, :])
    xrow[:, pl.ds(pl.multiple_of(j * tn, tn), tn)] = xn
    if keep_x:
        xo_ref[...] = xn

    @pl.when(j == pl.num_programs(1) - 1)
    def _():
        x = xrow[...]
        y = x * lax.rsqrt(jnp.mean(x * x, axis=-1, keepdims=True) + EPS)
        h_ref[...] = (y * g_ref[...]).astype(h_ref.dtype)


def _out_proj(ys, ym, x2d, w_out, g_next, keep_x):
    m, ws = ys.shape
    wm = ym.shape[1]
    kk, n = _w_shape(w_out)
    tm = _tile(m, 1024)
    tn = _tile(n, 512)
    w_arr, w_spec = _layer_spec(w_out, (kk, tn), lambda i, j: (0, j))
    h_dtype = BF16 if keep_x else F32
    row_spec = pl.BlockSpec((tm, n), lambda i, j: (i, 0))
    tile_spec = pl.BlockSpec((tm, tn), lambda i, j: (i, j))
    out_shape = [jax.ShapeDtypeStruct((m, n), h_dtype)]
    out_specs = [row_spec]
    if keep_x:
        out_shape.insert(0, jax.ShapeDtypeStruct((m, n), F32))
        out_specs.insert(0, tile_spec)
    return pl.pallas_call(
        functools.partial(_out_kernel, keep_x=keep_x),
        out_shape=tuple(out_shape),
        grid=(m // tm, n // tn),
        in_specs=[
            pl.BlockSpec((tm, ws), lambda i, j: (i, 0)),
            pl.BlockSpec((tm, wm), lambda i, j: (i, 0)),
            tile_spec,
            w_spec,
            pl.BlockSpec((1, n), lambda i, j: (0, 0)),
        ],
        out_specs=tuple(out_specs),
        scratch_shapes=[pltpu.VMEM((tm, n), F32)],
        compiler_params=_cp("parallel", "arbitrary"),
        name="out_proj_norm",
    )(ys, ym, x2d, w_arr, g_next.reshape(1, n).astype(F32))


def _t5_bucket(rel):
    n = jnp.maximum(rel, 0)
    max_exact = N_BUCKETS // 2
    nf = jnp.maximum(n, 1).astype(F32)
    large = max_exact + (jnp.log(nf / max_exact) / math.log(MAX_DISTANCE / max_exact)
                         * (N_BUCKETS - max_exact)).astype(jnp.int32)
    large = jnp.minimum(large, N_BUCKETS - 1)
    return jnp.where(n < max_exact, n, large)


def _bias_kernel(rb_ref, bd_ref, bp_ref, tswa_ref, tdsa_ref):
    bd = bd_ref[...]
    bp = bp_ref[...]
    for h in range(N_BIAS_HEADS):
        g, j = divmod(h, N_BIAS_HEADS // B_KV_HEADS)

        def body(b, c, h=h):
            d, p = c
            v = rb_ref[b, h]
            return jnp.where(bd == b, v, d), jnp.where(bp == b, v, p)

        z = jnp.zeros((QB, QB), F32)
        d, p = lax.fori_loop(0, N_BUCKETS, body, (z, z))
        far = rb_ref[N_BUCKETS - 1, h]
        rows = pl.ds(j * QB, QB)
        t = lax.broadcasted_iota(jnp.int32, (QB, QB), 0)
        c = lax.broadcasted_iota(jnp.int32, (QB, QB), 1)
        tswa_ref[0, g, rows, :] = jnp.where(t >= c, d * LOG2E, NEG_INF)
        tswa_ref[1, g, rows, :] = jnp.where(t + QB - c < WINDOW, p * LOG2E, NEG_INF)
        tdsa_ref[0, g, rows, :] = z
        tdsa_ref[1, g, rows, :] = (d - far) * LOG2E
        tdsa_ref[2, g, rows, :] = (p - far) * LOG2E


def _bias_tables(rel_bias):
    t = jnp.arange(QB)[:, None]
    c = jnp.arange(QB)[None, :]
    bd = _t5_bucket(t - c).astype(jnp.int32)
    bp = _t5_bucket(t + QB - c).astype(jnp.int32)
    rows = (N_BIAS_HEADS // B_KV_HEADS) * QB
    return pl.pallas_call(
        _bias_kernel,
        out_shape=(jax.ShapeDtypeStruct((2, B_KV_HEADS, rows, QB), F32),
                   jax.ShapeDtypeStruct((3, B_KV_HEADS, rows, QB), F32)),
        in_specs=[pl.BlockSpec(memory_space=pltpu.SMEM),
                  pl.BlockSpec(memory_space=pltpu.VMEM),
                  pl.BlockSpec(memory_space=pltpu.VMEM)],
        out_specs=(pl.BlockSpec(memory_space=pltpu.VMEM), pl.BlockSpec(memory_space=pltpu.VMEM)),
        compiler_params=pltpu.CompilerParams(vmem_limit_bytes=VMEM_LIMIT),
        name="t5_bias_tables",
    )(rel_bias.astype(F32), bd, bp)


def _swa_kernel(sink_ref, q_ref, kp_ref, kc_ref, vp_ref, vc_ref, t_ref, g_ref, o_ref):
    i = pl.program_id(1)
    n_per = C_HEADS // C_KV_HEADS
    d = C_HEAD_DIM
    kb = jnp.concatenate([kp_ref[...], kc_ref[...]], axis=0)
    vb = jnp.concatenate([vp_ref[...], vc_ref[...]], axis=0)
    ones = jnp.ones((2 * QB, d), BF16)
    no_prev = jnp.where(i == 0, NEG_INF, 0.0).astype(F32)
    low = lax.broadcasted_iota(jnp.int32, (QB, LANE), 1) < d
    outs = []
    for g in range(C_KV_HEADS):
        kg = kb[:, g * d:(g + 1) * d]
        vg = vb[:, g * d:(g + 1) * d]
        vaug = (jnp.concatenate([vg, ones, ones, vg], axis=1), jnp.concatenate([ones, vg, vg, ones], axis=1))
        zk = jnp.zeros_like(kg)
        kpar = (jnp.concatenate([kg, zk], axis=1), jnp.concatenate([zk, kg], axis=1))
        for jp in range(n_per // 2):
            pair = []
            q2 = q_ref[:, (g * n_per + 2 * jp) * d:(g * n_per + 2 * jp + 2) * d]
            for par in range(2):
                j = 2 * jp + par
                h = g * n_per + j
                rows = pl.ds(j * QB, QB)
                s = _dot_t(q2, kpar[par])
                s = s + jnp.concatenate([t_ref[1, g, rows, :] + no_prev, t_ref[0, g, rows, :]], axis=1)
                sink = sink_ref[0, h] * LOG2E
                m = jnp.maximum(jnp.broadcast_to(jnp.max(s, axis=1, keepdims=True), (QB, LANE)), sink)
                e = jnp.exp2(s - _rep2(m))
                acc = _dot(e.astype(BF16), vaug[par])
                pair.append(acc[:, :LANE] / (acc[:, LANE:] + jnp.exp2(sink - m)))
            outs.append(jnp.where(low, pair[0], pair[1]))
    o = jnp.concatenate(outs, axis=1) * _silu(g_ref[...].astype(F32))
    o_ref[...] = o.astype(o_ref.dtype)


def _swa_attention(g3, sinks, tswa, q_blk, k_blk, v_blk):
    b, s, _ = g3.shape
    wq = C_HEADS * C_HEAD_DIM
    wk = C_KV_HEADS * C_HEAD_DIM
    prev = lambda i: jnp.maximum(i - 1, 0)
    return pl.pallas_call(
        _swa_kernel,
        out_shape=jax.ShapeDtypeStruct((b, s, wq), BF16),
        grid=(b, s // QB),
        in_specs=[
            pl.BlockSpec(memory_space=pltpu.SMEM),
            pl.BlockSpec((None, QB, wq), lambda bb, i: (bb, i, q_blk)),
            pl.BlockSpec((None, QB, wk), lambda bb, i: (bb, prev(i), k_blk)),
            pl.BlockSpec((None, QB, wk), lambda bb, i: (bb, i, k_blk)),
            pl.BlockSpec((None, QB, wk), lambda bb, i: (bb, prev(i), v_blk)),
            pl.BlockSpec((None, QB, wk), lambda bb, i: (bb, i, v_blk)),
            pl.BlockSpec(tswa.shape, lambda bb, i: (0, 0, 0, 0)),
            pl.BlockSpec((None, QB, wq), lambda bb, i: (bb, i, 0)),
        ],
        out_specs=pl.BlockSpec((None, QB, wq), lambda bb, i: (bb, i, 0)),
        compiler_params=_cp("parallel", "parallel"),
        name="swa_attention",
    )(sinks.reshape(1, -1).astype(F32), g3, g3, g3, g3, g3, tswa, g3)


def _ordered_to_f32(key):
    bits = jnp.where(key >= 0, key, key ^ jnp.int32(0x7FFFFFFF))
    return lax.bitcast_convert_type(bits, F32)


def _dsa_kernel(q_ref, k_ref, v_ref, iq_ref, ik_ref, iw_ref, t_ref, g_ref, o_ref,
                sc, madd, iqs, ks, vaug, m_scr, acc_scr, *, s_len, k_top):
    i = pl.program_id(1)
    n_per = B_HEADS // B_KV_HEADS
    d = B_HEAD_DIM

    @pl.when(i == 0)
    def _():
        ones = jnp.ones((s_len, d), BF16)
        zeros = jnp.zeros((s_len, d), BF16)
        for g in range(B_KV_HEADS):
            kg = k_ref[:, g * d:(g + 1) * d]
            vg = v_ref[:, g * d:(g + 1) * d]
            ks[g, 0] = jnp.concatenate([kg, zeros], axis=1)
            ks[g, 1] = jnp.concatenate([zeros, kg], axis=1)
            vaug[g, 0] = jnp.concatenate([vg, ones], axis=1)
            vaug[g, 1] = jnp.concatenate([ones, vg], axis=1)

    half = lax.shift_right_logical(i, 1)
    odd = i & 1
    n_t = half + 1
    sub = 8
    key_i = lax.broadcasted_iota(jnp.int32, (KT, QB), 0)
    qry_i = i * QB + lax.broadcasted_iota(jnp.int32, (KT, QB), 1)

    for hp in range(IDX_HEADS // 2):
        for par in range(2):
            h = 2 * hp + par
            iqs[hp, par * QB:(par + 1) * QB, :] = iq_ref[:, h * IDX_DIM:(h + 1) * IDX_DIM]
    w = iw_ref[...] * (IDX_DIM ** -0.5 * IDX_HEADS ** -0.5)
    wrows = [w[h:h + 1, :] for h in range(IDX_HEADS)]

    def score_tile(kt, c):
        k0 = pl.multiple_of(kt * KT, KT)
        ikt = ik_ref[pl.ds(k0, KT), :]
        acc = jnp.zeros((KT, QB), F32)
        for hp in range(IDX_HEADS // 2):
            dots = _dot_t(ikt, iqs[hp])
            acc = acc + jnp.maximum(dots[:, :QB], 0.0) * wrows[2 * hp]
            acc = acc + jnp.maximum(dots[:, QB:], 0.0) * wrows[2 * hp + 1]
        sc[pl.ds(k0, KT), :] = jnp.where(qry_i >= k0 + key_i, acc, NEG_INF)
        return c

    lax.fori_loop(0, n_t, score_tile, 0)

    tail = (s_len - n_t * KT).astype(F32)
    int_min = jnp.int32(-2 ** 31)

    def bisect(b, prefix):
        cand = prefix + lax.shift_left(jnp.int32(1), 31 - b)
        cand_f = _ordered_to_f32(cand)

        def count_tile(kt, c):
            k0 = pl.multiple_of(kt * KT, KT)
            t = sc[pl.ds(k0, KT), :].reshape(4, KT // (4 * sub), sub, QB)
            part = jnp.sum(jnp.where(t >= cand_f[None, None], 1.0, 0.0), axis=1)
            return c + ((part[0] + part[1]) + (part[2] + part[3]))

        cnt = lax.fori_loop(0, n_t, count_tile, jnp.zeros((sub, QB), F32))
        total = jnp.sum(cnt, axis=0, keepdims=True) + jnp.where(cand_f <= NEG_INF, tail, 0.0)
        return jnp.where(total >= k_top, cand, prefix)

    thr = _ordered_to_f32(lax.fori_loop(0, 32, bisect, jnp.full((sub, QB), int_min, jnp.int32)))[:1]

    def mask_tile(kt, c):
        k0 = pl.multiple_of(kt * KT, KT)
        sel = (sc[pl.ds(k0, KT), :] >= thr) & (qry_i >= k0 + key_i)
        madd[:, pl.ds(k0, KT)] = jnp.where(sel, 0.0, NEG_INF).T
        return c

    lax.fori_loop(0, n_t, mask_tile, 0)

    near0 = jnp.where(odd == 1, half, jnp.maximum(half - 1, 0))
    m_scr[...] = jnp.full(m_scr.shape, -jnp.inf, F32)
    acc_scr[...] = jnp.zeros(acc_scr.shape, F32)

    def step(kt, near):
        k0 = pl.multiple_of(kt * KT, KT)
        mt = madd[:, pl.ds(k0, KT)]
        if near:
            li = jnp.where(odd == 1, 2, jnp.where(kt == half, 1, 0))
            ri = jnp.where(odd == 1, 1, jnp.where(kt == half, 0, 2))
        for g in range(B_KV_HEADS):
            kg = (ks[g, 0, pl.ds(k0, KT), :], ks[g, 1, pl.ds(k0, KT), :])
            vg = (vaug[g, 0, pl.ds(k0, KT), :], vaug[g, 1, pl.ds(k0, KT), :])
            for j in range(n_per):
                h = g * n_per + j
                q2 = q_ref[:, (h // 2) * LANE:(h // 2 + 1) * LANE]
                s = _dot_t(q2, kg[h % 2]) + mt
                if near:
                    rows = pl.ds(j * QB, QB)
                    s = s + jnp.concatenate([t_ref[li, g, rows, :], t_ref[ri, g, rows, :]], axis=1)
                m_prev = m_scr[h]
                m_new = jnp.maximum(m_prev, jnp.max(s, axis=1, keepdims=True))
                alpha = jnp.exp2(m_prev - m_new)
                p = jnp.exp2(s - _rep2(m_new))
                acc_scr[h] = alpha * acc_scr[h] + _dot(p.astype(BF16), vg[h % 2])
                m_scr[h] = m_new

    def far_body(kt, c):
        step(kt, False)
        return c

    def near_body(kt, c):
        step(kt, True)
        return c

    lax.fori_loop(0, near0, far_body, 0)
    lax.fori_loop(near0, n_t, near_body, 0)
    low = lax.broadcasted_iota(jnp.int32, (QB, LANE), 1) < d
    outs = []
    for hp in range(B_HEADS // 2):
        acc_e = acc_scr[2 * hp]
        acc_o = acc_scr[2 * hp + 1]
        num = jnp.where(low, acc_e, acc_o)
        den = pltpu.roll(jnp.where(low, acc_o, acc_e), d, 1)
        outs.append(num / den)
    o = jnp.concatenate(outs, axis=1) * _silu(g_ref[...].astype(F32))
    o_ref[...] = o.astype(o_ref.dtype)


def _dsa_attention(g3, ik3, iwt3, tdsa, q_blk, k_blk, v_blk, iq_blk):
    b, s, _ = g3.shape
    k_top = min(IDX_TOPK_MAX, s // 4)
    wq = B_HEADS * B_HEAD_DIM
    wk = B_KV_HEADS * B_HEAD_DIM
    wi = IDX_HEADS * IDX_DIM
    return pl.pallas_call(
        functools.partial(_dsa_kernel, s_len=s, k_top=k_top),
        out_shape=jax.ShapeDtypeStruct((b, s, wq), BF16),
        grid=(b, s // QB),
        in_specs=[
            pl.BlockSpec((None, QB, wq), lambda bb, i: (bb, i, q_blk)),
            pl.BlockSpec((None, s, wk), lambda bb, i: (bb, 0, k_blk)),
            pl.BlockSpec((None, s, wk), lambda bb, i: (bb, 0, v_blk)),
            pl.BlockSpec((None, QB, wi), lambda bb, i: (bb, i, iq_blk)),
            pl.BlockSpec((None, s, IDX_DIM), lambda bb, i: (bb, 0, 0)),
            pl.BlockSpec((None, IDX_HEADS, QB), lambda bb, i: (bb, 0, i)),
            pl.BlockSpec(tdsa.shape, lambda bb, i: (0, 0, 0, 0)),
            pl.BlockSpec((None, QB, wq), lambda bb, i: (bb, i, 0)),
        ],
        out_specs=pl.BlockSpec((None, QB, wq), lambda bb, i: (bb, i, 0)),
        scratch_shapes=[
            pltpu.VMEM((s, QB), F32),
            pltpu.VMEM((QB, s), F32),
            pltpu.VMEM((IDX_HEADS // 2, 2 * QB, IDX_DIM), BF16),
            pltpu.VMEM((B_KV_HEADS, 2, s, 2 * B_HEAD_DIM), BF16),
            pltpu.VMEM((B_KV_HEADS, 2, s, 2 * B_HEAD_DIM), BF16),
            pltpu.VMEM((B_HEADS, QB, LANE), F32),
            pltpu.VMEM((B_HEADS, QB, 2 * B_HEAD_DIM), F32),
        ],
        compiler_params=_cp("parallel", "arbitrary"),
        name="dsa_attention",
    )(g3, g3, g3, g3, ik3, iwt3, tdsa, g3)


def _mla_layer(h, w_in, q_norm, w_uq, kv_norm, w_ukv, rope1, rope2, b, s):
    o_ckv = A_Q_LORA + A_KV_LORA
    o_mq = o_ckv + A_ROPE
    o_gate = o_mq + MEM_WIDTH
    half = A_ROPE // 2
    w_f, w_kr2, w_g = _repack(w_in, (
        _pieces((0, o_ckv)),
        _pieces((o_ckv, A_ROPE), (o_ckv + half, half), (o_ckv, half)),
        _pieces((o_gate, A_HEADS * A_V + MEM_WIDTH), (o_mq, MEM_WIDTH)),
    ))
    cqkv = _matmul(h, w_f, F32)
    kr2 = _matmul(h, w_kr2, F32)
    g2 = _matmul(h, w_g, BF16)

    hd = A_NOPE + A_ROPE
    (w_uq2,) = _repack(w_uq, (
        _pieces(*[(hh * hd, A_NOPE, hd ** -0.5 * LOG2E) for hh in range(A_HEADS)],
                *[(hh * hd + A_NOPE, A_ROPE, hd ** -0.5 * LOG2E) for hh in range(A_HEADS)]),
    ))
    qp = _uq_proj(cqkv, q_norm, w_uq2, rope2[0], rope2[1], s)
    kv, krp = _ukv_proj(cqkv, kr2, kv_norm, w_ukv, rope1[0], rope1[1], s)
    self_out = _mla_attention(qp.reshape(b, s, -1), kv.reshape(b, s, -1), krp.reshape(b, s, -1),
                              g2.reshape(b, s, -1), b, s)
    return self_out.reshape(b * s, -1), g2


def _dsa_layer(h, w_in, tdsa, b, s):
    wq = B_HEADS * B_HEAD_DIM
    wk = B_KV_HEADS * B_HEAD_DIM
    wi = IDX_HEADS * IDX_DIM
    o_k = wq
    o_v = o_k + wk
    o_iq = o_v + wk
    o_ik = o_iq + wi
    o_iw = o_ik + IDX_DIM
    o_mq = o_iw + IDX_HEADS
    o_gate = o_mq + MEM_WIDTH
    w_g, w_i = _repack(w_in, (
        _pieces((o_gate, wq + MEM_WIDTH), (o_mq, MEM_WIDTH), (0, wq, B_HEAD_DIM ** -0.5 * LOG2E),
                (o_iq, wi), (o_k, wk), (o_v, wk)),
        _pieces((o_ik, IDX_DIM + IDX_HEADS)),
    ))
    g2 = _matmul(h, w_g, BF16)
    ii = _matmul(h, w_i, F32)
    ik3 = ii[:, :IDX_DIM].astype(BF16).reshape(b, s, IDX_DIM)
    iwt3 = ii[:, IDX_DIM:].reshape(b, s, IDX_HEADS).transpose(0, 2, 1)
    self_out = _dsa_attention(g2.reshape(b, s, -1), ik3, iwt3, tdsa,
                              q_blk=2, k_blk=7168 // wk, v_blk=7424 // wk, iq_blk=6)
    return self_out.reshape(b * s, -1), g2


def _swa_layer(h, w_in, sinks, tswa, b, s):
    wq = C_HEADS * C_HEAD_DIM
    wk = C_KV_HEADS * C_HEAD_DIM
    o_k = wq
    o_v = o_k + wk
    o_mq = o_v + wk
    o_gate = o_mq + MEM_WIDTH
    (w_g,) = _repack(w_in, (
        _pieces((o_gate, wq + MEM_WIDTH), (o_mq, MEM_WIDTH), (0, wq, C_HEAD_DIM ** -0.5 * LOG2E),
                (o_k, wk), (o_v, wk)),
    ))
    g2 = _matmul(h, w_g, BF16)
    self_out = _swa_attention(g2.reshape(b, s, -1), sinks, tswa, q_blk=2, k_blk=6144 // wk, v_blk=6400 // wk)
    return self_out.reshape(b * s, -1), g2


def kernel(x, mem, norm_in, final_norm, mem_norm, rel_bias, w_in_a, a_q_norm, w_uq, a_kv_norm, w_ukv,
           w_in_b, w_in_c, c_sinks, w_mem_kv, w_out):
    b, s, d = x.shape
    depth = norm_in.shape[0]
    ml = mem.shape[1]
    xf = x.reshape(b * s, d)
    mem_n = _rmsnorm(mem.reshape(b * ml, d), mem_norm, BF16)
    rope1, rope2 = _rope_tables(s)
    tswa, tdsa = _bias_tables(rel_bias)
    h = _rmsnorm(xf, norm_in[0], BF16)
    w_ukv_b, w_mem_b, w_out_b = w_ukv.astype(BF16), w_mem_kv.astype(BF16), w_out.astype(BF16)
    for i in range(depth):
        kind, j = i % N_MIXERS, i // N_MIXERS
        if kind == 0:
            ys, g2 = _mla_layer(h, (w_in_a, j), a_q_norm[j], (w_uq, j), a_kv_norm[j], (w_ukv_b, j),
                                rope1, rope2, b, s)
        elif kind == 1:
            ys, g2 = _dsa_layer(h, (w_in_b, j), tdsa, b, s)
        else:
            ys, g2 = _swa_layer(h, (w_in_c, j), c_sinks[j], tswa, b, s)
        memkv = _matmul(mem_n, (w_mem_b, i), BF16)
        ym = _mem_attention(g2.reshape(b, s, -1), 3, 2, memkv.reshape(b, ml, -1)).reshape(b * s, -1)
        last = i == depth - 1
        res = _out_proj(ys, ym, xf, (w_out_b, i), final_norm if last else norm_in[i + 1], not last)
        if last:
            return res[0].reshape(b, s, d)
        xf, h = res
```

```python
import functools
import math

import jax
import jax.numpy as jnp
from jax import lax
from jax.experimental import pallas as pl
from jax.experimental.pallas import tpu as pltpu

F32 = jnp.float32
BF16 = jnp.bfloat16

EPS = 1e-6
NEG_INF = -1e30
N_MIXERS = 3

N_BUCKETS = 32
MAX_DISTANCE = 128
N_BIAS_HEADS = 32

MEM_HEADS = 4
MEM_HEAD_DIM = 256
MEM_WIDTH = MEM_HEADS * MEM_HEAD_DIM

A_HEADS = 16
A_Q_LORA = 1536
A_KV_LORA = 512
A_NOPE = 128
A_ROPE = 64
A_V = 128
ROPE_THETA = 10000.0

B_HEADS = 32
B_KV_HEADS = 4
B_HEAD_DIM = 64
IDX_HEADS = 16
IDX_DIM = 64
IDX_TOPK_MAX = 256

C_HEADS = 32
C_KV_HEADS = 4
C_HEAD_DIM = 64
WINDOW = 128

LANE = 128
QB = 128
KT = 256
VMEM_LIMIT = 56 * 1024 * 1024
LOG2E = math.log2(math.e)
MM_MAX_TN = 2176


def _cp(*sem):
    return pltpu.CompilerParams(dimension_semantics=sem, vmem_limit_bytes=VMEM_LIMIT)


def _tile(n, pref):
    t = min(n, pref)
    assert n % t == 0, (n, pref)
    return t


def _dot(a, b):
    return jnp.dot(a, b, preferred_element_type=F32)


def _dot_t(a, b):
    return lax.dot_general(a, b, (((1,), (1,)), ((), ())), preferred_element_type=F32)


def _rms_kernel(x_ref, g_ref, o_ref):
    x = x_ref[...].astype(F32)
    y = x * lax.rsqrt(jnp.mean(x * x, axis=-1, keepdims=True) + EPS)
    o_ref[...] = (y * g_ref[...]).astype(o_ref.dtype)


def _rmsnorm(x2d, g, out_dtype):
    m, d = x2d.shape
    tm = _tile(m, 512)
    return pl.pallas_call(
        _rms_kernel,
        out_shape=jax.ShapeDtypeStruct((m, d), out_dtype),
        grid=(m // tm,),
        in_specs=[pl.BlockSpec((tm, d), lambda i: (i, 0)), pl.BlockSpec((1, d), lambda i: (0, 0))],
        out_specs=pl.BlockSpec((tm, d), lambda i: (i, 0)),
        compiler_params=_cp("parallel"),
        name="rmsnorm",
    )(x2d, g.reshape(1, d).astype(F32))


def _mm_kernel(a_ref, w_ref, o_ref, *, w_rows_are_outputs):
    dot = _dot_t if w_rows_are_outputs else _dot
    o_ref[...] = dot(a_ref[...], w_ref[...]).astype(o_ref.dtype)


def _layer_spec(w, block, index_map):
    if isinstance(w, tuple):
        stack, layer = w
        return stack, pl.BlockSpec((None,) + block, lambda *g: (layer,) + index_map(*g))
    return w, pl.BlockSpec(block, index_map)


def _w_shape(w):
    return w[0].shape[1:] if isinstance(w, tuple) else w.shape


def _matmul(a, w, out_dtype, w_rows_are_outputs=False):
    m, k = a.shape
    n = _w_shape(w)[0 if w_rows_are_outputs else 1]
    tm = _tile(m, 1024)
    tn = max([t for t in range(LANE, min(n, MM_MAX_TN) + 1, LANE) if n % t == 0], default=n)
    if w_rows_are_outputs:
        w_arr, w_spec = _layer_spec(w, (tn, k), lambda i, j: (j, 0))
    else:
        w_arr, w_spec = _layer_spec(w, (k, tn), lambda i, j: (0, j))
    return pl.pallas_call(
        functools.partial(_mm_kernel, w_rows_are_outputs=w_rows_are_outputs),
        out_shape=jax.ShapeDtypeStruct((m, n), out_dtype),
        grid=(m // tm, n // tn),
        in_specs=[pl.BlockSpec((tm, k), lambda i, j: (i, 0)), w_spec],
        out_specs=pl.BlockSpec((tm, tn), lambda i, j: (i, j)),
        compiler_params=_cp("parallel", "arbitrary"),
        name="matmul",
    )(a, w_arr)


def _repack_kernel(w_ref, *o_refs, groups, by_rows):
    for o_ref, pieces in zip(o_refs, groups):
        for src, width, dst, scale in pieces:
            v = w_ref[src:src + width, :] if by_rows else w_ref[:, src:src + width]
            if scale != 1.0:
                v = v * scale
            if by_rows:
                o_ref[dst:dst + width, :] = v.astype(o_ref.dtype)
            else:
                o_ref[:, dst:dst + width] = v.astype(o_ref.dtype)


def _repack(w, groups, by_rows=False):
    shape = _w_shape(w)
    k = shape[1 if by_rows else 0]
    tk = _tile(k, 256)
    widths = [sum(p[1] for p in pieces) for pieces in groups]
    if by_rows:
        w_arr, w_spec = _layer_spec(w, (shape[0], tk), lambda i: (0, i))
        out_shape = tuple(jax.ShapeDtypeStruct((wd, k), BF16) for wd in widths)
        out_specs = tuple(pl.BlockSpec((wd, tk), lambda i: (0, i)) for wd in widths)
    else:
        w_arr, w_spec = _layer_spec(w, (tk, shape[1]), lambda i: (i, 0))
        out_shape = tuple(jax.ShapeDtypeStruct((k, wd), BF16) for wd in widths)
        out_specs = tuple(pl.BlockSpec((tk, wd), lambda i: (i, 0)) for wd in widths)
    return pl.pallas_call(
        functools.partial(_repack_kernel, groups=groups, by_rows=by_rows),
        out_shape=out_shape,
        grid=(k // tk,),
        in_specs=[w_spec],
        out_specs=out_specs,
        compiler_params=_cp("parallel"),
        name="weight_repack",
    )(w_arr)


def _pieces(*src_width_scale):
    out, dst = [], 0
    for p in src_width_scale:
        src, width = p[0], p[1]
        out.append((src, width, dst, p[2] if len(p) > 2 else 1.0))
        dst += width
    return tuple(out)


def _rope_tables(s):
    d = A_ROPE
    inv = 1.0 / (ROPE_THETA ** (jnp.arange(0, d, 2, dtype=F32) / d))
    ang = jnp.arange(s, dtype=F32)[:, None] * inv[None, :]
    cos, sin = jnp.cos(ang), jnp.sin(ang)
    z = jnp.zeros((s, LANE - d), F32)
    one = (jnp.concatenate([cos, cos, z], axis=1), jnp.concatenate([-sin, sin, z], axis=1))
    two = (jnp.concatenate([cos] * 4, axis=1), jnp.concatenate([-sin, sin] * 2, axis=1))
    return one, two


def _uq_kernel(cq_ref, g_ref, w_ref, cos_ref, sin_ref, o_ref, *, n_nope):
    x = cq_ref[...]
    y = x * lax.rsqrt(jnp.mean(x * x, axis=-1, keepdims=True) + EPS)
    h = (y * g_ref[...]).astype(BF16)
    n = w_ref.shape[1]
    cw = 512
    for c in range(0, n_nope, cw):
        o_ref[:, c:c + cw] = _dot(h, w_ref[:, c:c + cw]).astype(BF16)
    cos = cos_ref[...]
    sin = sin_ref[...]
    lane = lax.broadcasted_iota(jnp.int32, (x.shape[0], LANE), 1)
    first_half = (lane & (A_ROPE - 1)) < A_ROPE // 2
    low = lane < A_ROPE
    for c in range(n_nope, n, cw):
        y = _dot(h, w_ref[:, c:c + cw])
        parts = []
        for k in range(cw // LANE):
            slab = y[:, k * LANE:(k + 1) * LANE]
            partner = jnp.where(first_half, pltpu.roll(slab, LANE - A_ROPE // 2, 1), pltpu.roll(slab, A_ROPE // 2, 1))
            r = slab * cos + partner * sin
            parts.append(jnp.where(low, r, 0.0))
            parts.append(jnp.where(low, pltpu.roll(r, A_ROPE, 1), 0.0))
        d0 = n_nope + 2 * (c - n_nope)
        o_ref[:, d0:d0 + 2 * cw] = jnp.concatenate(parts, axis=1).astype(BF16)


def _uq_proj(cqkv, q_norm, w_uq2, cos_t, sin_t, s):
    m = cqkv.shape[0]
    n = w_uq2.shape[1]
    n_nope = A_HEADS * A_NOPE
    n_out = n_nope + 2 * (n - n_nope)
    tm = _tile(s, 512)
    nblk = s // tm
    return pl.pallas_call(
        functools.partial(_uq_kernel, n_nope=n_nope),
        out_shape=jax.ShapeDtypeStruct((m, n_out), BF16),
        grid=(m // tm,),
        in_specs=[
            pl.BlockSpec((tm, A_Q_LORA), lambda i: (i, 0)),
            pl.BlockSpec((1, A_Q_LORA), lambda i: (0, 0)),
            pl.BlockSpec((A_Q_LORA, n), lambda i: (0, 0)),
            pl.BlockSpec((tm, LANE), lambda i: (i % nblk, 0)),
            pl.BlockSpec((tm, LANE), lambda i: (i % nblk, 0)),
        ],
        out_specs=pl.BlockSpec((tm, n_out), lambda i: (i, 0)),
        compiler_params=_cp("parallel"),
        name="mla_q_up",
    )(cqkv, q_norm.reshape(1, -1).astype(F32), w_uq2, cos_t, sin_t)


def _ukv_kernel(ckv_ref, kr_ref, g_ref, w_ref, cos_ref, sin_ref, kv_ref, krp_ref):
    x = ckv_ref[...]
    y = x * lax.rsqrt(jnp.mean(x * x, axis=-1, keepdims=True) + EPS)
    h = (y * g_ref[...]).astype(BF16)
    n = w_ref.shape[1]
    cw = 512
    for c in range(0, n, cw):
        kv_ref[:, c:c + cw] = _dot(h, w_ref[:, c:c + cw]).astype(BF16)
    slab = kr_ref[...]
    krp_ref[...] = (slab * cos_ref[...] + pltpu.roll(slab, LANE // 2, 1) * sin_ref[...]).astype(BF16)


def _ukv_proj(cqkv, kv_norm, w_ukv, cos_t, sin_t, s):
    m = cqkv.shape[0]
    n = _w_shape(w_ukv)[1]
    tm = _tile(s, 512)
    nblk = s // tm
    cblk = A_Q_LORA // A_KV_LORA
    w_arr, w_spec = _layer_spec(w_ukv, (A_KV_LORA, n), lambda i: (0, 0))
    return pl.pallas_call(
        _ukv_kernel,
        out_shape=(jax.ShapeDtypeStruct((m, n), BF16), jax.ShapeDtypeStruct((m, LANE), BF16)),
        grid=(m // tm,),
        in_specs=[
            pl.BlockSpec((tm, A_KV_LORA), lambda i: (i, cblk)),
            pl.BlockSpec((tm, LANE), lambda i: (i, (A_Q_LORA + A_KV_LORA) // LANE)),
            pl.BlockSpec((1, A_KV_LORA), lambda i: (0, 0)),
            w_spec,
            pl.BlockSpec((tm, LANE), lambda i: (i % nblk, 0)),
            pl.BlockSpec((tm, LANE), lambda i: (i % nblk, 0)),
        ],
        out_specs=(pl.BlockSpec((tm, n), lambda i: (i, 0)), pl.BlockSpec((tm, LANE), lambda i: (i, 0))),
        compiler_params=_cp("parallel"),
        name="mla_kv_up",
    )(cqkv, cqkv, kv_norm.reshape(1, -1).astype(F32), w_arr, cos_t, sin_t)


def _rep2(x):
    return jnp.concatenate([x, x], axis=1)


def _silu(g):
    return g * (1.0 / (1.0 + jnp.exp(-g)))


def _mla_attn_kernel(qn_ref, qr_ref, kv_ref, kr_ref, g_ref, o_ref, kcat, vaug, m_scr, acc_scr, *, tq, hb):
    i = pl.program_id(2)

    @pl.when(i == 0)
    def _():
        ones = jnp.ones((kv_ref.shape[0], LANE), BF16)
        for hh in range(hb):
            kcat[hh, :, :LANE] = kv_ref[:, 2 * hh * LANE:(2 * hh + 1) * LANE]
            kcat[hh, :, LANE:] = kr_ref[...]
            vaug[hh, :, :LANE] = kv_ref[:, (2 * hh + 1) * LANE:(2 * hh + 2) * LANE]
            vaug[hh, :, LANE:] = ones

    m_scr[...] = jnp.full(m_scr.shape, -jnp.inf, F32)
    acc_scr[...] = jnp.zeros(acc_scr.shape, F32)
    qs = [jnp.concatenate([qn_ref[:, hh * LANE:(hh + 1) * LANE], qr_ref[:, hh * LANE:(hh + 1) * LANE]], axis=1)
          for hh in range(hb)]

    n_wide = lax.shift_right_logical(i, 1)

    def step(k0, tk, masked):
        for hh in range(hb):
            s = _dot_t(qs[hh], kcat[hh, pl.ds(k0, tk), :])
            if masked:
                row = lax.broadcasted_iota(jnp.int32, s.shape, 0)
                col = lax.broadcasted_iota(jnp.int32, s.shape, 1)
                s = jnp.where(row >= col, s, NEG_INF)
            m_prev = m_scr[hh]
            m_new = jnp.maximum(m_prev, jnp.max(s, axis=1, keepdims=True))
            alpha = jnp.exp2(m_prev - m_new)
            p = jnp.exp2(s - jnp.concatenate([m_new] * (tk // LANE), axis=1))
            acc_scr[hh] = _rep2(alpha) * acc_scr[hh] + _dot(p.astype(BF16), vaug[hh, pl.ds(k0, tk), :])
            m_scr[hh] = m_new

    def body(j, c):
        step(pl.multiple_of(j * 2 * tq, 2 * tq), 2 * tq, False)
        return c

    lax.fori_loop(0, n_wide, body, 0)

    @pl.when((i & 1) == 1)
    def _():
        step(pl.multiple_of((i - 1) * tq, tq), tq, False)

    step(pl.multiple_of(i * tq, tq), tq, True)
    for hh in range(hb):
        acc = acc_scr[hh]
        cols = slice(hh * LANE, (hh + 1) * LANE)
        gate = _silu(g_ref[:, cols].astype(F32))
        o_ref[:, cols] = (acc[:, :LANE] / acc[:, LANE:] * gate).astype(o_ref.dtype)


def _mla_attention(qp, kv, krp, g3, b, s):
    tq = _tile(s, 256)
    hn = A_HEADS
    hb = 8
    ng = hn // hb
    return pl.pallas_call(
        functools.partial(_mla_attn_kernel, tq=tq, hb=hb),
        out_shape=jax.ShapeDtypeStruct((b, s, hn * A_V), BF16),
        grid=(b, ng, s // tq),
        in_specs=[
            pl.BlockSpec((None, tq, hb * LANE), lambda bb, h, i: (bb, i, h)),
            pl.BlockSpec((None, tq, hb * LANE), lambda bb, h, i: (bb, i, ng + h)),
            pl.BlockSpec((None, s, 2 * hb * LANE), lambda bb, h, i: (bb, 0, h)),
            pl.BlockSpec((None, s, LANE), lambda bb, h, i: (bb, 0, 0)),
            pl.BlockSpec((None, tq, hb * LANE), lambda bb, h, i: (bb, i, h)),
        ],
        out_specs=pl.BlockSpec((None, tq, hb * LANE), lambda bb, h, i: (bb, i, h)),
        scratch_shapes=[
            pltpu.VMEM((hb, s, 2 * LANE), BF16),
            pltpu.VMEM((hb, s, 2 * LANE), BF16),
            pltpu.VMEM((hb, tq, LANE), F32),
            pltpu.VMEM((hb, tq, 2 * LANE), F32),
        ],
        compiler_params=_cp("parallel", "parallel", "arbitrary"),
        name="mla_attention",
    )(qp, qp, kv, krp, g3)


def _mem_attn_kernel(q_ref, kv_ref, g_ref, o_ref):
    d = MEM_HEAD_DIM
    for h in range(MEM_HEADS):
        cols = slice(h * d, (h + 1) * d)
        k = kv_ref[:, cols]
        v = kv_ref[:, MEM_WIDTH + h * d:MEM_WIDTH + (h + 1) * d]
        s = _dot_t(q_ref[:, cols], k) * (d ** -0.5)
        m = jnp.max(s, axis=1, keepdims=True)
        e = jnp.exp(s - m)
        p = e / jnp.sum(e, axis=1, keepdims=True)
        o = _dot(p.astype(BF16), v) * _silu(g_ref[:, cols].astype(F32))
        o_ref[:, cols] = o.astype(o_ref.dtype)


def _mem_attention(g3, mq_blk, gate_blk, memkv3):
    b, s, _ = g3.shape
    ml = memkv3.shape[1]
    tq = _tile(s, 512)
    return pl.pallas_call(
        _mem_attn_kernel,
        out_shape=jax.ShapeDtypeStruct((b, s, MEM_WIDTH), BF16),
        grid=(b, s // tq),
        in_specs=[
            pl.BlockSpec((None, tq, MEM_WIDTH), lambda bb, i: (bb, i, mq_blk)),
            pl.BlockSpec((None, ml, 2 * MEM_WIDTH), lambda bb, i: (bb, 0, 0)),
            pl.BlockSpec((None, tq, MEM_WIDTH), lambda bb, i: (bb, i, gate_blk)),
        ],
        out_specs=pl.BlockSpec((None, tq, MEM_WIDTH), lambda bb, i: (bb, i, 0)),
        compiler_params=_cp("parallel", "parallel"),
        name="mem_attention",
    )(g3, memkv3, g3)


def _out_kernel(ys_ref, ym_ref, x_ref, w_ref, g_ref, *refs, keep_x):
    if keep_x:
        xo_ref, h_ref, xrow = refs
    else:
        h_ref, xrow = refs
    j = pl.program_id(1)
    ws = ys_ref.shape[1]
    tn = x_ref.shape[1]
    xn = x_ref[...] + _dot(ys_ref[...], w_ref[:ws, :]) + _dot(ym_ref[...], w_ref[ws:, :])
    xrow[:, pl.ds(pl.multiple_of(j * tn, tn), tn)] = xn
    if keep_x:
        xo_ref[...] = xn

    @pl.when(j == pl.num_programs(1) - 1)
    def _():
        x = xrow[...]
        y = x * lax.rsqrt(jnp.mean(x * x, axis=-1, keepdims=True) + EPS)
        h_ref[...] = (y * g_ref[...]).astype(h_ref.dtype)


def _out_proj(ys, ym, x2d, w_out, g_next, keep_x):
    m, ws = ys.shape
    wm = ym.shape[1]
    kk, n = _w_shape(w_out)
    tm = _tile(m, 1024)
    tn = _tile(n, 512)
    w_arr, w_spec = _layer_spec(w_out, (kk, tn), lambda i, j: (0, j))
    h_dtype = BF16 if keep_x else F32
    row_spec = pl.BlockSpec((tm, n), lambda i, j: (i, 0))
    tile_spec = pl.BlockSpec((tm, tn), lambda i, j: (i, j))
    out_shape = [jax.ShapeDtypeStruct((m, n), h_dtype)]
    out_specs = [row_spec]
    if keep_x:
        out_shape.insert(0, jax.ShapeDtypeStruct((m, n), F32))
        out_specs.insert(0, tile_spec)
    return pl.pallas_call(
        functools.partial(_out_kernel, keep_x=keep_x),
        out_shape=tuple(out_shape),
        grid=(m // tm, n // tn),
        in_specs=[
            pl.BlockSpec((tm, ws), lambda i, j: (i, 0)),
            pl.BlockSpec((tm, wm), lambda i, j: (i, 0)),
            tile_spec,
            w_spec,
            pl.BlockSpec((1, n), lambda i, j: (0, 0)),
        ],
        out_specs=tuple(out_specs),
        scratch_shapes=[pltpu.VMEM((tm, n), F32)],
        compiler_params=_cp("parallel", "arbitrary"),
        name="out_proj_norm",
    )(ys, ym, x2d, w_arr, g_next.reshape(1, n).astype(F32))


def _t5_bucket(rel):
    n = jnp.maximum(rel, 0)
    max_exact = N_BUCKETS // 2
    nf = jnp.maximum(n, 1).astype(F32)
    large = max_exact + (jnp.log(nf / max_exact) / math.log(MAX_DISTANCE / max_exact)
                         * (N_BUCKETS - max_exact)).astype(jnp.int32)
    large = jnp.minimum(large, N_BUCKETS - 1)
    return jnp.where(n < max_exact, n, large)


def _bias_kernel(rb_ref, bd_ref, bp_ref, tswa_ref, tdsa_ref):
    bd = bd_ref[...]
    bp = bp_ref[...]
    for h in range(N_BIAS_HEADS):
        g, j = divmod(h, N_BIAS_HEADS // B_KV_HEADS)

        def body(b, c, h=h):
            d, p = c
            v = rb_ref[b, h]
            return jnp.where(bd == b, v, d), jnp.where(bp == b, v, p)

        z = jnp.zeros((QB, QB), F32)
        d, p = lax.fori_loop(0, N_BUCKETS, body, (z, z))
        far = rb_ref[N_BUCKETS - 1, h]
        rows = pl.ds(j * QB, QB)
        t = lax.broadcasted_iota(jnp.int32, (QB, QB), 0)
        c = lax.broadcasted_iota(jnp.int32, (QB, QB), 1)
        tswa_ref[0, g, rows, :] = jnp.where(t >= c, d * LOG2E, NEG_INF)
        tswa_ref[1, g, rows, :] = jnp.where(t + QB - c < WINDOW, p * LOG2E, NEG_INF)
        tdsa_ref[0, g, rows, :] = z
        tdsa_ref[1, g, rows, :] = (d - far) * LOG2E
        tdsa_ref[2, g, rows, :] = (p - far) * LOG2E


def _bias_tables(rel_bias):
    t = jnp.arange(QB)[:, None]
    c = jnp.arange(QB)[None, :]
    bd = _t5_bucket(t - c).astype(jnp.int32)
    bp = _t5_bucket(t + QB - c).astype(jnp.int32)
    rows = (N_BIAS_HEADS // B_KV_HEADS) * QB
    return pl.pallas_call(
        _bias_kernel,
        out_shape=(jax.ShapeDtypeStruct((2, B_KV_HEADS, rows, QB), F32),
                   jax.ShapeDtypeStruct((3, B_KV_HEADS, rows, QB), F32)),
        in_specs=[pl.BlockSpec(memory_space=pltpu.SMEM),
                  pl.BlockSpec(memory_space=pltpu.VMEM),
                  pl.BlockSpec(memory_space=pltpu.VMEM)],
        out_specs=(pl.BlockSpec(memory_space=pltpu.VMEM), pl.BlockSpec(memory_space=pltpu.VMEM)),
        compiler_params=pltpu.CompilerParams(vmem_limit_bytes=VMEM_LIMIT),
        name="t5_bias_tables",
    )(rel_bias.astype(F32), bd, bp)


def _swa_kernel(sink_ref, q_ref, kp_ref, kc_ref, vp_ref, vc_ref, t_ref, g_ref, o_ref):
    i = pl.program_id(1)
    n_per = C_HEADS // C_KV_HEADS
    d = C_HEAD_DIM
    kb = jnp.concatenate([kp_ref[...], kc_ref[...]], axis=0)
    vb = jnp.concatenate([vp_ref[...], vc_ref[...]], axis=0)
    ones = jnp.ones((2 * QB, d), BF16)
    no_prev = jnp.where(i == 0, NEG_INF, 0.0).astype(F32)
    low = lax.broadcasted_iota(jnp.int32, (QB, LANE), 1) < d
    outs = []
    for g in range(C_KV_HEADS):
        kg = kb[:, g * d:(g + 1) * d]
        vg = vb[:, g * d:(g + 1) * d]
        vaug = (jnp.concatenate([vg, ones, ones, vg], axis=1), jnp.concatenate([ones, vg, vg, ones], axis=1))
        zk = jnp.zeros_like(kg)
        kpar = (jnp.concatenate([kg, zk], axis=1), jnp.concatenate([zk, kg], axis=1))
        for jp in range(n_per // 2):
            pair = []
            q2 = q_ref[:, (g * n_per + 2 * jp) * d:(g * n_per + 2 * jp + 2) * d]
            for par in range(2):
                j = 2 * jp + par
                h = g * n_per + j
                rows = pl.ds(j * QB, QB)
                s = _dot_t(q2, kpar[par])
                s = s + jnp.concatenate([t_ref[1, g, rows, :] + no_prev, t_ref[0, g, rows, :]], axis=1)
                sink = sink_ref[0, h] * LOG2E
                m = jnp.maximum(jnp.broadcast_to(jnp.max(s, axis=1, keepdims=True), (QB, LANE)), sink)
                e = jnp.exp2(s - _rep2(m))
                acc = _dot(e.astype(BF16), vaug[par])
                pair.append(acc[:, :LANE] / (acc[:, LANE:] + jnp.exp2(sink - m)))
            outs.append(jnp.where(low, pair[0], pair[1]))
    o = jnp.concatenate(outs, axis=1) * _silu(g_ref[...].astype(F32))
    o_ref[...] = o.astype(o_ref.dtype)


def _swa_attention(g3, sinks, tswa, q_blk, k_blk, v_blk):
    b, s, _ = g3.shape
    wq = C_HEADS * C_HEAD_DIM
    wk = C_KV_HEADS * C_HEAD_DIM
    prev = lambda i: jnp.maximum(i - 1, 0)
    return pl.pallas_call(
        _swa_kernel,
        out_shape=jax.ShapeDtypeStruct((b, s, wq), BF16),
        grid=(b, s // QB),
        in_specs=[
            pl.BlockSpec(memory_space=pltpu.SMEM),
            pl.BlockSpec((None, QB, wq), lambda bb, i: (bb, i, q_blk)),
            pl.BlockSpec((None, QB, wk), lambda bb, i: (bb, prev(i), k_blk)),
            pl.BlockSpec((None, QB, wk), lambda bb, i: (bb, i, k_blk)),
            pl.BlockSpec((None, QB, wk), lambda bb, i: (bb, prev(i), v_blk)),
            pl.BlockSpec((None, QB, wk), lambda bb, i: (bb, i, v_blk)),
            pl.BlockSpec(tswa.shape, lambda bb, i: (0, 0, 0, 0)),
            pl.BlockSpec((None, QB, wq), lambda bb, i: (bb, i, 0)),
        ],
        out_specs=pl.BlockSpec((None, QB, wq), lambda bb, i: (bb, i, 0)),
        compiler_params=_cp("parallel", "parallel"),
        name="swa_attention",
    )(sinks.reshape(1, -1).astype(F32), g3, g3, g3, g3, g3, tswa, g3)


def _ordered_to_f32(key):
    bits = jnp.where(key >= 0, key, key ^ jnp.int32(0x7FFFFFFF))
    return lax.bitcast_convert_type(bits, F32)


def _dsa_kernel(q_ref, k_ref, v_ref, iq_ref, ik_ref, iw_ref, t_ref, g_ref, o_ref,
                sc, madd, iqs, ks, vaug, m_scr, acc_scr, *, s_len, k_top):
    i = pl.program_id(1)
    n_per = B_HEADS // B_KV_HEADS
    d = B_HEAD_DIM

    @pl.when(i == 0)
    def _():
        ones = jnp.ones((s_len, d), BF16)
        zeros = jnp.zeros((s_len, d), BF16)
        for g in range(B_KV_HEADS):
            kg = k_ref[:, g * d:(g + 1) * d]
            vg = v_ref[:, g * d:(g + 1) * d]
            ks[g, 0] = jnp.concatenate([kg, zeros], axis=1)
            ks[g, 1] = jnp.concatenate([zeros, kg], axis=1)
            vaug[g, 0] = jnp.concatenate([vg, ones], axis=1)
            vaug[g, 1] = jnp.concatenate([ones, vg], axis=1)

    half = lax.shift_right_logical(i, 1)
    odd = i & 1
    n_t = half + 1
    sub = 8
    key_i = lax.broadcasted_iota(jnp.int32, (KT, QB), 0)
    qry_i = i * QB + lax.broadcasted_iota(jnp.int32, (KT, QB), 1)

    for hp in range(IDX_HEADS // 2):
        for par in range(2):
            h = 2 * hp + par
            iqs[hp, par * QB:(par + 1) * QB, :] = iq_ref[:, h * IDX_DIM:(h + 1) * IDX_DIM]
    w = iw_ref[...] * (IDX_DIM ** -0.5 * IDX_HEADS ** -0.5)
    wrows = [w[h:h + 1, :] for h in range(IDX_HEADS)]

    def score_tile(kt, c):
        k0 = pl.multiple_of(kt * KT, KT)
        ikt = ik_ref[pl.ds(k0, KT), :]
        acc = jnp.zeros((KT, QB), F32)
        for hp in range(IDX_HEADS // 2):
            dots = _dot_t(ikt, iqs[hp])
            acc = acc + jnp.maximum(dots[:, :QB], 0.0) * wrows[2 * hp]
            acc = acc + jnp.maximum(dots[:, QB:], 0.0) * wrows[2 * hp + 1]
        sc[pl.ds(k0, KT), :] = jnp.where(qry_i >= k0 + key_i, acc, NEG_INF)
        return c

    lax.fori_loop(0, n_t, score_tile, 0)

    tail = (s_len - n_t * KT).astype(F32)
    int_min = jnp.int32(-2 ** 31)

    def bisect(b, prefix):
        cand = prefix + lax.shift_left(jnp.int32(1), 31 - b)
        cand_f = _ordered_to_f32(cand)

        def count_tile(kt, c):
            k0 = pl.multiple_of(kt * KT, KT)
            t = sc[pl.ds(k0, KT), :].reshape(4, KT // (4 * sub), sub, QB)
            part = jnp.sum(jnp.where(t >= cand_f[None, None], 1.0, 0.0), axis=1)
            return c + ((part[0] + part[1]) + (part[2] + part[3]))

        cnt = lax.fori_loop(0, n_t, count_tile, jnp.zeros((sub, QB), F32))
        total = jnp.sum(cnt, axis=0, keepdims=True) + jnp.where(cand_f <= NEG_INF, tail, 0.0)
        return jnp.where(total >= k_top, cand, prefix)

    thr = _ordered_to_f32(lax.fori_loop(0, 32, bisect, jnp.full((sub, QB), int_min, jnp.int32)))[:1]

    def mask_tile(kt, c):
        k0 = pl.multiple_of(kt * KT, KT)
        sel = (sc[pl.ds(k0, KT), :] >= thr) & (qry_i >= k0 + key_i)
        madd[:, pl.ds(k0, KT)] = jnp.where(sel, 0.0, NEG_INF).T
        return c

    lax.fori_loop(0, n_t, mask_tile, 0)

    near0 = jnp.where(odd == 1, half, jnp.maximum(half - 1, 0))
    m_scr[...] = jnp.full(m_scr.shape, -jnp.inf, F32)
    acc_scr[...] = jnp.zeros(acc_scr.shape, F32)

    def step(kt, near):
        k0 = pl.multiple_of(kt * KT, KT)
        mt = madd[:, pl.ds(k0, KT)]
        if near:
            li = jnp.where(odd == 1, 2, jnp.where(kt == half, 1, 0))
            ri = jnp.where(odd == 1, 1, jnp.where(kt == half, 0, 2))
        for g in range(B_KV_HEADS):
            kg = (ks[g, 0, pl.ds(k0, KT), :], ks[g, 1, pl.ds(k0, KT), :])
            vg = (vaug[g, 0, pl.ds(k0, KT), :], vaug[g, 1, pl.ds(k0, KT), :])
            for j in range(n_per):
                h = g * n_per + j
                q2 = q_ref[:, (h // 2) * LANE:(h // 2 + 1) * LANE]
                s = _dot_t(q2, kg[h % 2]) + mt
                if near:
                    rows = pl.ds(j * QB, QB)
                    s = s + jnp.concatenate([t_ref[li, g, rows, :], t_ref[ri, g, rows, :]], axis=1)
                m_prev = m_scr[h]
                m_new = jnp.maximum(m_prev, jnp.max(s, axis=1, keepdims=True))
                alpha = jnp.exp2(m_prev - m_new)
                p = jnp.exp2(s - _rep2(m_new))
                acc_scr[h] = alpha * acc_scr[h] + _dot(p.astype(BF16), vg[h % 2])
                m_scr[h] = m_new

    def far_body(kt, c):
        step(kt, False)
        return c

    def near_body(kt, c):
        step(kt, True)
        return c

    lax.fori_loop(0, near0, far_body, 0)
    lax.fori_loop(near0, n_t, near_body, 0)
    low = lax.broadcasted_iota(jnp.int32, (QB, LANE), 1) < d
    outs = []
    for hp in range(B_HEADS // 2):
        acc_e = acc_scr[2 * hp]
        acc_o = acc_scr[2 * hp + 1]
        num = jnp.where(low, acc_e, acc_o)
        den = pltpu.roll(jnp.where(low, acc_o, acc_e), d, 1)
        outs.append(num / den)
    o = jnp.concatenate(outs, axis=1) * _silu(g_ref[...].astype(F32))
    o_ref[...] = o.astype(o_ref.dtype)


def _dsa_attention(g3, ik3, iwt3, tdsa, q_blk, k_blk, v_blk, iq_blk):
    b, s, _ = g3.shape
    k_top = min(IDX_TOPK_MAX, s // 4)
    wq = B_HEADS * B_HEAD_DIM
    wk = B_KV_HEADS * B_HEAD_DIM
    wi = IDX_HEADS * IDX_DIM
    return pl.pallas_call(
        functools.partial(_dsa_kernel, s_len=s, k_top=k_top),
        out_shape=jax.ShapeDtypeStruct((b, s, wq), BF16),
        grid=(b, s // QB),
        in_specs=[
            pl.BlockSpec((None, QB, wq), lambda bb, i: (bb, i, q_blk)),
            pl.BlockSpec((None, s, wk), lambda bb, i: (bb, 0, k_blk)),
            pl.BlockSpec((None, s, wk), lambda bb, i: (bb, 0, v_blk)),
            pl.BlockSpec((None, QB, wi), lambda bb, i: (bb, i, iq_blk)),
            pl.BlockSpec((None, s, IDX_DIM), lambda bb, i: (bb, 0, 0)),
            pl.BlockSpec((None, IDX_HEADS, QB), lambda bb, i: (bb, 0, i)),
            pl.BlockSpec(tdsa.shape, lambda bb, i: (0, 0, 0, 0)),
            pl.BlockSpec((None, QB, wq), lambda bb, i: (bb, i, 0)),
        ],
        out_specs=pl.BlockSpec((None, QB, wq), lambda bb, i: (bb, i, 0)),
        scratch_shapes=[
            pltpu.VMEM((s, QB), F32),
            pltpu.VMEM((QB, s), F32),
            pltpu.VMEM((IDX_HEADS // 2, 2 * QB, IDX_DIM), BF16),
            pltpu.VMEM((B_KV_HEADS, 2, s, 2 * B_HEAD_DIM), BF16),
            pltpu.VMEM((B_KV_HEADS, 2, s, 2 * B_HEAD_DIM), BF16),
            pltpu.VMEM((B_HEADS, QB, LANE), F32),
            pltpu.VMEM((B_HEADS, QB, 2 * B_HEAD_DIM), F32),
        ],
        compiler_params=_cp("parallel", "arbitrary"),
        name="dsa_attention",
    )(g3, g3, g3, g3, ik3, iwt3, tdsa, g3)


def _mla_layer(h, w_in_t, q_norm, w_uq, kv_norm, w_ukv, rope1, rope2, b, s):
    o_ckv = A_Q_LORA + A_KV_LORA
    o_mq = o_ckv + A_ROPE
    o_gate = o_mq + MEM_WIDTH
    half = A_ROPE // 2
    w_f, w_g = _repack(w_in_t, (
        _pieces((0, o_ckv), (o_ckv, A_ROPE), (o_ckv + half, half), (o_ckv, half)),
        _pieces((o_gate, A_HEADS * A_V + MEM_WIDTH), (o_mq, MEM_WIDTH)),
    ), by_rows=True)
    cqkv = _matmul(h, w_f, F32, w_rows_are_outputs=True)
    g2 = _matmul(h, w_g, BF16, w_rows_are_outputs=True)

    hd = A_NOPE + A_ROPE
    (w_uq2,) = _repack(w_uq, (
        _pieces(*[(hh * hd, A_NOPE, hd ** -0.5 * LOG2E) for hh in range(A_HEADS)],
                *[(hh * hd + A_NOPE, A_ROPE, hd ** -0.5 * LOG2E) for hh in range(A_HEADS)]),
    ))
    qp = _uq_proj(cqkv, q_norm, w_uq2, rope2[0], rope2[1], s)
    kv, krp = _ukv_proj(cqkv, kv_norm, w_ukv, rope1[0], rope1[1], s)
    self_out = _mla_attention(qp.reshape(b, s, -1), kv.reshape(b, s, -1), krp.reshape(b, s, -1),
                              g2.reshape(b, s, -1), b, s)
    return self_out.reshape(b * s, -1), g2


def _dsa_layer(h, w_in_t, tdsa, b, s):
    wq = B_HEADS * B_HEAD_DIM
    wk = B_KV_HEADS * B_HEAD_DIM
    wi = IDX_HEADS * IDX_DIM
    o_k = wq
    o_v = o_k + wk
    o_iq = o_v + wk
    o_ik = o_iq + wi
    o_iw = o_ik + IDX_DIM
    o_mq = o_iw + IDX_HEADS
    o_gate = o_mq + MEM_WIDTH
    w_g, w_i = _repack(w_in_t, (
        _pieces((o_gate, wq + MEM_WIDTH), (o_mq, MEM_WIDTH), (0, wq, B_HEAD_DIM ** -0.5 * LOG2E),
                (o_iq, wi), (o_k, wk), (o_v, wk)),
        _pieces((o_ik, IDX_DIM + IDX_HEADS)),
    ), by_rows=True)
    g2 = _matmul(h, w_g, BF16, w_rows_are_outputs=True)
    ii = _matmul(h, w_i, F32, w_rows_are_outputs=True)
    ik3 = ii[:, :IDX_DIM].astype(BF16).reshape(b, s, IDX_DIM)
    iwt3 = ii[:, IDX_DIM:].reshape(b, s, IDX_HEADS).transpose(0, 2, 1)
    self_out = _dsa_attention(g2.reshape(b, s, -1), ik3, iwt3, tdsa,
                              q_blk=2, k_blk=7168 // wk, v_blk=7424 // wk, iq_blk=6)
    return self_out.reshape(b * s, -1), g2


def _swa_layer(h, w_in, sinks, tswa, b, s):
    wq = C_HEADS * C_HEAD_DIM
    wk = C_KV_HEADS * C_HEAD_DIM
    o_k = wq
    o_v = o_k + wk
    o_mq = o_v + wk
    o_gate = o_mq + MEM_WIDTH
    (w_g,) = _repack(w_in, (
        _pieces((o_gate, wq + MEM_WIDTH), (o_mq, MEM_WIDTH), (0, wq, C_HEAD_DIM ** -0.5 * LOG2E),
                (o_k, wk), (o_v, wk)),
    ))
    g2 = _matmul(h, w_g, BF16)
    self_out = _swa_attention(g2.reshape(b, s, -1), sinks, tswa, q_blk=2, k_blk=6144 // wk, v_blk=6400 // wk)
    return self_out.reshape(b * s, -1), g2


def kernel(x, mem, norm_in, final_norm, mem_norm, rel_bias, w_in_a, a_q_norm, w_uq, a_kv_norm, w_ukv,
           w_in_b, w_in_c, c_sinks, w_mem_kv, w_out):
    b, s, d = x.shape
    depth = norm_in.shape[0]
    ml = mem.shape[1]
    xf = x.reshape(b * s, d)
    mem_n = _rmsnorm(mem.reshape(b * ml, d), mem_norm, BF16)
    rope1, rope2 = _rope_tables(s)
    tswa, tdsa = _bias_tables(rel_bias)
    h = _rmsnorm(xf, norm_in[0], BF16)
    w_ukv_b, w_mem_b, w_out_b = w_ukv.astype(BF16), w_mem_kv.astype(BF16), w_out.astype(BF16)
    w_in_a_t, w_in_b_t = jnp.swapaxes(w_in_a, 1, 2), jnp.swapaxes(w_in_b, 1, 2)
    for i in range(depth):
        kind, j = i % N_MIXERS, i // N_MIXERS
        if kind == 0:
            ys, g2 = _mla_layer(h, (w_in_a_t, j), a_q_norm[j], (w_uq, j), a_kv_norm[j], (w_ukv_b, j),
                                rope1, rope2, b, s)
        elif kind == 1:
            ys, g2 = _dsa_layer(h, (w_in_b_t, j), tdsa, b, s)
        else:
            ys, g2 = _swa_layer(h, (w_in_c, j), c_sinks[j], tswa, b, s)
        memkv = _matmul(mem_n, (w_mem_b, i), BF16)
        ym = _mem_attention(g2.reshape(b, s, -1), 3, 2, memkv.reshape(b, ml, -1)).reshape(b * s, -1)
        last = i == depth - 1
        res = _out_proj(ys, ym, xf, (w_out_b, i), final_norm if last else norm_in[i + 1], not last)
        if last:
            return res[0].reshape(b, s, d)
        xf, h = res
```

```python
import functools
import math

import jax
import jax.numpy as jnp
from jax import lax
from jax.experimental import pallas as pl
from jax.experimental.pallas import tpu as pltpu

F32 = jnp.float32
BF16 = jnp.bfloat16

EPS = 1e-6
NEG_INF = -1e30
N_MIXERS = 3

N_BUCKETS = 32
MAX_DISTANCE = 128
N_BIAS_HEADS = 32

MEM_HEADS = 4
MEM_HEAD_DIM = 256
MEM_WIDTH = MEM_HEADS * MEM_HEAD_DIM

A_HEADS = 16
A_Q_LORA = 1536
A_KV_LORA = 512
A_NOPE = 128
A_ROPE = 64
A_V = 128
ROPE_THETA = 10000.0

B_HEADS = 32
B_KV_HEADS = 4
B_HEAD_DIM = 64
IDX_HEADS = 16
IDX_DIM = 64
IDX_TOPK_MAX = 256

C_HEADS = 32
C_KV_HEADS = 4
C_HEAD_DIM = 64
WINDOW = 128

LANE = 128
QB = 128
KT = 256
VMEM_LIMIT = 56 * 1024 * 1024
LOG2E = math.log2(math.e)
MXU_N = 256
MM_MAX_TN = 3328


def _cp(*sem):
    return pltpu.CompilerParams(dimension_semantics=sem, vmem_limit_bytes=VMEM_LIMIT)


def _tile(n, pref):
    t = min(n, pref)
    assert n % t == 0, (n, pref)
    return t


def _dot(a, b):
    return jnp.dot(a, b, preferred_element_type=F32)


def _dot_t(a, b):
    return lax.dot_general(a, b, (((1,), (1,)), ((), ())), preferred_element_type=F32)


def _rms_kernel(x_ref, g_ref, o_ref):
    x = x_ref[...].astype(F32)
    y = x * lax.rsqrt(jnp.mean(x * x, axis=-1, keepdims=True) + EPS)
    o_ref[...] = (y * g_ref[...]).astype(o_ref.dtype)


def _rmsnorm(x2d, g, out_dtype):
    m, d = x2d.shape
    tm = _tile(m, 512)
    return pl.pallas_call(
        _rms_kernel,
        out_shape=jax.ShapeDtypeStruct((m, d), out_dtype),
        grid=(m // tm,),
        in_specs=[pl.BlockSpec((tm, d), lambda i: (i, 0)), pl.BlockSpec((1, d), lambda i: (0, 0))],
        out_specs=pl.BlockSpec((tm, d), lambda i: (i, 0)),
        compiler_params=_cp("parallel"),
        name="rmsnorm",
    )(x2d, g.reshape(1, d).astype(F32))


def _mm_kernel(a_ref, w_ref, o_ref, *, w_rows_are_outputs):
    dot = _dot_t if w_rows_are_outputs else _dot
    o_ref[...] = dot(a_ref[...], w_ref[...]).astype(o_ref.dtype)


def _layer_spec(w, block, index_map):
    if isinstance(w, tuple):
        stack, layer = w
        return stack, pl.BlockSpec((None,) + block, lambda *g: (layer,) + index_map(*g))
    return w, pl.BlockSpec(block, index_map)


def _w_shape(w):
    return w[0].shape[1:] if isinstance(w, tuple) else w.shape


def _matmul(a, w, out_dtype, w_rows_are_outputs=False):
    m, k = a.shape
    n = _w_shape(w)[0 if w_rows_are_outputs else 1]
    tm = _tile(m, 1024)
    tn = max([t for t in range(MXU_N, min(n, MM_MAX_TN) + 1, MXU_N) if n % t == 0]
             or [t for t in range(LANE, min(n, MM_MAX_TN) + 1, LANE) if n % t == 0] or [n])
    if w_rows_are_outputs:
        w_arr, w_spec = _layer_spec(w, (tn, k), lambda i, j: (j, 0))
    else:
        w_arr, w_spec = _layer_spec(w, (k, tn), lambda i, j: (0, j))
    return pl.pallas_call(
        functools.partial(_mm_kernel, w_rows_are_outputs=w_rows_are_outputs),
        out_shape=jax.ShapeDtypeStruct((m, n), out_dtype),
        grid=(m // tm, n // tn),
        in_specs=[pl.BlockSpec((tm, k), lambda i, j: (i, 0)), w_spec],
        out_specs=pl.BlockSpec((tm, tn), lambda i, j: (i, j)),
        compiler_params=_cp("parallel", "arbitrary"),
        name="matmul",
    )(a, w_arr)


def _repack_kernel(w_ref, *o_refs, groups, by_rows):
    for o_ref, pieces in zip(o_refs, groups):
        for src, width, dst, scale in pieces:
            v = w_ref[src:src + width, :] if by_rows else w_ref[:, src:src + width]
            if scale != 1.0:
                v = v * scale
            if by_rows:
                o_ref[dst:dst + width, :] = v.astype(o_ref.dtype)
            else:
                o_ref[:, dst:dst + width] = v.astype(o_ref.dtype)


def _repack(w, groups, by_rows=False):
    shape = _w_shape(w)
    k = shape[1 if by_rows else 0]
    tk = _tile(k, 256)
    widths = [sum(p[1] for p in pieces) for pieces in groups]
    if by_rows:
        w_arr, w_spec = _layer_spec(w, (shape[0], tk), lambda i: (0, i))
        out_shape = tuple(jax.ShapeDtypeStruct((wd, k), BF16) for wd in widths)
        out_specs = tuple(pl.BlockSpec((wd, tk), lambda i: (0, i)) for wd in widths)
    else:
        w_arr, w_spec = _layer_spec(w, (tk, shape[1]), lambda i: (i, 0))
        out_shape = tuple(jax.ShapeDtypeStruct((k, wd), BF16) for wd in widths)
        out_specs = tuple(pl.BlockSpec((tk, wd), lambda i: (i, 0)) for wd in widths)
    return pl.pallas_call(
        functools.partial(_repack_kernel, groups=groups, by_rows=by_rows),
        out_shape=out_shape,
        grid=(k // tk,),
        in_specs=[w_spec],
        out_specs=out_specs,
        compiler_params=_cp("parallel"),
        name="weight_repack",
    )(w_arr)


def _pieces(*src_width_scale):
    out, dst = [], 0
    for p in src_width_scale:
        src, width = p[0], p[1]
        out.append((src, width, dst, p[2] if len(p) > 2 else 1.0))
        dst += width
    return tuple(out)


def _rope_tables(s):
    d = A_ROPE
    inv = 1.0 / (ROPE_THETA ** (jnp.arange(0, d, 2, dtype=F32) / d))
    ang = jnp.arange(s, dtype=F32)[:, None] * inv[None, :]
    cos, sin = jnp.cos(ang), jnp.sin(ang)
    z = jnp.zeros((s, LANE - d), F32)
    one = (jnp.concatenate([cos, cos, z], axis=1), jnp.concatenate([-sin, sin, z], axis=1))
    two = (jnp.concatenate([cos] * 4, axis=1), jnp.concatenate([-sin, sin] * 2, axis=1))
    return one, two


def _uq_kernel(cq_ref, g_ref, w_ref, cos_ref, sin_ref, o_ref, *, n_nope):
    x = cq_ref[...]
    y = x * lax.rsqrt(jnp.mean(x * x, axis=-1, keepdims=True) + EPS)
    h = (y * g_ref[...]).astype(BF16)
    n = w_ref.shape[1]
    cw = 512
    for c in range(0, n_nope, cw):
        o_ref[:, c:c + cw] = _dot(h, w_ref[:, c:c + cw]).astype(BF16)
    cos = cos_ref[...]
    sin = sin_ref[...]
    lane = lax.broadcasted_iota(jnp.int32, (x.shape[0], LANE), 1)
    first_half = (lane & (A_ROPE - 1)) < A_ROPE // 2
    low = lane < A_ROPE
    for c in range(n_nope, n, cw):
        y = _dot(h, w_ref[:, c:c + cw])
        parts = []
        for k in range(cw // LANE):
            slab = y[:, k * LANE:(k + 1) * LANE]
            partner = jnp.where(first_half, pltpu.roll(slab, LANE - A_ROPE // 2, 1), pltpu.roll(slab, A_ROPE // 2, 1))
            r = slab * cos + partner * sin
            parts.append(jnp.where(low, r, 0.0))
            parts.append(jnp.where(low, pltpu.roll(r, A_ROPE, 1), 0.0))
        d0 = n_nope + 2 * (c - n_nope)
        o_ref[:, d0:d0 + 2 * cw] = jnp.concatenate(parts, axis=1).astype(BF16)


def _uq_proj(cqkv, q_norm, w_uq2, cos_t, sin_t, s):
    m = cqkv.shape[0]
    n = w_uq2.shape[1]
    n_nope = A_HEADS * A_NOPE
    n_out = n_nope + 2 * (n - n_nope)
    tm = _tile(s, 512)
    nblk = s // tm
    return pl.pallas_call(
        functools.partial(_uq_kernel, n_nope=n_nope),
        out_shape=jax.ShapeDtypeStruct((m, n_out), BF16),
        grid=(m // tm,),
        in_specs=[
            pl.BlockSpec((tm, A_Q_LORA), lambda i: (i, 0)),
            pl.BlockSpec((1, A_Q_LORA), lambda i: (0, 0)),
            pl.BlockSpec((A_Q_LORA, n), lambda i: (0, 0)),
            pl.BlockSpec((tm, LANE), lambda i: (i % nblk, 0)),
            pl.BlockSpec((tm, LANE), lambda i: (i % nblk, 0)),
        ],
        out_specs=pl.BlockSpec((tm, n_out), lambda i: (i, 0)),
        compiler_params=_cp("parallel"),
        name="mla_q_up",
    )(cqkv, q_norm.reshape(1, -1).astype(F32), w_uq2, cos_t, sin_t)


def _ukv_kernel(ckv_ref, kr_ref, g_ref, w_ref, cos_ref, sin_ref, kv_ref, krp_ref):
    x = ckv_ref[...]
    y = x * lax.rsqrt(jnp.mean(x * x, axis=-1, keepdims=True) + EPS)
    h = (y * g_ref[...]).astype(BF16)
    n = w_ref.shape[1]
    cw = 512
    for c in range(0, n, cw):
        kv_ref[:, c:c + cw] = _dot(h, w_ref[:, c:c + cw]).astype(BF16)
    slab = kr_ref[...]
    krp_ref[...] = (slab * cos_ref[...] + pltpu.roll(slab, LANE // 2, 1) * sin_ref[...]).astype(BF16)


def _ukv_proj(cqkv, kv_norm, w_ukv, cos_t, sin_t, s):
    m = cqkv.shape[0]
    n = _w_shape(w_ukv)[1]
    tm = _tile(s, 512)
    nblk = s // tm
    cblk = A_Q_LORA // A_KV_LORA
    w_arr, w_spec = _layer_spec(w_ukv, (A_KV_LORA, n), lambda i: (0, 0))
    return pl.pallas_call(
        _ukv_kernel,
        out_shape=(jax.ShapeDtypeStruct((m, n), BF16), jax.ShapeDtypeStruct((m, LANE), BF16)),
        grid=(m // tm,),
        in_specs=[
            pl.BlockSpec((tm, A_KV_LORA), lambda i: (i, cblk)),
            pl.BlockSpec((tm, LANE), lambda i: (i, (A_Q_LORA + A_KV_LORA) // LANE)),
            pl.BlockSpec((1, A_KV_LORA), lambda i: (0, 0)),
            w_spec,
            pl.BlockSpec((tm, LANE), lambda i: (i % nblk, 0)),
            pl.BlockSpec((tm, LANE), lambda i: (i % nblk, 0)),
        ],
        out_specs=(pl.BlockSpec((tm, n), lambda i: (i, 0)), pl.BlockSpec((tm, LANE), lambda i: (i, 0))),
        compiler_params=_cp("parallel"),
        name="mla_kv_up",
    )(cqkv, cqkv, kv_norm.reshape(1, -1).astype(F32), w_arr, cos_t, sin_t)


def _rep2(x):
    return jnp.concatenate([x, x], axis=1)


def _silu(g):
    return g * (1.0 / (1.0 + jnp.exp(-g)))


def _mla_attn_kernel(qn_ref, qr_ref, kv_ref, kr_ref, g_ref, o_ref, kcat, vaug, m_scr, acc_scr, *, tq, hb):
    i = pl.program_id(2)

    @pl.when(i == 0)
    def _():
        ones = jnp.ones((kv_ref.shape[0], LANE), BF16)
        for hh in range(hb):
            kcat[hh, :, :LANE] = kv_ref[:, 2 * hh * LANE:(2 * hh + 1) * LANE]
            kcat[hh, :, LANE:] = kr_ref[...]
            vaug[hh, :, :LANE] = kv_ref[:, (2 * hh + 1) * LANE:(2 * hh + 2) * LANE]
            vaug[hh, :, LANE:] = ones

    m_scr[...] = jnp.full(m_scr.shape, -jnp.inf, F32)
    acc_scr[...] = jnp.zeros(acc_scr.shape, F32)
    qs = [jnp.concatenate([qn_ref[:, hh * LANE:(hh + 1) * LANE], qr_ref[:, hh * LANE:(hh + 1) * LANE]], axis=1)
          for hh in range(hb)]

    n_wide = lax.shift_right_logical(i, 1)

    def step(k0, tk, masked):
        for hh in range(hb):
            s = _dot_t(qs[hh], kcat[hh, pl.ds(k0, tk), :])
            if masked:
                row = lax.broadcasted_iota(jnp.int32, s.shape, 0)
                col = lax.broadcasted_iota(jnp.int32, s.shape, 1)
                s = jnp.where(row >= col, s, NEG_INF)
            m_prev = m_scr[hh]
            m_new = jnp.maximum(m_prev, jnp.max(s, axis=1, keepdims=True))
            alpha = jnp.exp2(m_prev - m_new)
            p = jnp.exp2(s - jnp.concatenate([m_new] * (tk // LANE), axis=1))
            acc_scr[hh] = _rep2(alpha) * acc_scr[hh] + _dot(p.astype(BF16), vaug[hh, pl.ds(k0, tk), :])
            m_scr[hh] = m_new

    def body(j, c):
        step(pl.multiple_of(j * 2 * tq, 2 * tq), 2 * tq, False)
        return c

    lax.fori_loop(0, n_wide, body, 0)

    @pl.when((i & 1) == 1)
    def _():
        step(pl.multiple_of((i - 1) * tq, tq), tq, False)

    step(pl.multiple_of(i * tq, tq), tq, True)
    for hh in range(hb):
        acc = acc_scr[hh]
        cols = slice(hh * LANE, (hh + 1) * LANE)
        gate = _silu(g_ref[:, cols].astype(F32))
        o_ref[:, cols] = (acc[:, :LANE] / acc[:, LANE:] * gate).astype(o_ref.dtype)


def _mla_attention(qp, kv, krp, g3, b, s):
    tq = _tile(s, 256)
    hn = A_HEADS
    hb = 8
    ng = hn // hb
    return pl.pallas_call(
        functools.partial(_mla_attn_kernel, tq=tq, hb=hb),
        out_shape=jax.ShapeDtypeStruct((b, s, hn * A_V), BF16),
        grid=(b, ng, s // tq),
        in_specs=[
            pl.BlockSpec((None, tq, hb * LANE), lambda bb, h, i: (bb, i, h)),
            pl.BlockSpec((None, tq, hb * LANE), lambda bb, h, i: (bb, i, ng + h)),
            pl.BlockSpec((None, s, 2 * hb * LANE), lambda bb, h, i: (bb, 0, h)),
            pl.BlockSpec((None, s, LANE), lambda bb, h, i: (bb, 0, 0)),
            pl.BlockSpec((None, tq, hb * LANE), lambda bb, h, i: (bb, i, h)),
        ],
        out_specs=pl.BlockSpec((None, tq, hb * LANE), lambda bb, h, i: (bb, i, h)),
        scratch_shapes=[
            pltpu.VMEM((hb, s, 2 * LANE), BF16),
            pltpu.VMEM((hb, s, 2 * LANE), BF16),
            pltpu.VMEM((hb, tq, LANE), F32),
            pltpu.VMEM((hb, tq, 2 * LANE), F32),
        ],
        compiler_params=_cp("parallel", "parallel", "arbitrary"),
        name="mla_attention",
    )(qp, qp, kv, krp, g3)


def _mem_attn_kernel(q_ref, kv_ref, g_ref, o_ref):
    d = MEM_HEAD_DIM
    for h in range(MEM_HEADS):
        cols = slice(h * d, (h + 1) * d)
        k = kv_ref[:, cols]
        v = kv_ref[:, MEM_WIDTH + h * d:MEM_WIDTH + (h + 1) * d]
        s = _dot_t(q_ref[:, cols], k) * (d ** -0.5)
        m = jnp.max(s, axis=1, keepdims=True)
        e = jnp.exp(s - m)
        p = e / jnp.sum(e, axis=1, keepdims=True)
        o = _dot(p.astype(BF16), v) * _silu(g_ref[:, cols].astype(F32))
        o_ref[:, cols] = o.astype(o_ref.dtype)


def _mem_attention(g3, mq_blk, gate_blk, memkv3):
    b, s, _ = g3.shape
    ml = memkv3.shape[1]
    tq = _tile(s, 512)
    return pl.pallas_call(
        _mem_attn_kernel,
        out_shape=jax.ShapeDtypeStruct((b, s, MEM_WIDTH), BF16),
        grid=(b, s // tq),
        in_specs=[
            pl.BlockSpec((None, tq, MEM_WIDTH), lambda bb, i: (bb, i, mq_blk)),
            pl.BlockSpec((None, ml, 2 * MEM_WIDTH), lambda bb, i: (bb, 0, 0)),
            pl.BlockSpec((None, tq, MEM_WIDTH), lambda bb, i: (bb, i, gate_blk)),
        ],
        out_specs=pl.BlockSpec((None, tq, MEM_WIDTH), lambda bb, i: (bb, i, 0)),
        compiler_params=_cp("parallel", "parallel"),
        name="mem_attention",
    )(g3, memkv3, g3)


def _out_kernel(ys_ref, ym_ref, x_ref, w_ref, g_ref, *refs, keep_x):
    if keep_x:
        xo_ref, h_ref, xrow = refs
    else:
        h_ref, xrow = refs
    j = pl.program_id(1)
    ws = ys_ref.shape[1]
    tn = x_ref.shape[1]
    xn = x_ref[...] + _dot(ys_ref[...], w_ref[:ws, :]) + _dot(ym_ref[...], w_ref[ws:, :])
    xrow[:, pl.ds(pl.multiple_of(j * tn, tn), tn)] = xn
    if keep_x:
        xo_ref[...] = xn

    @pl.when(j == pl.num_programs(1) - 1)
    def _():
        x = xrow[...]
        y = x * lax.rsqrt(jnp.mean(x * x, axis=-1, keepdims=True) + EPS)
        h_ref[...] = (y * g_ref[...]).astype(h_ref.dtype)


def _out_proj(ys, ym, x2d, w_out, g_next, keep_x):
    m, ws = ys.shape
    wm = ym.shape[1]
    kk, n = _w_shape(w_out)
    tm = _tile(m, 1024)
    tn = _tile(n, 512)
    w_arr, w_spec = _layer_spec(w_out, (kk, tn), lambda i, j: (0, j))
    h_dtype = BF16 if keep_x else F32
    row_spec = pl.BlockSpec((tm, n), lambda i, j: (i, 0))
    tile_spec = pl.BlockSpec((tm, tn), lambda i, j: (i, j))
    out_shape = [jax.ShapeDtypeStruct((m, n), h_dtype)]
    out_specs = [row_spec]
    if keep_x:
        out_shape.insert(0, jax.ShapeDtypeStruct((m, n), F32))
        out_specs.insert(0, tile_spec)
    return pl.pallas_call(
        functools.partial(_out_kernel, keep_x=keep_x),
        out_shape=tuple(out_shape),
        grid=(m // tm, n // tn),
        in_specs=[
            pl.BlockSpec((tm, ws), lambda i, j: (i, 0)),
            pl.BlockSpec((tm, wm), lambda i, j: (i, 0)),
            tile_spec,
            w_spec,
            pl.BlockSpec((1, n), lambda i, j: (0, 0)),
        ],
        out_specs=tuple(out_specs),
        scratch_shapes=[pltpu.VMEM((tm, n), F32)],
        compiler_params=_cp("parallel", "arbitrary"),
        name="out_proj_norm",
    )(ys, ym, x2d, w_arr, g_next.reshape(1, n).astype(F32))


def _t5_bucket(rel):
    n = jnp.maximum(rel, 0)
    max_exact = N_BUCKETS // 2
    nf = jnp.maximum(n, 1).astype(F32)
    large = max_exact + (jnp.log(nf / max_exact) / math.log(MAX_DISTANCE / max_exact)
                         * (N_BUCKETS - max_exact)).astype(jnp.int32)
    large = jnp.minimum(large, N_BUCKETS - 1)
    return jnp.where(n < max_exact, n, large)


def _bias_kernel(rb_ref, bd_ref, bp_ref, tswa_ref, tdsa_ref):
    bd = bd_ref[...]
    bp = bp_ref[...]
    for h in range(N_BIAS_HEADS):
        g, j = divmod(h, N_BIAS_HEADS // B_KV_HEADS)

        def body(b, c, h=h):
            d, p = c
            v = rb_ref[b, h]
            return jnp.where(bd == b, v, d), jnp.where(bp == b, v, p)

        z = jnp.zeros((QB, QB), F32)
        d, p = lax.fori_loop(0, N_BUCKETS, body, (z, z))
        far = rb_ref[N_BUCKETS - 1, h]
        rows = pl.ds(j * QB, QB)
        t = lax.broadcasted_iota(jnp.int32, (QB, QB), 0)
        c = lax.broadcasted_iota(jnp.int32, (QB, QB), 1)
        tswa_ref[0, g, rows, :] = jnp.where(t >= c, d * LOG2E, NEG_INF)
        tswa_ref[1, g, rows, :] = jnp.where(t + QB - c < WINDOW, p * LOG2E, NEG_INF)
        tdsa_ref[0, g, rows, :] = z
        tdsa_ref[1, g, rows, :] = (d - far) * LOG2E
        tdsa_ref[2, g, rows, :] = (p - far) * LOG2E


def _bias_tables(rel_bias):
    t = jnp.arange(QB)[:, None]
    c = jnp.arange(QB)[None, :]
    bd = _t5_bucket(t - c).astype(jnp.int32)
    bp = _t5_bucket(t + QB - c).astype(jnp.int32)
    rows = (N_BIAS_HEADS // B_KV_HEADS) * QB
    return pl.pallas_call(
        _bias_kernel,
        out_shape=(jax.ShapeDtypeStruct((2, B_KV_HEADS, rows, QB), F32),
                   jax.ShapeDtypeStruct((3, B_KV_HEADS, rows, QB), F32)),
        in_specs=[pl.BlockSpec(memory_space=pltpu.SMEM),
                  pl.BlockSpec(memory_space=pltpu.VMEM),
                  pl.BlockSpec(memory_space=pltpu.VMEM)],
        out_specs=(pl.BlockSpec(memory_space=pltpu.VMEM), pl.BlockSpec(memory_space=pltpu.VMEM)),
        compiler_params=pltpu.CompilerParams(vmem_limit_bytes=VMEM_LIMIT),
        name="t5_bias_tables",
    )(rel_bias.astype(F32), bd, bp)


def _swa_kernel(sink_ref, q_ref, kp_ref, kc_ref, vp_ref, vc_ref, t_ref, g_ref, o_ref):
    i = pl.program_id(1)
    n_per = C_HEADS // C_KV_HEADS
    d = C_HEAD_DIM
    kb = jnp.concatenate([kp_ref[...], kc_ref[...]], axis=0)
    vb = jnp.concatenate([vp_ref[...], vc_ref[...]], axis=0)
    ones = jnp.ones((2 * QB, d), BF16)
    no_prev = jnp.where(i == 0, NEG_INF, 0.0).astype(F32)
    low = lax.broadcasted_iota(jnp.int32, (QB, LANE), 1) < d
    outs = []
    for g in range(C_KV_HEADS):
        kg = kb[:, g * d:(g + 1) * d]
        vg = vb[:, g * d:(g + 1) * d]
        vaug = (jnp.concatenate([vg, ones, ones, vg], axis=1), jnp.concatenate([ones, vg, vg, ones], axis=1))
        zk = jnp.zeros_like(kg)
        kpar = (jnp.concatenate([kg, zk], axis=1), jnp.concatenate([zk, kg], axis=1))
        for jp in range(n_per // 2):
            pair = []
            q2 = q_ref[:, (g * n_per + 2 * jp) * d:(g * n_per + 2 * jp + 2) * d]
            for par in range(2):
                j = 2 * jp + par
                h = g * n_per + j
                rows = pl.ds(j * QB, QB)
                s = _dot_t(q2, kpar[par])
                s = s + jnp.concatenate([t_ref[1, g, rows, :] + no_prev, t_ref[0, g, rows, :]], axis=1)
                sink = sink_ref[0, h] * LOG2E
                m = jnp.maximum(jnp.broadcast_to(jnp.max(s, axis=1, keepdims=True), (QB, LANE)), sink)
                e = jnp.exp2(s - _rep2(m))
                acc = _dot(e.astype(BF16), vaug[par])
                pair.append(acc[:, :LANE] / (acc[:, LANE:] + jnp.exp2(sink - m)))
            outs.append(jnp.where(low, pair[0], pair[1]))
    o = jnp.concatenate(outs, axis=1) * _silu(g_ref[...].astype(F32))
    o_ref[...] = o.astype(o_ref.dtype)


def _swa_attention(g3, sinks, tswa, q_blk, k_blk, v_blk):
    b, s, _ = g3.shape
    wq = C_HEADS * C_HEAD_DIM
    wk = C_KV_HEADS * C_HEAD_DIM
    prev = lambda i: jnp.maximum(i - 1, 0)
    return pl.pallas_call(
        _swa_kernel,
        out_shape=jax.ShapeDtypeStruct((b, s, wq), BF16),
        grid=(b, s // QB),
        in_specs=[
            pl.BlockSpec(memory_space=pltpu.SMEM),
            pl.BlockSpec((None, QB, wq), lambda bb, i: (bb, i, q_blk)),
            pl.BlockSpec((None, QB, wk), lambda bb, i: (bb, prev(i), k_blk)),
            pl.BlockSpec((None, QB, wk), lambda bb, i: (bb, i, k_blk)),
            pl.BlockSpec((None, QB, wk), lambda bb, i: (bb, prev(i), v_blk)),
            pl.BlockSpec((None, QB, wk), lambda bb, i: (bb, i, v_blk)),
            pl.BlockSpec(tswa.shape, lambda bb, i: (0, 0, 0, 0)),
            pl.BlockSpec((None, QB, wq), lambda bb, i: (bb, i, 0)),
        ],
        out_specs=pl.BlockSpec((None, QB, wq), lambda bb, i: (bb, i, 0)),
        compiler_params=_cp("parallel", "parallel"),
        name="swa_attention",
    )(sinks.reshape(1, -1).astype(F32), g3, g3, g3, g3, g3, tswa, g3)


def _ordered_to_f32(key):
    bits = jnp.where(key >= 0, key, key ^ jnp.int32(0x7FFFFFFF))
    return lax.bitcast_convert_type(bits, F32)


def _dsa_kernel(q_ref, k_ref, v_ref, iq_ref, ik_ref, iw_ref, t_ref, g_ref, o_ref,
                sc, madd, iqs, ks, vaug, m_scr, acc_scr, *, s_len, k_top):
    i = pl.program_id(1)
    n_per = B_HEADS // B_KV_HEADS
    d = B_HEAD_DIM

    @pl.when(i == 0)
    def _():
        ones = jnp.ones((s_len, d), BF16)
        zeros = jnp.zeros((s_len, d), BF16)
        for g in range(B_KV_HEADS):
            kg = k_ref[:, g * d:(g + 1) * d]
            vg = v_ref[:, g * d:(g + 1) * d]
            ks[g, 0] = jnp.concatenate([kg, zeros], axis=1)
            ks[g, 1] = jnp.concatenate([zeros, kg], axis=1)
            vaug[g, 0] = jnp.concatenate([vg, ones], axis=1)
            vaug[g, 1] = jnp.concatenate([ones, vg], axis=1)

    half = lax.shift_right_logical(i, 1)
    odd = i & 1
    n_t = half + 1
    sub = 8
    key_i = lax.broadcasted_iota(jnp.int32, (KT, QB), 0)
    qry_i = i * QB + lax.broadcasted_iota(jnp.int32, (KT, QB), 1)

    for hp in range(IDX_HEADS // 2):
        for par in range(2):
            h = 2 * hp + par
            iqs[hp, par * QB:(par + 1) * QB, :] = iq_ref[:, h * IDX_DIM:(h + 1) * IDX_DIM]
    w = iw_ref[...] * (IDX_DIM ** -0.5 * IDX_HEADS ** -0.5)
    wrows = [w[h:h + 1, :] for h in range(IDX_HEADS)]

    def score_tile(kt, c):
        k0 = pl.multiple_of(kt * KT, KT)
        ikt = ik_ref[pl.ds(k0, KT), :]
        acc = jnp.zeros((KT, QB), F32)
        for hp in range(IDX_HEADS // 2):
            dots = _dot_t(ikt, iqs[hp])
            acc = acc + jnp.maximum(dots[:, :QB], 0.0) * wrows[2 * hp]
            acc = acc + jnp.maximum(dots[:, QB:], 0.0) * wrows[2 * hp + 1]
        sc[pl.ds(k0, KT), :] = jnp.where(qry_i >= k0 + key_i, acc, NEG_INF)
        return c

    lax.fori_loop(0, n_t, score_tile, 0)

    tail = (s_len - n_t * KT).astype(F32)
    int_min = jnp.int32(-2 ** 31)

    def bisect(b, prefix):
        cand = prefix + lax.shift_left(jnp.int32(1), 31 - b)
        cand_f = _ordered_to_f32(cand)

        def count_tile(kt, c):
            k0 = pl.multiple_of(kt * KT, KT)
            t = sc[pl.ds(k0, KT), :].reshape(4, KT // (4 * sub), sub, QB)
            part = jnp.sum(jnp.where(t >= cand_f[None, None], 1.0, 0.0), axis=1)
            return c + ((part[0] + part[1]) + (part[2] + part[3]))

        cnt = lax.fori_loop(0, n_t, count_tile, jnp.zeros((sub, QB), F32))
        total = jnp.sum(cnt, axis=0, keepdims=True) + jnp.where(cand_f <= NEG_INF, tail, 0.0)
        return jnp.where(total >= k_top, cand, prefix)

    thr = _ordered_to_f32(lax.fori_loop(0, 32, bisect, jnp.full((sub, QB), int_min, jnp.int32)))[:1]

    def mask_tile(kt, c):
        k0 = pl.multiple_of(kt * KT, KT)
        sel = (sc[pl.ds(k0, KT), :] >= thr) & (qry_i >= k0 + key_i)
        madd[:, pl.ds(k0, KT)] = jnp.where(sel, 0.0, NEG_INF).T
        return c

    lax.fori_loop(0, n_t, mask_tile, 0)

    near0 = jnp.where(odd == 1, half, jnp.maximum(half - 1, 0))
    m_scr[...] = jnp.full(m_scr.shape, -jnp.inf, F32)
    acc_scr[...] = jnp.zeros(acc_scr.shape, F32)

    def step(kt, near):
        k0 = pl.multiple_of(kt * KT, KT)
        mt = madd[:, pl.ds(k0, KT)]
        if near:
            li = jnp.where(odd == 1, 2, jnp.where(kt == half, 1, 0))
            ri = jnp.where(odd == 1, 1, jnp.where(kt == half, 0, 2))
        for g in range(B_KV_HEADS):
            kg = (ks[g, 0, pl.ds(k0, KT), :], ks[g, 1, pl.ds(k0, KT), :])
            vg = (vaug[g, 0, pl.ds(k0, KT), :], vaug[g, 1, pl.ds(k0, KT), :])
            for j in range(n_per):
                h = g * n_per + j
                q2 = q_ref[:, (h // 2) * LANE:(h // 2 + 1) * LANE]
                s = _dot_t(q2, kg[h % 2]) + mt
                if near:
                    rows = pl.ds(j * QB, QB)
                    s = s + jnp.concatenate([t_ref[li, g, rows, :], t_ref[ri, g, rows, :]], axis=1)
                m_prev = m_scr[h]
                m_new = jnp.maximum(m_prev, jnp.max(s, axis=1, keepdims=True))
                alpha = jnp.exp2(m_prev - m_new)
                p = jnp.exp2(s - _rep2(m_new))
                acc_scr[h] = alpha * acc_scr[h] + _dot(p.astype(BF16), vg[h % 2])
                m_scr[h] = m_new

    def far_body(kt, c):
        step(kt, False)
        return c

    def near_body(kt, c):
        step(kt, True)
        return c

    lax.fori_loop(0, near0, far_body, 0)
    lax.fori_loop(near0, n_t, near_body, 0)
    low = lax.broadcasted_iota(jnp.int32, (QB, LANE), 1) < d
    outs = []
    for hp in range(B_HEADS // 2):
        acc_e = acc_scr[2 * hp]
        acc_o = acc_scr[2 * hp + 1]
        num = jnp.where(low, acc_e, acc_o)
        den = pltpu.roll(jnp.where(low, acc_o, acc_e), d, 1)
        outs.append(num / den)
    o = jnp.concatenate(outs, axis=1) * _silu(g_ref[...].astype(F32))
    o_ref[...] = o.astype(o_ref.dtype)


def _dsa_attention(g3, ik3, iwt3, tdsa, q_blk, k_blk, v_blk, iq_blk):
    b, s, _ = g3.shape
    k_top = min(IDX_TOPK_MAX, s // 4)
    wq = B_HEADS * B_HEAD_DIM
    wk = B_KV_HEADS * B_HEAD_DIM
    wi = IDX_HEADS * IDX_DIM
    return pl.pallas_call(
        functools.partial(_dsa_kernel, s_len=s, k_top=k_top),
        out_shape=jax.ShapeDtypeStruct((b, s, wq), BF16),
        grid=(b, s // QB),
        in_specs=[
            pl.BlockSpec((None, QB, wq), lambda bb, i: (bb, i, q_blk)),
            pl.BlockSpec((None, s, wk), lambda bb, i: (bb, 0, k_blk)),
            pl.BlockSpec((None, s, wk), lambda bb, i: (bb, 0, v_blk)),
            pl.BlockSpec((None, QB, wi), lambda bb, i: (bb, i, iq_blk)),
            pl.BlockSpec((None, s, IDX_DIM), lambda bb, i: (bb, 0, 0)),
            pl.BlockSpec((None, IDX_HEADS, QB), lambda bb, i: (bb, 0, i)),
            pl.BlockSpec(tdsa.shape, lambda bb, i: (0, 0, 0, 0)),
            pl.BlockSpec((None, QB, wq), lambda bb, i: (bb, i, 0)),
        ],
        out_specs=pl.BlockSpec((None, QB, wq), lambda bb, i: (bb, i, 0)),
        scratch_shapes=[
            pltpu.VMEM((s, QB), F32),
            pltpu.VMEM((QB, s), F32),
            pltpu.VMEM((IDX_HEADS // 2, 2 * QB, IDX_DIM), BF16),
            pltpu.VMEM((B_KV_HEADS, 2, s, 2 * B_HEAD_DIM), BF16),
            pltpu.VMEM((B_KV_HEADS, 2, s, 2 * B_HEAD_DIM), BF16),
            pltpu.VMEM((B_HEADS, QB, LANE), F32),
            pltpu.VMEM((B_HEADS, QB, 2 * B_HEAD_DIM), F32),
        ],
        compiler_params=_cp("parallel", "arbitrary"),
        name="dsa_attention",
    )(g3, g3, g3, g3, ik3, iwt3, tdsa, g3)


def _mla_layer(h, w_in_t, q_norm, w_uq, kv_norm, w_ukv, rope1, rope2, b, s):
    o_ckv = A_Q_LORA + A_KV_LORA
    o_mq = o_ckv + A_ROPE
    o_gate = o_mq + MEM_WIDTH
    half = A_ROPE // 2
    w_f, w_g = _repack(w_in_t, (
        _pieces((0, o_ckv), (o_ckv, A_ROPE), (o_ckv + half, half), (o_ckv, half)),
        _pieces((o_gate, A_HEADS * A_V + MEM_WIDTH), (o_mq, MEM_WIDTH)),
    ), by_rows=True)
    cqkv = _matmul(h, w_f, F32, w_rows_are_outputs=True)
    g2 = _matmul(h, w_g, BF16, w_rows_are_outputs=True)

    hd = A_NOPE + A_ROPE
    (w_uq2,) = _repack(w_uq, (
        _pieces(*[(hh * hd, A_NOPE, hd ** -0.5 * LOG2E) for hh in range(A_HEADS)],
                *[(hh * hd + A_NOPE, A_ROPE, hd ** -0.5 * LOG2E) for hh in range(A_HEADS)]),
    ))
    qp = _uq_proj(cqkv, q_norm, w_uq2, rope2[0], rope2[1], s)
    kv, krp = _ukv_proj(cqkv, kv_norm, w_ukv, rope1[0], rope1[1], s)
    self_out = _mla_attention(qp.reshape(b, s, -1), kv.reshape(b, s, -1), krp.reshape(b, s, -1),
                              g2.reshape(b, s, -1), b, s)
    return self_out.reshape(b * s, -1), g2


def _dsa_layer(h, w_in_t, tdsa, b, s):
    wq = B_HEADS * B_HEAD_DIM
    wk = B_KV_HEADS * B_HEAD_DIM
    wi = IDX_HEADS * IDX_DIM
    o_k = wq
    o_v = o_k + wk
    o_iq = o_v + wk
    o_ik = o_iq + wi
    o_iw = o_ik + IDX_DIM
    o_mq = o_iw + IDX_HEADS
    o_gate = o_mq + MEM_WIDTH
    w_g, w_i = _repack(w_in_t, (
        _pieces((o_gate, wq + MEM_WIDTH), (o_mq, MEM_WIDTH), (0, wq, B_HEAD_DIM ** -0.5 * LOG2E),
                (o_iq, wi), (o_k, wk), (o_v, wk)),
        _pieces((o_ik, IDX_DIM + IDX_HEADS)),
    ), by_rows=True)
    g2 = _matmul(h, w_g, BF16, w_rows_are_outputs=True)
    ii = _matmul(h, w_i, F32, w_rows_are_outputs=True)
    ik3 = ii[:, :IDX_DIM].astype(BF16).reshape(b, s, IDX_DIM)
    iwt3 = ii[:, IDX_DIM:].reshape(b, s, IDX_HEADS).transpose(0, 2, 1)
    self_out = _dsa_attention(g2.reshape(b, s, -1), ik3, iwt3, tdsa,
                              q_blk=2, k_blk=7168 // wk, v_blk=7424 // wk, iq_blk=6)
    return self_out.reshape(b * s, -1), g2


def _swa_layer(h, w_in, sinks, tswa, b, s):
    wq = C_HEADS * C_HEAD_DIM
    wk = C_KV_HEADS * C_HEAD_DIM
    o_k = wq
    o_v = o_k + wk
    o_mq = o_v + wk
    o_gate = o_mq + MEM_WIDTH
    (w_g,) = _repack(w_in, (
        _pieces((o_gate, wq + MEM_WIDTH), (o_mq, MEM_WIDTH), (0, wq, C_HEAD_DIM ** -0.5 * LOG2E),
                (o_k, wk), (o_v, wk)),
    ))
    g2 = _matmul(h, w_g, BF16)
    self_out = _swa_attention(g2.reshape(b, s, -1), sinks, tswa, q_blk=2, k_blk=6144 // wk, v_blk=6400 // wk)
    return self_out.reshape(b * s, -1), g2


def kernel(x, mem, norm_in, final_norm, mem_norm, rel_bias, w_in_a, a_q_norm, w_uq, a_kv_norm, w_ukv,
           w_in_b, w_in_c, c_sinks, w_mem_kv, w_out):
    b, s, d = x.shape
    depth = norm_in.shape[0]
    ml = mem.shape[1]
    xf = x.reshape(b * s, d)
    mem_n = _rmsnorm(mem.reshape(b * ml, d), mem_norm, BF16)
    rope1, rope2 = _rope_tables(s)
    tswa, tdsa = _bias_tables(rel_bias)
    h = _rmsnorm(xf, norm_in[0], BF16)
    w_ukv_b, w_mem_b, w_out_b = w_ukv.astype(BF16), w_mem_kv.astype(BF16), w_out.astype(BF16)
    w_in_a_t, w_in_b_t = jnp.swapaxes(w_in_a, 1, 2), jnp.swapaxes(w_in_b, 1, 2)
    for i in range(depth):
        kind, j = i % N_MIXERS, i // N_MIXERS
        if kind == 0:
            ys, g2 = _mla_layer(h, (w_in_a_t, j), a_q_norm[j], (w_uq, j), a_kv_norm[j], (w_ukv_b, j),
                                rope1, rope2, b, s)
        elif kind == 1:
            ys, g2 = _dsa_layer(h, (w_in_b_t, j), tdsa, b, s)
        else:
            ys, g2 = _swa_layer(h, (w_in_c, j), c_sinks[j], tswa, b, s)
        memkv = _matmul(mem_n, (w_mem_b, i), BF16)
        ym = _mem_attention(g2.reshape(b, s, -1), 3, 2, memkv.reshape(b, ml, -1)).reshape(b * s, -1)
        last = i == depth - 1
        res = _out_proj(ys, ym, xf, (w_out_b, i), final_norm if last else norm_in[i + 1], not last)
        if last:
            return res[0].reshape(b, s, d)
        xf, h = res
```

```python
import functools
import math

import jax
import jax.numpy as jnp
from jax import lax
from jax.experimental import pallas as pl
from jax.experimental.pallas import tpu as pltpu

F32 = jnp.float32
BF16 = jnp.bfloat16

EPS = 1e-6
NEG_INF = -1e30
N_MIXERS = 3

N_BUCKETS = 32
MAX_DISTANCE = 128
N_BIAS_HEADS = 32

MEM_HEADS = 4
MEM_HEAD_DIM = 256
MEM_WIDTH = MEM_HEADS * MEM_HEAD_DIM

A_HEADS = 16
A_Q_LORA = 1536
A_KV_LORA = 512
A_NOPE = 128
A_ROPE = 64
A_V = 128
ROPE_THETA = 10000.0

B_HEADS = 32
B_KV_HEADS = 4
B_HEAD_DIM = 64
IDX_HEADS = 16
IDX_DIM = 64
IDX_TOPK_MAX = 256

C_HEADS = 32
C_KV_HEADS = 4
C_HEAD_DIM = 64
WINDOW = 128

LANE = 128
MXU_N = 256
QB = 128
KT = MXU_N
VMEM_LIMIT = 56 * 1024 * 1024
LOG2E = math.log2(math.e)
MM_MAX_TN = 13 * MXU_N


def _cp(*sem):
    return pltpu.CompilerParams(dimension_semantics=sem, vmem_limit_bytes=VMEM_LIMIT)


def _tile(n, pref):
    t = min(n, pref)
    assert n % t == 0, (n, pref)
    return t


def _dot(a, b):
    return jnp.dot(a, b, preferred_element_type=F32)


def _dot_t(a, b):
    return lax.dot_general(a, b, (((1,), (1,)), ((), ())), preferred_element_type=F32)


def _rms_kernel(x_ref, g_ref, o_ref):
    x = x_ref[...].astype(F32)
    y = x * lax.rsqrt(jnp.mean(x * x, axis=-1, keepdims=True) + EPS)
    o_ref[...] = (y * g_ref[...]).astype(o_ref.dtype)


def _rmsnorm(x2d, g, out_dtype):
    m, d = x2d.shape
    tm = _tile(m, 512)
    return pl.pallas_call(
        _rms_kernel,
        out_shape=jax.ShapeDtypeStruct((m, d), out_dtype),
        grid=(m // tm,),
        in_specs=[pl.BlockSpec((tm, d), lambda i: (i, 0)), pl.BlockSpec((1, d), lambda i: (0, 0))],
        out_specs=pl.BlockSpec((tm, d), lambda i: (i, 0)),
        compiler_params=_cp("parallel"),
        name="rmsnorm",
    )(x2d, g.reshape(1, d).astype(F32))


def _mm_kernel(a_ref, w_ref, o_ref, *, w_rows_are_outputs):
    dot = _dot_t if w_rows_are_outputs else _dot
    o_ref[...] = dot(a_ref[...], w_ref[...]).astype(o_ref.dtype)


def _layer_spec(w, block, index_map):
    if isinstance(w, tuple):
        stack, layer = w
        return stack, pl.BlockSpec((None,) + block, lambda *g: (layer,) + index_map(*g))
    return w, pl.BlockSpec(block, index_map)


def _w_shape(w):
    return w[0].shape[1:] if isinstance(w, tuple) else w.shape


def _matmul(a, w, out_dtype, w_rows_are_outputs=False):
    m, k = a.shape
    n = _w_shape(w)[0 if w_rows_are_outputs else 1]
    tm = _tile(m, 1024)
    tn = max([t for t in range(MXU_N, min(n, MM_MAX_TN) + 1, MXU_N) if n % t == 0]
             or [t for t in range(LANE, min(n, MM_MAX_TN) + 1, LANE) if n % t == 0] or [n])
    if w_rows_are_outputs:
        w_arr, w_spec = _layer_spec(w, (tn, k), lambda i, j: (j, 0))
    else:
        w_arr, w_spec = _layer_spec(w, (k, tn), lambda i, j: (0, j))
    return pl.pallas_call(
        functools.partial(_mm_kernel, w_rows_are_outputs=w_rows_are_outputs),
        out_shape=jax.ShapeDtypeStruct((m, n), out_dtype),
        grid=(m // tm, n // tn),
        in_specs=[pl.BlockSpec((tm, k), lambda i, j: (i, 0)), w_spec],
        out_specs=pl.BlockSpec((tm, tn), lambda i, j: (i, j)),
        compiler_params=_cp("parallel", "arbitrary"),
        name="matmul",
    )(a, w_arr)


def _repack_kernel(w_ref, *o_refs, groups, by_rows):
    for o_ref, pieces in zip(o_refs, groups):
        for src, width, dst, scale in pieces:
            v = w_ref[src:src + width, :] if by_rows else w_ref[:, src:src + width]
            if scale != 1.0:
                v = v * scale
            if by_rows:
                o_ref[dst:dst + width, :] = v.astype(o_ref.dtype)
            else:
                o_ref[:, dst:dst + width] = v.astype(o_ref.dtype)


def _repack(w, groups, by_rows=False):
    shape = _w_shape(w)
    k = shape[1 if by_rows else 0]
    tk = _tile(k, 256)
    widths = [sum(p[1] for p in pieces) for pieces in groups]
    if by_rows:
        w_arr, w_spec = _layer_spec(w, (shape[0], tk), lambda i: (0, i))
        out_shape = tuple(jax.ShapeDtypeStruct((wd, k), BF16) for wd in widths)
        out_specs = tuple(pl.BlockSpec((wd, tk), lambda i: (0, i)) for wd in widths)
    else:
        w_arr, w_spec = _layer_spec(w, (tk, shape[1]), lambda i: (i, 0))
        out_shape = tuple(jax.ShapeDtypeStruct((k, wd), BF16) for wd in widths)
        out_specs = tuple(pl.BlockSpec((tk, wd), lambda i: (i, 0)) for wd in widths)
    return pl.pallas_call(
        functools.partial(_repack_kernel, groups=groups, by_rows=by_rows),
        out_shape=out_shape,
        grid=(k // tk,),
        in_specs=[w_spec],
        out_specs=out_specs,
        compiler_params=_cp("parallel"),
        name="weight_repack",
    )(w_arr)


def _pieces(*src_width_scale):
    out, dst = [], 0
    for p in src_width_scale:
        src, width = p[0], p[1]
        out.append((src, width, dst, p[2] if len(p) > 2 else 1.0))
        dst += width
    return tuple(out)


def _rope_tables(s):
    d = A_ROPE
    inv = 1.0 / (ROPE_THETA ** (jnp.arange(0, d, 2, dtype=F32) / d))
    ang = jnp.arange(s, dtype=F32)[:, None] * inv[None, :]
    cos, sin = jnp.cos(ang), jnp.sin(ang)
    z = jnp.zeros((s, LANE - d), F32)
    one = (jnp.concatenate([cos, cos, z], axis=1), jnp.concatenate([-sin, sin, z], axis=1))
    two = (jnp.concatenate([cos] * 4, axis=1), jnp.concatenate([-sin, sin] * 2, axis=1))
    return one, two


def _uq_kernel(cq_ref, g_ref, w_ref, cos_ref, sin_ref, o_ref, *, n_nope):
    x = cq_ref[...]
    y = x * lax.rsqrt(jnp.mean(x * x, axis=-1, keepdims=True) + EPS)
    h = (y * g_ref[...]).astype(BF16)
    n = w_ref.shape[1]
    cw = 512
    for c in range(0, n_nope, cw):
        o_ref[:, c:c + cw] = _dot(h, w_ref[:, c:c + cw]).astype(BF16)
    cos = cos_ref[...]
    sin = sin_ref[...]
    lane = lax.broadcasted_iota(jnp.int32, (x.shape[0], LANE), 1)
    first_half = (lane & (A_ROPE - 1)) < A_ROPE // 2
    low = lane < A_ROPE
    for c in range(n_nope, n, cw):
        y = _dot(h, w_ref[:, c:c + cw])
        parts = []
        for k in range(cw // LANE):
            slab = y[:, k * LANE:(k + 1) * LANE]
            partner = jnp.where(first_half, pltpu.roll(slab, LANE - A_ROPE // 2, 1), pltpu.roll(slab, A_ROPE // 2, 1))
            r = slab * cos + partner * sin
            parts.append(jnp.where(low, r, 0.0))
            parts.append(jnp.where(low, pltpu.roll(r, A_ROPE, 1), 0.0))
        d0 = n_nope + 2 * (c - n_nope)
        o_ref[:, d0:d0 + 2 * cw] = jnp.concatenate(parts, axis=1).astype(BF16)


def _uq_proj(cqkv, q_norm, w_uq2, cos_t, sin_t, s):
    m = cqkv.shape[0]
    n = w_uq2.shape[1]
    n_nope = A_HEADS * A_NOPE
    n_out = n_nope + 2 * (n - n_nope)
    tm = _tile(s, 1024)
    nblk = s // tm
    return pl.pallas_call(
        functools.partial(_uq_kernel, n_nope=n_nope),
        out_shape=jax.ShapeDtypeStruct((m, n_out), BF16),
        grid=(m // tm,),
        in_specs=[
            pl.BlockSpec((tm, A_Q_LORA), lambda i: (i, 0)),
            pl.BlockSpec((1, A_Q_LORA), lambda i: (0, 0)),
            pl.BlockSpec((A_Q_LORA, n), lambda i: (0, 0)),
            pl.BlockSpec((tm, LANE), lambda i: (i % nblk, 0)),
            pl.BlockSpec((tm, LANE), lambda i: (i % nblk, 0)),
        ],
        out_specs=pl.BlockSpec((tm, n_out), lambda i: (i, 0)),
        compiler_params=_cp("parallel"),
        name="mla_q_up",
    )(cqkv, q_norm.reshape(1, -1).astype(F32), w_uq2, cos_t, sin_t)


def _ukv_kernel(ckv_ref, kr_ref, g_ref, w_ref, cos_ref, sin_ref, kv_ref, krp_ref):
    x = ckv_ref[...]
    y = x * lax.rsqrt(jnp.mean(x * x, axis=-1, keepdims=True) + EPS)
    h = (y * g_ref[...]).astype(BF16)
    n = w_ref.shape[1]
    cw = 512
    for c in range(0, n, cw):
        kv_ref[:, c:c + cw] = _dot(h, w_ref[:, c:c + cw]).astype(BF16)
    slab = kr_ref[...]
    krp_ref[...] = (slab * cos_ref[...] + pltpu.roll(slab, LANE // 2, 1) * sin_ref[...]).astype(BF16)


def _ukv_proj(cqkv, kv_norm, w_ukv, cos_t, sin_t, s):
    m = cqkv.shape[0]
    n = _w_shape(w_ukv)[1]
    tm = _tile(s, 1024)
    nblk = s // tm
    cblk = A_Q_LORA // A_KV_LORA
    w_arr, w_spec = _layer_spec(w_ukv, (A_KV_LORA, n), lambda i: (0, 0))
    return pl.pallas_call(
        _ukv_kernel,
        out_shape=(jax.ShapeDtypeStruct((m, n), BF16), jax.ShapeDtypeStruct((m, LANE), BF16)),
        grid=(m // tm,),
        in_specs=[
            pl.BlockSpec((tm, A_KV_LORA), lambda i: (i, cblk)),
            pl.BlockSpec((tm, LANE), lambda i: (i, (A_Q_LORA + A_KV_LORA) // LANE)),
            pl.BlockSpec((1, A_KV_LORA), lambda i: (0, 0)),
            w_spec,
            pl.BlockSpec((tm, LANE), lambda i: (i % nblk, 0)),
            pl.BlockSpec((tm, LANE), lambda i: (i % nblk, 0)),
        ],
        out_specs=(pl.BlockSpec((tm, n), lambda i: (i, 0)), pl.BlockSpec((tm, LANE), lambda i: (i, 0))),
        compiler_params=_cp("parallel"),
        name="mla_kv_up",
    )(cqkv, cqkv, kv_norm.reshape(1, -1).astype(F32), w_arr, cos_t, sin_t)


def _rep2(x):
    return jnp.concatenate([x, x], axis=1)


def _silu(g):
    return g * (1.0 / (1.0 + jnp.exp(-g)))


def _mla_attn_kernel(qn_ref, qr_ref, kv_ref, kr_ref, g_ref, o_ref, kcat, vaug, m_scr, acc_scr, *, tq, hb):
    i = pl.program_id(2)

    @pl.when(i == 0)
    def _():
        ones = jnp.ones((kv_ref.shape[0], LANE), BF16)
        for hh in range(hb):
            kcat[hh, :, :LANE] = kv_ref[:, 2 * hh * LANE:(2 * hh + 1) * LANE]
            kcat[hh, :, LANE:] = kr_ref[...]
            vaug[hh, :, :LANE] = kv_ref[:, (2 * hh + 1) * LANE:(2 * hh + 2) * LANE]
            vaug[hh, :, LANE:] = ones

    m_scr[...] = jnp.full(m_scr.shape, -jnp.inf, F32)
    acc_scr[...] = jnp.zeros(acc_scr.shape, F32)
    qs = [jnp.concatenate([qn_ref[:, hh * LANE:(hh + 1) * LANE], qr_ref[:, hh * LANE:(hh + 1) * LANE]], axis=1)
          for hh in range(hb)]

    n_wide = lax.shift_right_logical(i, 1)

    def step(k0, tk, diag):
        for hh in range(hb):
            s = _dot_t(qs[hh], kcat[hh, pl.ds(k0, tk), :])
            if diag:
                row = lax.broadcasted_iota(jnp.int32, s.shape, 0) + (tk - tq)
                col = lax.broadcasted_iota(jnp.int32, s.shape, 1)
                s = jnp.where(row >= col, s, NEG_INF)
            m_prev = m_scr[hh]
            m_new = jnp.maximum(m_prev, jnp.max(s, axis=1, keepdims=True))
            alpha = jnp.exp2(m_prev - m_new)
            p = jnp.exp2(s - jnp.concatenate([m_new] * (tk // LANE), axis=1))
            acc_scr[hh] = _rep2(alpha) * acc_scr[hh] + _dot(p.astype(BF16), vaug[hh, pl.ds(k0, tk), :])
            m_scr[hh] = m_new

    def body(j, c):
        step(pl.multiple_of(j * 2 * tq, 2 * tq), 2 * tq, False)
        return c

    lax.fori_loop(0, n_wide, body, 0)

    @pl.when((i & 1) == 1)
    def _():
        step(pl.multiple_of((i - 1) * tq, 2 * tq), 2 * tq, True)

    @pl.when((i & 1) == 0)
    def _():
        step(pl.multiple_of(i * tq, 2 * tq), tq, True)
    for hh in range(hb):
        acc = acc_scr[hh]
        cols = slice(hh * LANE, (hh + 1) * LANE)
        gate = _silu(g_ref[:, cols].astype(F32))
        o_ref[:, cols] = (acc[:, :LANE] / acc[:, LANE:] * gate).astype(o_ref.dtype)


def _mla_attention(qp, kv, krp, g3, b, s):
    tq = _tile(s, 256)
    hn = A_HEADS
    hb = 8
    ng = hn // hb
    return pl.pallas_call(
        functools.partial(_mla_attn_kernel, tq=tq, hb=hb),
        out_shape=jax.ShapeDtypeStruct((b, s, hn * A_V), BF16),
        grid=(b, ng, s // tq),
        in_specs=[
            pl.BlockSpec((None, tq, hb * LANE), lambda bb, h, i: (bb, i, h)),
            pl.BlockSpec((None, tq, hb * LANE), lambda bb, h, i: (bb, i, ng + h)),
            pl.BlockSpec((None, s, 2 * hb * LANE), lambda bb, h, i: (bb, 0, h)),
            pl.BlockSpec((None, s, LANE), lambda bb, h, i: (bb, 0, 0)),
            pl.BlockSpec((None, tq, hb * LANE), lambda bb, h, i: (bb, i, h)),
        ],
        out_specs=pl.BlockSpec((None, tq, hb * LANE), lambda bb, h, i: (bb, i, h)),
        scratch_shapes=[
            pltpu.VMEM((hb, s, 2 * LANE), BF16),
            pltpu.VMEM((hb, s, 2 * LANE), BF16),
            pltpu.VMEM((hb, tq, LANE), F32),
            pltpu.VMEM((hb, tq, 2 * LANE), F32),
        ],
        compiler_params=_cp("parallel", "parallel", "arbitrary"),
        name="mla_attention",
    )(qp, qp, kv, krp, g3)


def _mem_attn_kernel(q_ref, kv_ref, g_ref, o_ref):
    d = MEM_HEAD_DIM
    for h in range(MEM_HEADS):
        cols = slice(h * d, (h + 1) * d)
        k = kv_ref[:, cols]
        v = kv_ref[:, MEM_WIDTH + h * d:MEM_WIDTH + (h + 1) * d]
        s = _dot_t(q_ref[:, cols], k) * (d ** -0.5)
        m = jnp.max(s, axis=1, keepdims=True)
        e = jnp.exp(s - m)
        p = e / jnp.sum(e, axis=1, keepdims=True)
        o = _dot(p.astype(BF16), v) * _silu(g_ref[:, cols].astype(F32))
        o_ref[:, cols] = o.astype(o_ref.dtype)


def _mem_attention(g3, mq_blk, gate_blk, memkv3):
    b, s, _ = g3.shape
    ml = memkv3.shape[1]
    tq = _tile(s, 512)
    return pl.pallas_call(
        _mem_attn_kernel,
        out_shape=jax.ShapeDtypeStruct((b, s, MEM_WIDTH), BF16),
        grid=(b, s // tq),
        in_specs=[
            pl.BlockSpec((None, tq, MEM_WIDTH), lambda bb, i: (bb, i, mq_blk)),
            pl.BlockSpec((None, ml, 2 * MEM_WIDTH), lambda bb, i: (bb, 0, 0)),
            pl.BlockSpec((None, tq, MEM_WIDTH), lambda bb, i: (bb, i, gate_blk)),
        ],
        out_specs=pl.BlockSpec((None, tq, MEM_WIDTH), lambda bb, i: (bb, i, 0)),
        compiler_params=_cp("parallel", "parallel"),
        name="mem_attention",
    )(g3, memkv3, g3)


def _out_kernel(ys_ref, ym_ref, x_ref, w_ref, g_ref, *refs, keep_x):
    if keep_x:
        xo_ref, h_ref, xrow = refs
    else:
        h_ref, xrow = refs
    j = pl.program_id(1)
    ws = ys_ref.shape[1]
    tn = x_ref.shape[1]
    xn = x_ref[...] + _dot(ys_ref[...], w_ref[:ws, :]) + _dot(ym_ref[...], w_ref[ws:, :])
    xrow[:, pl.ds(pl.multiple_of(j * tn, tn), tn)] = xn
    if keep_x:
        xo_ref[...] = xn

    @pl.when(j == pl.num_programs(1) - 1)
    def _():
        x = xrow[...]
        y = x * lax.rsqrt(jnp.mean(x * x, axis=-1, keepdims=True) + EPS)
        h_ref[...] = (y * g_ref[...]).astype(h_ref.dtype)


def _out_proj(ys, ym, x2d, w_out, g_next, keep_x):
    m, ws = ys.shape
    wm = ym.shape[1]
    kk, n = _w_shape(w_out)
    tm = _tile(m, 1024)
    tn = _tile(n, 512)
    w_arr, w_spec = _layer_spec(w_out, (kk, tn), lambda i, j: (0, j))
    h_dtype = BF16 if keep_x else F32
    row_spec = pl.BlockSpec((tm, n), lambda i, j: (i, 0))
    tile_spec = pl.BlockSpec((tm, tn), lambda i, j: (i, j))
    out_shape = [jax.ShapeDtypeStruct((m, n), h_dtype)]
    out_specs = [row_spec]
    if keep_x:
        out_shape.insert(0, jax.ShapeDtypeStruct((m, n), F32))
        out_specs.insert(0, tile_spec)
    return pl.pallas_call(
        functools.partial(_out_kernel, keep_x=keep_x),
        out_shape=tuple(out_shape),
        grid=(m // tm, n // tn),
        in_specs=[
            pl.BlockSpec((tm, ws), lambda i, j: (i, 0)),
            pl.BlockSpec((tm, wm), lambda i, j: (i, 0)),
            tile_spec,
            w_spec,
            pl.BlockSpec((1, n), lambda i, j: (0, 0)),
        ],
        out_specs=tuple(out_specs),
        scratch_shapes=[pltpu.VMEM((tm, n), F32)],
        compiler_params=_cp("parallel", "arbitrary"),
        name="out_proj_norm",
    )(ys, ym, x2d, w_arr, g_next.reshape(1, n).astype(F32))


def _t5_bucket(rel):
    n = jnp.maximum(rel, 0)
    max_exact = N_BUCKETS // 2
    nf = jnp.maximum(n, 1).astype(F32)
    large = max_exact + (jnp.log(nf / max_exact) / math.log(MAX_DISTANCE / max_exact)
                         * (N_BUCKETS - max_exact)).astype(jnp.int32)
    large = jnp.minimum(large, N_BUCKETS - 1)
    return jnp.where(n < max_exact, n, large)


def _bias_kernel(rb_ref, bd_ref, bp_ref, tswa_ref, tdsa_ref):
    bd = bd_ref[...]
    bp = bp_ref[...]
    for h in range(N_BIAS_HEADS):
        g, j = divmod(h, N_BIAS_HEADS // B_KV_HEADS)

        def body(b, c, h=h):
            d, p = c
            v = rb_ref[b, h]
            return jnp.where(bd == b, v, d), jnp.where(bp == b, v, p)

        z = jnp.zeros((QB, QB), F32)
        d, p = lax.fori_loop(0, N_BUCKETS, body, (z, z))
        far = rb_ref[N_BUCKETS - 1, h]
        rows = pl.ds(j * QB, QB)
        t = lax.broadcasted_iota(jnp.int32, (QB, QB), 0)
        c = lax.broadcasted_iota(jnp.int32, (QB, QB), 1)
        tswa_ref[0, g, rows, :] = jnp.where(t >= c, d * LOG2E, NEG_INF)
        tswa_ref[1, g, rows, :] = jnp.where(t + QB - c < WINDOW, p * LOG2E, NEG_INF)
        tdsa_ref[0, g, rows, :] = z
        tdsa_ref[1, g, rows, :] = (d - far) * LOG2E
        tdsa_ref[2, g, rows, :] = (p - far) * LOG2E


def _bias_tables(rel_bias):
    t = jnp.arange(QB)[:, None]
    c = jnp.arange(QB)[None, :]
    bd = _t5_bucket(t - c).astype(jnp.int32)
    bp = _t5_bucket(t + QB - c).astype(jnp.int32)
    rows = (N_BIAS_HEADS // B_KV_HEADS) * QB
    return pl.pallas_call(
        _bias_kernel,
        out_shape=(jax.ShapeDtypeStruct((2, B_KV_HEADS, rows, QB), F32),
                   jax.ShapeDtypeStruct((3, B_KV_HEADS, rows, QB), F32)),
        in_specs=[pl.BlockSpec(memory_space=pltpu.SMEM),
                  pl.BlockSpec(memory_space=pltpu.VMEM),
                  pl.BlockSpec(memory_space=pltpu.VMEM)],
        out_specs=(pl.BlockSpec(memory_space=pltpu.VMEM), pl.BlockSpec(memory_space=pltpu.VMEM)),
        compiler_params=pltpu.CompilerParams(vmem_limit_bytes=VMEM_LIMIT),
        name="t5_bias_tables",
    )(rel_bias.astype(F32), bd, bp)


def _swa_kernel(sink_ref, q_ref, kp_ref, kc_ref, vp_ref, vc_ref, t_ref, g_ref, o_ref):
    i = pl.program_id(1)
    n_per = C_HEADS // C_KV_HEADS
    d = C_HEAD_DIM
    kb = jnp.concatenate([kp_ref[...], kc_ref[...]], axis=0)
    vb = jnp.concatenate([vp_ref[...], vc_ref[...]], axis=0)
    ones = jnp.ones((2 * QB, d), BF16)
    no_prev = jnp.where(i == 0, NEG_INF, 0.0).astype(F32)
    low = lax.broadcasted_iota(jnp.int32, (QB, LANE), 1) < d
    outs = []
    for g in range(C_KV_HEADS):
        kg = kb[:, g * d:(g + 1) * d]
        vg = vb[:, g * d:(g + 1) * d]
        vaug = (jnp.concatenate([vg, ones, ones, vg], axis=1), jnp.concatenate([ones, vg, vg, ones], axis=1))
        zk = jnp.zeros_like(kg)
        kpar = (jnp.concatenate([kg, zk], axis=1), jnp.concatenate([zk, kg], axis=1))
        for jp in range(n_per // 2):
            pair = []
            q2 = q_ref[:, (g * n_per + 2 * jp) * d:(g * n_per + 2 * jp + 2) * d]
            for par in range(2):
                j = 2 * jp + par
                h = g * n_per + j
                rows = pl.ds(j * QB, QB)
                s = _dot_t(q2, kpar[par])
                s = s + jnp.concatenate([t_ref[1, g, rows, :] + no_prev, t_ref[0, g, rows, :]], axis=1)
                sink = sink_ref[0, h] * LOG2E
                m = jnp.maximum(jnp.broadcast_to(jnp.max(s, axis=1, keepdims=True), (QB, LANE)), sink)
                e = jnp.exp2(s - _rep2(m))
                acc = _dot(e.astype(BF16), vaug[par])
                pair.append(acc[:, :LANE] / (acc[:, LANE:] + jnp.exp2(sink - m)))
            outs.append(jnp.where(low, pair[0], pair[1]))
    o = jnp.concatenate(outs, axis=1) * _silu(g_ref[...].astype(F32))
    o_ref[...] = o.astype(o_ref.dtype)


def _swa_attention(g3, sinks, tswa, q_blk, k_blk, v_blk):
    b, s, _ = g3.shape
    wq = C_HEADS * C_HEAD_DIM
    wk = C_KV_HEADS * C_HEAD_DIM
    prev = lambda i: jnp.maximum(i - 1, 0)
    return pl.pallas_call(
        _swa_kernel,
        out_shape=jax.ShapeDtypeStruct((b, s, wq), BF16),
        grid=(b, s // QB),
        in_specs=[
            pl.BlockSpec(memory_space=pltpu.SMEM),
            pl.BlockSpec((None, QB, wq), lambda bb, i: (bb, i, q_blk)),
            pl.BlockSpec((None, QB, wk), lambda bb, i: (bb, prev(i), k_blk)),
            pl.BlockSpec((None, QB, wk), lambda bb, i: (bb, i, k_blk)),
            pl.BlockSpec((None, QB, wk), lambda bb, i: (bb, prev(i), v_blk)),
            pl.BlockSpec((None, QB, wk), lambda bb, i: (bb, i, v_blk)),
            pl.BlockSpec(tswa.shape, lambda bb, i: (0, 0, 0, 0)),
            pl.BlockSpec((None, QB, wq), lambda bb, i: (bb, i, 0)),
        ],
        out_specs=pl.BlockSpec((None, QB, wq), lambda bb, i: (bb, i, 0)),
        compiler_params=_cp("parallel", "parallel"),
        name="swa_attention",
    )(sinks.reshape(1, -1).astype(F32), g3, g3, g3, g3, g3, tswa, g3)


def _ordered_to_f32(key):
    bits = jnp.where(key >= 0, key, key ^ jnp.int32(0x7FFFFFFF))
    return lax.bitcast_convert_type(bits, F32)


def _dsa_kernel(q_ref, k_ref, v_ref, iq_ref, ik_ref, iw_ref, t_ref, g_ref, o_ref,
                sc, madd, iqs, ks, vaug, m_scr, acc_scr, *, s_len, k_top):
    i = pl.program_id(1)
    n_per = B_HEADS // B_KV_HEADS
    d = B_HEAD_DIM

    @pl.when(i == 0)
    def _():
        ones = jnp.ones((s_len, d), BF16)
        zeros = jnp.zeros((s_len, d), BF16)
        for g in range(B_KV_HEADS):
            kg = k_ref[:, g * d:(g + 1) * d]
            vg = v_ref[:, g * d:(g + 1) * d]
            ks[g, 0] = jnp.concatenate([kg, zeros], axis=1)
            ks[g, 1] = jnp.concatenate([zeros, kg], axis=1)
            vaug[g, 0] = jnp.concatenate([vg, ones], axis=1)
            vaug[g, 1] = jnp.concatenate([ones, vg], axis=1)

    half = lax.shift_right_logical(i, 1)
    odd = i & 1
    n_t = half + 1
    sub = 8
    key_i = lax.broadcasted_iota(jnp.int32, (KT, QB), 0)
    qry_i = i * QB + lax.broadcasted_iota(jnp.int32, (KT, QB), 1)

    for hp in range(IDX_HEADS // 2):
        for par in range(2):
            h = 2 * hp + par
            iqs[hp, par * QB:(par + 1) * QB, :] = iq_ref[:, h * IDX_DIM:(h + 1) * IDX_DIM]
    w = iw_ref[...] * (IDX_DIM ** -0.5 * IDX_HEADS ** -0.5)
    wrows = [w[h:h + 1, :] for h in range(IDX_HEADS)]

    def score_tile(kt, c):
        k0 = pl.multiple_of(kt * KT, KT)
        ikt = ik_ref[pl.ds(k0, KT), :]
        acc = jnp.zeros((KT, QB), F32)
        for hp in range(IDX_HEADS // 2):
            dots = _dot_t(ikt, iqs[hp])
            acc = acc + jnp.maximum(dots[:, :QB], 0.0) * wrows[2 * hp]
            acc = acc + jnp.maximum(dots[:, QB:], 0.0) * wrows[2 * hp + 1]
        sc[pl.ds(k0, KT), :] = jnp.where(qry_i >= k0 + key_i, acc, NEG_INF)
        return c

    lax.fori_loop(0, n_t, score_tile, 0)

    tail = (s_len - n_t * KT).astype(F32)
    int_min = jnp.int32(-2 ** 31)

    def bisect(b, prefix):
        cand = prefix + lax.shift_left(jnp.int32(1), 31 - b)
        cand_f = _ordered_to_f32(cand)

        def count_tile(kt, c):
            k0 = pl.multiple_of(kt * KT, KT)
            t = sc[pl.ds(k0, KT), :].reshape(4, KT // (4 * sub), sub, QB)
            part = jnp.sum(jnp.where(t >= cand_f[None, None], 1.0, 0.0), axis=1)
            return c + ((part[0] + part[1]) + (part[2] + part[3]))

        cnt = lax.fori_loop(0, n_t, count_tile, jnp.zeros((sub, QB), F32))
        total = jnp.sum(cnt, axis=0, keepdims=True) + jnp.where(cand_f <= NEG_INF, tail, 0.0)
        return jnp.where(total >= k_top, cand, prefix)

    thr = _ordered_to_f32(lax.fori_loop(0, 32, bisect, jnp.full((sub, QB), int_min, jnp.int32)))[:1]

    def mask_tile(kt, c):
        k0 = pl.multiple_of(kt * KT, KT)
        sel = (sc[pl.ds(k0, KT), :] >= thr) & (qry_i >= k0 + key_i)
        madd[:, pl.ds(k0, KT)] = jnp.where(sel, 0.0, NEG_INF).T
        return c

    lax.fori_loop(0, n_t, mask_tile, 0)

    near0 = jnp.where(odd == 1, half, jnp.maximum(half - 1, 0))
    m_scr[...] = jnp.full(m_scr.shape, -jnp.inf, F32)
    acc_scr[...] = jnp.zeros(acc_scr.shape, F32)

    def step(kt, near):
        k0 = pl.multiple_of(kt * KT, KT)
        mt = madd[:, pl.ds(k0, KT)]
        if near:
            li = jnp.where(odd == 1, 2, jnp.where(kt == half, 1, 0))
            ri = jnp.where(odd == 1, 1, jnp.where(kt == half, 0, 2))
        for g in range(B_KV_HEADS):
            kg = (ks[g, 0, pl.ds(k0, KT), :], ks[g, 1, pl.ds(k0, KT), :])
            vg = (vaug[g, 0, pl.ds(k0, KT), :], vaug[g, 1, pl.ds(k0, KT), :])
            for j in range(n_per):
                h = g * n_per + j
                q2 = q_ref[:, (h // 2) * LANE:(h // 2 + 1) * LANE]
                s = _dot_t(q2, kg[h % 2]) + mt
                if near:
                    rows = pl.ds(j * QB, QB)
                    s = s + jnp.concatenate([t_ref[li, g, rows, :], t_ref[ri, g, rows, :]], axis=1)
                m_prev = m_scr[h]
                m_new = jnp.maximum(m_prev, jnp.max(s, axis=1, keepdims=True))
                alpha = jnp.exp2(m_prev - m_new)
                p = jnp.exp2(s - _rep2(m_new))
                acc_scr[h] = alpha * acc_scr[h] + _dot(p.astype(BF16), vg[h % 2])
                m_scr[h] = m_new

    def far_body(kt, c):
        step(kt, False)
        return c

    def near_body(kt, c):
        step(kt, True)
        return c

    lax.fori_loop(0, near0, far_body, 0)
    lax.fori_loop(near0, n_t, near_body, 0)
    low = lax.broadcasted_iota(jnp.int32, (QB, LANE), 1) < d
    outs = []
    for hp in range(B_HEADS // 2):
        acc_e = acc_scr[2 * hp]
        acc_o = acc_scr[2 * hp + 1]
        num = jnp.where(low, acc_e, acc_o)
        den = pltpu.roll(jnp.where(low, acc_o, acc_e), d, 1)
        outs.append(num / den)
    o = jnp.concatenate(outs, axis=1) * _silu(g_ref[...].astype(F32))
    o_ref[...] = o.astype(o_ref.dtype)


def _dsa_attention(g3, ik3, iwt3, tdsa, q_blk, k_blk, v_blk, iq_blk):
    b, s, _ = g3.shape
    k_top = min(IDX_TOPK_MAX, s // 4)
    wq = B_HEADS * B_HEAD_DIM
    wk = B_KV_HEADS * B_HEAD_DIM
    wi = IDX_HEADS * IDX_DIM
    return pl.pallas_call(
        functools.partial(_dsa_kernel, s_len=s, k_top=k_top),
        out_shape=jax.ShapeDtypeStruct((b, s, wq), BF16),
        grid=(b, s // QB),
        in_specs=[
            pl.BlockSpec((None, QB, wq), lambda bb, i: (bb, i, q_blk)),
            pl.BlockSpec((None, s, wk), lambda bb, i: (bb, 0, k_blk)),
            pl.BlockSpec((None, s, wk), lambda bb, i: (bb, 0, v_blk)),
            pl.BlockSpec((None, QB, wi), lambda bb, i: (bb, i, iq_blk)),
            pl.BlockSpec((None, s, IDX_DIM), lambda bb, i: (bb, 0, 0)),
            pl.BlockSpec((None, IDX_HEADS, QB), lambda bb, i: (bb, 0, i)),
            pl.BlockSpec(tdsa.shape, lambda bb, i: (0, 0, 0, 0)),
            pl.BlockSpec((None, QB, wq), lambda bb, i: (bb, i, 0)),
        ],
        out_specs=pl.BlockSpec((None, QB, wq), lambda bb, i: (bb, i, 0)),
        scratch_shapes=[
            pltpu.VMEM((s, QB), F32),
            pltpu.VMEM((QB, s), F32),
            pltpu.VMEM((IDX_HEADS // 2, 2 * QB, IDX_DIM), BF16),
            pltpu.VMEM((B_KV_HEADS, 2, s, 2 * B_HEAD_DIM), BF16),
            pltpu.VMEM((B_KV_HEADS, 2, s, 2 * B_HEAD_DIM), BF16),
            pltpu.VMEM((B_HEADS, QB, LANE), F32),
            pltpu.VMEM((B_HEADS, QB, 2 * B_HEAD_DIM), F32),
        ],
        compiler_params=_cp("parallel", "arbitrary"),
        name="dsa_attention",
    )(g3, g3, g3, g3, ik3, iwt3, tdsa, g3)


def _mla_layer(h, w_in_t, q_norm, w_uq, kv_norm, w_ukv, rope1, rope2, b, s):
    o_ckv = A_Q_LORA + A_KV_LORA
    o_mq = o_ckv + A_ROPE
    o_gate = o_mq + MEM_WIDTH
    half = A_ROPE // 2
    w_f, w_g = _repack(w_in_t, (
        _pieces((0, o_ckv), (o_ckv, A_ROPE), (o_ckv + half, half), (o_ckv, half)),
        _pieces((o_gate, A_HEADS * A_V + MEM_WIDTH), (o_mq, MEM_WIDTH)),
    ), by_rows=True)
    cqkv = _matmul(h, w_f, F32, w_rows_are_outputs=True)
    g2 = _matmul(h, w_g, BF16, w_rows_are_outputs=True)

    hd = A_NOPE + A_ROPE
    (w_uq2,) = _repack(w_uq, (
        _pieces(*[(hh * hd, A_NOPE, hd ** -0.5 * LOG2E) for hh in range(A_HEADS)],
                *[(hh * hd + A_NOPE, A_ROPE, hd ** -0.5 * LOG2E) for hh in range(A_HEADS)]),
    ))
    qp = _uq_proj(cqkv, q_norm, w_uq2, rope2[0], rope2[1], s)
    kv, krp = _ukv_proj(cqkv, kv_norm, w_ukv, rope1[0], rope1[1], s)
    self_out = _mla_attention(qp.reshape(b, s, -1), kv.reshape(b, s, -1), krp.reshape(b, s, -1),
                              g2.reshape(b, s, -1), b, s)
    return self_out.reshape(b * s, -1), g2


def _dsa_layer(h, w_in_t, tdsa, b, s):
    wq = B_HEADS * B_HEAD_DIM
    wk = B_KV_HEADS * B_HEAD_DIM
    wi = IDX_HEADS * IDX_DIM
    o_k = wq
    o_v = o_k + wk
    o_iq = o_v + wk
    o_ik = o_iq + wi
    o_iw = o_ik + IDX_DIM
    o_mq = o_iw + IDX_HEADS
    o_gate = o_mq + MEM_WIDTH
    w_g, w_i = _repack(w_in_t, (
        _pieces((o_gate, wq + MEM_WIDTH), (o_mq, MEM_WIDTH), (0, wq, B_HEAD_DIM ** -0.5 * LOG2E),
                (o_iq, wi), (o_k, wk), (o_v, wk)),
        _pieces((o_ik, IDX_DIM + IDX_HEADS)),
    ), by_rows=True)
    g2 = _matmul(h, w_g, BF16, w_rows_are_outputs=True)
    ii = _matmul(h, w_i, F32, w_rows_are_outputs=True)
    ik3 = ii[:, :IDX_DIM].astype(BF16).reshape(b, s, IDX_DIM)
    iwt3 = ii[:, IDX_DIM:].reshape(b, s, IDX_HEADS).transpose(0, 2, 1)
    self_out = _dsa_attention(g2.reshape(b, s, -1), ik3, iwt3, tdsa,
                              q_blk=2, k_blk=7168 // wk, v_blk=7424 // wk, iq_blk=6)
    return self_out.reshape(b * s, -1), g2


def _swa_layer(h, w_in, sinks, tswa, b, s):
    wq = C_HEADS * C_HEAD_DIM
    wk = C_KV_HEADS * C_HEAD_DIM
    o_k = wq
    o_v = o_k + wk
    o_mq = o_v + wk
    o_gate = o_mq + MEM_WIDTH
    (w_g,) = _repack(w_in, (
        _pieces((o_gate, wq + MEM_WIDTH), (o_mq, MEM_WIDTH), (0, wq, C_HEAD_DIM ** -0.5 * LOG2E),
                (o_k, wk), (o_v, wk)),
    ))
    g2 = _matmul(h, w_g, BF16)
    self_out = _swa_attention(g2.reshape(b, s, -1), sinks, tswa, q_blk=2, k_blk=6144 // wk, v_blk=6400 // wk)
    return self_out.reshape(b * s, -1), g2


def kernel(x, mem, norm_in, final_norm, mem_norm, rel_bias, w_in_a, a_q_norm, w_uq, a_kv_norm, w_ukv,
           w_in_b, w_in_c, c_sinks, w_mem_kv, w_out):
    b, s, d = x.shape
    depth = norm_in.shape[0]
    ml = mem.shape[1]
    xf = x.reshape(b * s, d)
    mem_n = _rmsnorm(mem.reshape(b * ml, d), mem_norm, BF16)
    rope1, rope2 = _rope_tables(s)
    tswa, tdsa = _bias_tables(rel_bias)
    h = _rmsnorm(xf, norm_in[0], BF16)
    w_ukv_b, w_mem_b, w_out_b = w_ukv.astype(BF16), w_mem_kv.astype(BF16), w_out.astype(BF16)
    w_in_a_t, w_in_b_t = jnp.swapaxes(w_in_a, 1, 2), jnp.swapaxes(w_in_b, 1, 2)
    for i in range(depth):
        kind, j = i % N_MIXERS, i // N_MIXERS
        if kind == 0:
            ys, g2 = _mla_layer(h, (w_in_a_t, j), a_q_norm[j], (w_uq, j), a_kv_norm[j], (w_ukv_b, j),
                                rope1, rope2, b, s)
        elif kind == 1:
            ys, g2 = _dsa_layer(h, (w_in_b_t, j), tdsa, b, s)
        else:
            ys, g2 = _swa_layer(h, (w_in_c, j), c_sinks[j], tswa, b, s)
        memkv = _matmul(mem_n, (w_mem_b, i), BF16)
        ym = _mem_attention(g2.reshape(b, s, -1), 3, 2, memkv.reshape(b, ml, -1)).reshape(b * s, -1)
        last = i == depth - 1
        res = _out_proj(ys, ym, xf, (w_out_b, i), final_norm if last else norm_in[i + 1], not last)
        if last:
            return res[0].reshape(b, s, d)
        xf, h = res
```

```python
import functools
import math

import jax
import jax.numpy as jnp
from jax import lax
from jax.experimental import pallas as pl
from jax.experimental.pallas import tpu as pltpu

F32 = jnp.float32
BF16 = jnp.bfloat16

EPS = 1e-6
NEG_INF = -1e30
N_MIXERS = 3

N_BUCKETS = 32
MAX_DISTANCE = 128
N_BIAS_HEADS = 32

MEM_HEADS = 4
MEM_HEAD_DIM = 256
MEM_WIDTH = MEM_HEADS * MEM_HEAD_DIM

A_HEADS = 16
A_Q_LORA = 1536
A_KV_LORA = 512
A_NOPE = 128
A_ROPE = 64
A_V = 128
ROPE_THETA = 10000.0

B_HEADS = 32
B_KV_HEADS = 4
B_HEAD_DIM = 64
IDX_HEADS = 16
IDX_DIM = 64
IDX_TOPK_MAX = 256

C_HEADS = 32
C_KV_HEADS = 4
C_HEAD_DIM = 64
WINDOW = 128

LANE = 128
MXU_N = 256
QB = 128
KT = MXU_N
VMEM_LIMIT = 56 * 1024 * 1024
LOG2E = math.log2(math.e)
MM_MAX_TN = 13 * MXU_N


def _cp(*sem):
    return pltpu.CompilerParams(dimension_semantics=sem, vmem_limit_bytes=VMEM_LIMIT)


def _tile(n, pref):
    t = min(n, pref)
    assert n % t == 0, (n, pref)
    return t


def _dot(a, b):
    return jnp.dot(a, b, preferred_element_type=F32)


def _dot_t(a, b):
    return lax.dot_general(a, b, (((1,), (1,)), ((), ())), preferred_element_type=F32)


def _rms_kernel(x_ref, g_ref, o_ref):
    x = x_ref[...].astype(F32)
    y = x * lax.rsqrt(jnp.mean(x * x, axis=-1, keepdims=True) + EPS)
    o_ref[...] = (y * g_ref[...]).astype(o_ref.dtype)


def _rmsnorm(x2d, g, out_dtype):
    m, d = x2d.shape
    tm = _tile(m, 512)
    return pl.pallas_call(
        _rms_kernel,
        out_shape=jax.ShapeDtypeStruct((m, d), out_dtype),
        grid=(m // tm,),
        in_specs=[pl.BlockSpec((tm, d), lambda i: (i, 0)), pl.BlockSpec((1, d), lambda i: (0, 0))],
        out_specs=pl.BlockSpec((tm, d), lambda i: (i, 0)),
        compiler_params=_cp("parallel"),
        name="rmsnorm",
    )(x2d, g.reshape(1, d).astype(F32))


def _mm_kernel(a_ref, w_ref, o_ref, *, w_rows_are_outputs):
    dot = _dot_t if w_rows_are_outputs else _dot
    o_ref[...] = dot(a_ref[...], w_ref[...]).astype(o_ref.dtype)


def _layer_spec(w, block, index_map):
    if isinstance(w, tuple):
        stack, layer = w
        return stack, pl.BlockSpec((None,) + block, lambda *g: (layer,) + index_map(*g))
    return w, pl.BlockSpec(block, index_map)


def _w_shape(w):
    return w[0].shape[1:] if isinstance(w, tuple) else w.shape


def _matmul(a, w, out_dtype, w_rows_are_outputs=False):
    m, k = a.shape
    n = _w_shape(w)[0 if w_rows_are_outputs else 1]
    tm = _tile(m, 1024)
    tn = max([t for t in range(MXU_N, min(n, MM_MAX_TN) + 1, MXU_N) if n % t == 0]
             or [t for t in range(LANE, min(n, MM_MAX_TN) + 1, LANE) if n % t == 0] or [n])
    if w_rows_are_outputs:
        w_arr, w_spec = _layer_spec(w, (tn, k), lambda i, j: (j, 0))
    else:
        w_arr, w_spec = _layer_spec(w, (k, tn), lambda i, j: (0, j))
    return pl.pallas_call(
        functools.partial(_mm_kernel, w_rows_are_outputs=w_rows_are_outputs),
        out_shape=jax.ShapeDtypeStruct((m, n), out_dtype),
        grid=(m // tm, n // tn),
        in_specs=[pl.BlockSpec((tm, k), lambda i, j: (i, 0)), w_spec],
        out_specs=pl.BlockSpec((tm, tn), lambda i, j: (i, j)),
        compiler_params=_cp("parallel", "arbitrary"),
        name="matmul",
    )(a, w_arr)


def _repack_kernel(w_ref, *o_refs, groups, by_rows):
    for o_ref, pieces in zip(o_refs, groups):
        for src, width, dst, scale in pieces:
            v = w_ref[src:src + width, :] if by_rows else w_ref[:, src:src + width]
            if scale != 1.0:
                v = v * scale
            if by_rows:
                o_ref[dst:dst + width, :] = v.astype(o_ref.dtype)
            else:
                o_ref[:, dst:dst + width] = v.astype(o_ref.dtype)


def _repack(w, groups, by_rows=False):
    shape = _w_shape(w)
    k = shape[1 if by_rows else 0]
    tk = _tile(k, 256)
    widths = [sum(p[1] for p in pieces) for pieces in groups]
    if by_rows:
        w_arr, w_spec = _layer_spec(w, (shape[0], tk), lambda i: (0, i))
        out_shape = tuple(jax.ShapeDtypeStruct((wd, k), BF16) for wd in widths)
        out_specs = tuple(pl.BlockSpec((wd, tk), lambda i: (0, i)) for wd in widths)
    else:
        w_arr, w_spec = _layer_spec(w, (tk, shape[1]), lambda i: (i, 0))
        out_shape = tuple(jax.ShapeDtypeStruct((k, wd), BF16) for wd in widths)
        out_specs = tuple(pl.BlockSpec((tk, wd), lambda i: (i, 0)) for wd in widths)
    return pl.pallas_call(
        functools.partial(_repack_kernel, groups=groups, by_rows=by_rows),
        out_shape=out_shape,
        grid=(k // tk,),
        in_specs=[w_spec],
        out_specs=out_specs,
        compiler_params=_cp("parallel"),
        name="weight_repack",
    )(w_arr)


def _pieces(*src_width_scale):
    out, dst = [], 0
    for p in src_width_scale:
        src, width = p[0], p[1]
        out.append((src, width, dst, p[2] if len(p) > 2 else 1.0))
        dst += width
    return tuple(out)


def _rope_tables(s):
    d = A_ROPE
    inv = 1.0 / (ROPE_THETA ** (jnp.arange(0, d, 2, dtype=F32) / d))
    ang = jnp.arange(s, dtype=F32)[:, None] * inv[None, :]
    cos, sin = jnp.cos(ang), jnp.sin(ang)
    z = jnp.zeros((s, LANE - d), F32)
    one = (jnp.concatenate([cos, cos, z], axis=1), jnp.concatenate([-sin, sin, z], axis=1))
    two = (jnp.concatenate([cos] * 4, axis=1), jnp.concatenate([-sin, sin] * 2, axis=1))
    return one, two


def _uq_kernel(cq_ref, g_ref, w_ref, cos_ref, sin_ref, o_ref, *, n_nope):
    x = cq_ref[...]
    y = x * lax.rsqrt(jnp.mean(x * x, axis=-1, keepdims=True) + EPS)
    h = (y * g_ref[...]).astype(BF16)
    n = w_ref.shape[1]
    cw = 512
    for c in range(0, n_nope, cw):
        o_ref[:, c:c + cw] = _dot(h, w_ref[:, c:c + cw]).astype(BF16)
    cos = cos_ref[...]
    sin = sin_ref[...]
    lane = lax.broadcasted_iota(jnp.int32, (x.shape[0], LANE), 1)
    first_half = (lane & (A_ROPE - 1)) < A_ROPE // 2
    low = lane < A_ROPE
    for c in range(n_nope, n, cw):
        y = _dot(h, w_ref[:, c:c + cw])
        parts = []
        for k in range(cw // LANE):
            slab = y[:, k * LANE:(k + 1) * LANE]
            partner = jnp.where(first_half, pltpu.roll(slab, LANE - A_ROPE // 2, 1), pltpu.roll(slab, A_ROPE // 2, 1))
            r = slab * cos + partner * sin
            parts.append(jnp.where(low, r, 0.0))
            parts.append(jnp.where(low, pltpu.roll(r, A_ROPE, 1), 0.0))
        d0 = n_nope + 2 * (c - n_nope)
        o_ref[:, d0:d0 + 2 * cw] = jnp.concatenate(parts, axis=1).astype(BF16)


def _uq_proj(cqkv, q_norm, w_uq2, cos_t, sin_t, s):
    m = cqkv.shape[0]
    n = w_uq2.shape[1]
    n_nope = A_HEADS * A_NOPE
    n_out = n_nope + 2 * (n - n_nope)
    tm = _tile(s, 1024)
    nblk = s // tm
    return pl.pallas_call(
        functools.partial(_uq_kernel, n_nope=n_nope),
        out_shape=jax.ShapeDtypeStruct((m, n_out), BF16),
        grid=(m // tm,),
        in_specs=[
            pl.BlockSpec((tm, A_Q_LORA), lambda i: (i, 0)),
            pl.BlockSpec((1, A_Q_LORA), lambda i: (0, 0)),
            pl.BlockSpec((A_Q_LORA, n), lambda i: (0, 0)),
            pl.BlockSpec((tm, LANE), lambda i: (i % nblk, 0)),
            pl.BlockSpec((tm, LANE), lambda i: (i % nblk, 0)),
        ],
        out_specs=pl.BlockSpec((tm, n_out), lambda i: (i, 0)),
        compiler_params=_cp("parallel"),
        name="mla_q_up",
    )(cqkv, q_norm.reshape(1, -1).astype(F32), w_uq2, cos_t, sin_t)


def _ukv_kernel(ckv_ref, kr_ref, g_ref, w_ref, cos_ref, sin_ref, kv_ref, krp_ref):
    x = ckv_ref[...]
    y = x * lax.rsqrt(jnp.mean(x * x, axis=-1, keepdims=True) + EPS)
    h = (y * g_ref[...]).astype(BF16)
    n = w_ref.shape[1]
    cw = 512
    for c in range(0, n, cw):
        kv_ref[:, c:c + cw] = _dot(h, w_ref[:, c:c + cw]).astype(BF16)
    slab = kr_ref[...]
    krp_ref[...] = (slab * cos_ref[...] + pltpu.roll(slab, LANE // 2, 1) * sin_ref[...]).astype(BF16)


def _ukv_proj(cqkv, kv_norm, w_ukv, cos_t, sin_t, s):
    m = cqkv.shape[0]
    n = _w_shape(w_ukv)[1]
    tm = _tile(s, 1024)
    nblk = s // tm
    cblk = A_Q_LORA // A_KV_LORA
    w_arr, w_spec = _layer_spec(w_ukv, (A_KV_LORA, n), lambda i: (0, 0))
    return pl.pallas_call(
        _ukv_kernel,
        out_shape=(jax.ShapeDtypeStruct((m, n), BF16), jax.ShapeDtypeStruct((m, LANE), BF16)),
        grid=(m // tm,),
        in_specs=[
            pl.BlockSpec((tm, A_KV_LORA), lambda i: (i, cblk)),
            pl.BlockSpec((tm, LANE), lambda i: (i, (A_Q_LORA + A_KV_LORA) // LANE)),
            pl.BlockSpec((1, A_KV_LORA), lambda i: (0, 0)),
            w_spec,
            pl.BlockSpec((tm, LANE), lambda i: (i % nblk, 0)),
            pl.BlockSpec((tm, LANE), lambda i: (i % nblk, 0)),
        ],
        out_specs=(pl.BlockSpec((tm, n), lambda i: (i, 0)), pl.BlockSpec((tm, LANE), lambda i: (i, 0))),
        compiler_params=_cp("parallel"),
        name="mla_kv_up",
    )(cqkv, cqkv, kv_norm.reshape(1, -1).astype(F32), w_arr, cos_t, sin_t)


def _rep2(x):
    return jnp.concatenate([x, x], axis=1)


def _silu(g):
    return g * (1.0 / (1.0 + jnp.exp(-g)))


def _mla_attn_kernel(qn_ref, qr_ref, kv_ref, kr_ref, g_ref, o_ref, kcat, vaug, m_scr, acc_scr, *, tq, hb):
    i = pl.program_id(2)

    @pl.when(i == 0)
    def _():
        ones = jnp.ones((kv_ref.shape[0], LANE), BF16)
        for hh in range(hb):
            kcat[hh, :, :LANE] = kv_ref[:, 2 * hh * LANE:(2 * hh + 1) * LANE]
            kcat[hh, :, LANE:] = kr_ref[...]
            vaug[hh, :, :LANE] = kv_ref[:, (2 * hh + 1) * LANE:(2 * hh + 2) * LANE]
            vaug[hh, :, LANE:] = ones

    m_scr[...] = jnp.full(m_scr.shape, -jnp.inf, F32)
    acc_scr[...] = jnp.zeros(acc_scr.shape, F32)
    qs = [jnp.concatenate([qn_ref[:, hh * LANE:(hh + 1) * LANE], qr_ref[:, hh * LANE:(hh + 1) * LANE]], axis=1)
          for hh in range(hb)]

    n_wide = lax.shift_right_logical(i, 1)

    def step(k0, tk, diag):
        for hh in range(hb):
            s = _dot_t(qs[hh], kcat[hh, pl.ds(k0, tk), :])
            if diag:
                row = lax.broadcasted_iota(jnp.int32, s.shape, 0) + (tk - tq)
                col = lax.broadcasted_iota(jnp.int32, s.shape, 1)
                s = jnp.where(row >= col, s, NEG_INF)
            m_prev = m_scr[hh]
            m_new = jnp.maximum(m_prev, jnp.max(s, axis=1, keepdims=True))
            alpha = jnp.exp2(m_prev - m_new)
            p = jnp.exp2(s - jnp.concatenate([m_new] * (tk // LANE), axis=1))
            acc_scr[hh] = _rep2(alpha) * acc_scr[hh] + _dot(p.astype(BF16), vaug[hh, pl.ds(k0, tk), :])
            m_scr[hh] = m_new

    def body(j, c):
        step(pl.multiple_of(j * 2 * tq, 2 * tq), 2 * tq, False)
        return c

    lax.fori_loop(0, n_wide, body, 0)

    @pl.when((i & 1) == 1)
    def _():
        step(pl.multiple_of((i - 1) * tq, 2 * tq), 2 * tq, True)

    @pl.when((i & 1) == 0)
    def _():
        step(pl.multiple_of(i * tq, 2 * tq), tq, True)
    for hh in range(hb):
        acc = acc_scr[hh]
        cols = slice(hh * LANE, (hh + 1) * LANE)
        gate = _silu(g_ref[:, cols].astype(F32))
        o_ref[:, cols] = (acc[:, :LANE] / acc[:, LANE:] * gate).astype(o_ref.dtype)


def _mla_attention(qp, kv, krp, g3, b, s):
    tq = _tile(s, 256)
    hn = A_HEADS
    hb = 8
    ng = hn // hb
    return pl.pallas_call(
        functools.partial(_mla_attn_kernel, tq=tq, hb=hb),
        out_shape=jax.ShapeDtypeStruct((b, s, hn * A_V), BF16),
        grid=(b, ng, s // tq),
        in_specs=[
            pl.BlockSpec((None, tq, hb * LANE), lambda bb, h, i: (bb, i, h)),
            pl.BlockSpec((None, tq, hb * LANE), lambda bb, h, i: (bb, i, ng + h)),
            pl.BlockSpec((None, s, 2 * hb * LANE), lambda bb, h, i: (bb, 0, h)),
            pl.BlockSpec((None, s, LANE), lambda bb, h, i: (bb, 0, 0)),
            pl.BlockSpec((None, tq, hb * LANE), lambda bb, h, i: (bb, i, h)),
        ],
        out_specs=pl.BlockSpec((None, tq, hb * LANE), lambda bb, h, i: (bb, i, h)),
        scratch_shapes=[
            pltpu.VMEM((hb, s, 2 * LANE), BF16),
            pltpu.VMEM((hb, s, 2 * LANE), BF16),
            pltpu.VMEM((hb, tq, LANE), F32),
            pltpu.VMEM((hb, tq, 2 * LANE), F32),
        ],
        compiler_params=_cp("parallel", "parallel", "arbitrary"),
        name="mla_attention",
    )(qp, qp, kv, krp, g3)


def _mem_attn_kernel(q_ref, kv_ref, g_ref, o_ref):
    d = MEM_HEAD_DIM
    for h in range(MEM_HEADS):
        cols = slice(h * d, (h + 1) * d)
        k = kv_ref[:, cols]
        v = kv_ref[:, MEM_WIDTH + h * d:MEM_WIDTH + (h + 1) * d]
        s = _dot_t(q_ref[:, cols], k) * (d ** -0.5)
        m = jnp.max(s, axis=1, keepdims=True)
        e = jnp.exp(s - m)
        p = e / jnp.sum(e, axis=1, keepdims=True)
        o = _dot(p.astype(BF16), v) * _silu(g_ref[:, cols].astype(F32))
        o_ref[:, cols] = o.astype(o_ref.dtype)


def _mem_attention(g3, mq_blk, gate_blk, memkv3):
    b, s, _ = g3.shape
    ml = memkv3.shape[1]
    tq = _tile(s, 512)
    return pl.pallas_call(
        _mem_attn_kernel,
        out_shape=jax.ShapeDtypeStruct((b, s, MEM_WIDTH), BF16),
        grid=(b, s // tq),
        in_specs=[
            pl.BlockSpec((None, tq, MEM_WIDTH), lambda bb, i: (bb, i, mq_blk)),
            pl.BlockSpec((None, ml, 2 * MEM_WIDTH), lambda bb, i: (bb, 0, 0)),
            pl.BlockSpec((None, tq, MEM_WIDTH), lambda bb, i: (bb, i, gate_blk)),
        ],
        out_specs=pl.BlockSpec((None, tq, MEM_WIDTH), lambda bb, i: (bb, i, 0)),
        compiler_params=_cp("parallel", "parallel"),
        name="mem_attention",
    )(g3, memkv3, g3)


def _out_kernel(ys_ref, ym_ref, x_ref, w_ref, g_ref, *refs, keep_x):
    if keep_x:
        xo_ref, h_ref, xrow = refs
    else:
        h_ref, xrow = refs
    j = pl.program_id(1)
    ws = ys_ref.shape[1]
    tn = x_ref.shape[1]
    xn = x_ref[...] + _dot(ys_ref[...], w_ref[:ws, :]) + _dot(ym_ref[...], w_ref[ws:, :])
    xrow[:, pl.ds(pl.multiple_of(j * tn, tn), tn)] = xn
    if keep_x:
        xo_ref[...] = xn

    @pl.when(j == pl.num_programs(1) - 1)
    def _():
        x = xrow[...]
        y = x * lax.rsqrt(jnp.mean(x * x, axis=-1, keepdims=True) + EPS)
        h_ref[...] = (y * g_ref[...]).astype(h_ref.dtype)


def _out_proj(ys, ym, x2d, w_out, g_next, keep_x):
    m, ws = ys.shape
    wm = ym.shape[1]
    kk, n = _w_shape(w_out)
    tm = _tile(m, 1024)
    tn = _tile(n, 512)
    w_arr, w_spec = _layer_spec(w_out, (kk, tn), lambda i, j: (0, j))
    h_dtype = BF16 if keep_x else F32
    row_spec = pl.BlockSpec((tm, n), lambda i, j: (i, 0))
    tile_spec = pl.BlockSpec((tm, tn), lambda i, j: (i, j))
    out_shape = [jax.ShapeDtypeStruct((m, n), h_dtype)]
    out_specs = [row_spec]
    if keep_x:
        out_shape.insert(0, jax.ShapeDtypeStruct((m, n), F32))
        out_specs.insert(0, tile_spec)
    return pl.pallas_call(
        functools.partial(_out_kernel, keep_x=keep_x),
        out_shape=tuple(out_shape),
        grid=(m // tm, n // tn),
        in_specs=[
            pl.BlockSpec((tm, ws), lambda i, j: (i, 0)),
            pl.BlockSpec((tm, wm), lambda i, j: (i, 0)),
            tile_spec,
            w_spec,
            pl.BlockSpec((1, n), lambda i, j: (0, 0)),
        ],
        out_specs=tuple(out_specs),
        scratch_shapes=[pltpu.VMEM((tm, n), F32)],
        compiler_params=_cp("parallel", "arbitrary"),
        name="out_proj_norm",
    )(ys, ym, x2d, w_arr, g_next.reshape(1, n).astype(F32))


def _t5_bucket(rel):
    n = jnp.maximum(rel, 0)
    max_exact = N_BUCKETS // 2
    nf = jnp.maximum(n, 1).astype(F32)
    large = max_exact + (jnp.log(nf / max_exact) / math.log(MAX_DISTANCE / max_exact)
                         * (N_BUCKETS - max_exact)).astype(jnp.int32)
    large = jnp.minimum(large, N_BUCKETS - 1)
    return jnp.where(n < max_exact, n, large)


def _bias_kernel(rb_ref, bd_ref, bp_ref, tswa_ref, tdsa_ref):
    bd = bd_ref[...]
    bp = bp_ref[...]
    for h in range(N_BIAS_HEADS):
        g, j = divmod(h, N_BIAS_HEADS // B_KV_HEADS)

        def body(b, c, h=h):
            d, p = c
            v = rb_ref[b, h]
            return jnp.where(bd == b, v, d), jnp.where(bp == b, v, p)

        z = jnp.zeros((QB, QB), F32)
        d, p = lax.fori_loop(0, N_BUCKETS, body, (z, z))
        far = rb_ref[N_BUCKETS - 1, h]
        rows = pl.ds(j * QB, QB)
        t = lax.broadcasted_iota(jnp.int32, (QB, QB), 0)
        c = lax.broadcasted_iota(jnp.int32, (QB, QB), 1)
        tswa_ref[0, g, rows, :] = jnp.where(t >= c, d * LOG2E, NEG_INF)
        tswa_ref[1, g, rows, :] = jnp.where(t + QB - c < WINDOW, p * LOG2E, NEG_INF)
        tdsa_ref[0, g, rows, :] = z
        tdsa_ref[1, g, rows, :] = (d - far) * LOG2E
        tdsa_ref[2, g, rows, :] = (p - far) * LOG2E


def _bias_tables(rel_bias):
    t = jnp.arange(QB)[:, None]
    c = jnp.arange(QB)[None, :]
    bd = _t5_bucket(t - c).astype(jnp.int32)
    bp = _t5_bucket(t + QB - c).astype(jnp.int32)
    rows = (N_BIAS_HEADS // B_KV_HEADS) * QB
    return pl.pallas_call(
        _bias_kernel,
        out_shape=(jax.ShapeDtypeStruct((2, B_KV_HEADS, rows, QB), F32),
                   jax.ShapeDtypeStruct((3, B_KV_HEADS, rows, QB), F32)),
        in_specs=[pl.BlockSpec(memory_space=pltpu.SMEM),
                  pl.BlockSpec(memory_space=pltpu.VMEM),
                  pl.BlockSpec(memory_space=pltpu.VMEM)],
        out_specs=(pl.BlockSpec(memory_space=pltpu.VMEM), pl.BlockSpec(memory_space=pltpu.VMEM)),
        compiler_params=pltpu.CompilerParams(vmem_limit_bytes=VMEM_LIMIT),
        name="t5_bias_tables",
    )(rel_bias.astype(F32), bd, bp)


def _swa_kernel(sink_ref, q_ref, kp_ref, kc_ref, vp_ref, vc_ref, t_ref, g_ref, o_ref):
    i = pl.program_id(1)
    n_per = C_HEADS // C_KV_HEADS
    d = C_HEAD_DIM
    kb = jnp.concatenate([kp_ref[...], kc_ref[...]], axis=0)
    vb = jnp.concatenate([vp_ref[...], vc_ref[...]], axis=0)
    ones = jnp.ones((2 * QB, d), BF16)
    no_prev = jnp.where(i == 0, NEG_INF, 0.0).astype(F32)
    low = lax.broadcasted_iota(jnp.int32, (QB, LANE), 1) < d
    outs = []
    for g in range(C_KV_HEADS):
        kg = kb[:, g * d:(g + 1) * d]
        vg = vb[:, g * d:(g + 1) * d]
        vaug = (jnp.concatenate([vg, ones, ones, vg], axis=1), jnp.concatenate([ones, vg, vg, ones], axis=1))
        zk = jnp.zeros_like(kg)
        kpar = (jnp.concatenate([kg, zk], axis=1), jnp.concatenate([zk, kg], axis=1))
        for jp in range(n_per // 2):
            pair = []
            q2 = q_ref[:, (g * n_per + 2 * jp) * d:(g * n_per + 2 * jp + 2) * d]
            for par in range(2):
                j = 2 * jp + par
                h = g * n_per + j
                rows = pl.ds(j * QB, QB)
                s = _dot_t(q2, kpar[par])
                s = s + jnp.concatenate([t_ref[1, g, rows, :] + no_prev, t_ref[0, g, rows, :]], axis=1)
                sink = sink_ref[0, h] * LOG2E
                m = jnp.maximum(jnp.broadcast_to(jnp.max(s, axis=1, keepdims=True), (QB, LANE)), sink)
                e = jnp.exp2(s - _rep2(m))
                acc = _dot(e.astype(BF16), vaug[par])
                pair.append(acc[:, :LANE] / (acc[:, LANE:] + jnp.exp2(sink - m)))
            outs.append(jnp.where(low, pair[0], pair[1]))
    o = jnp.concatenate(outs, axis=1) * _silu(g_ref[...].astype(F32))
    o_ref[...] = o.astype(o_ref.dtype)


def _swa_attention(g3, sinks, tswa, q_blk, k_blk, v_blk):
    b, s, _ = g3.shape
    wq = C_HEADS * C_HEAD_DIM
    wk = C_KV_HEADS * C_HEAD_DIM
    prev = lambda i: jnp.maximum(i - 1, 0)
    return pl.pallas_call(
        _swa_kernel,
        out_shape=jax.ShapeDtypeStruct((b, s, wq), BF16),
        grid=(b, s // QB),
        in_specs=[
            pl.BlockSpec(memory_space=pltpu.SMEM),
            pl.BlockSpec((None, QB, wq), lambda bb, i: (bb, i, q_blk)),
            pl.BlockSpec((None, QB, wk), lambda bb, i: (bb, prev(i), k_blk)),
            pl.BlockSpec((None, QB, wk), lambda bb, i: (bb, i, k_blk)),
            pl.BlockSpec((None, QB, wk), lambda bb, i: (bb, prev(i), v_blk)),
            pl.BlockSpec((None, QB, wk), lambda bb, i: (bb, i, v_blk)),
            pl.BlockSpec(tswa.shape, lambda bb, i: (0, 0, 0, 0)),
            pl.BlockSpec((None, QB, wq), lambda bb, i: (bb, i, 0)),
        ],
        out_specs=pl.BlockSpec((None, QB, wq), lambda bb, i: (bb, i, 0)),
        compiler_params=_cp("parallel", "parallel"),
        name="swa_attention",
    )(sinks.reshape(1, -1).astype(F32), g3, g3, g3, g3, g3, tswa, g3)


def _ordered_to_f32(key):
    bits = jnp.where(key >= 0, key, key ^ jnp.int32(0x7FFFFFFF))
    return lax.bitcast_convert_type(bits, F32)


def _dsa_kernel(q_ref, k_ref, v_ref, iq_ref, ik_ref, iw_ref, t_ref, g_ref, o_ref,
                sc, madd, jcut, iqs, ks, vaug, m_scr, acc_scr, *, s_len, k_top):
    i = pl.program_id(1)
    n_per = B_HEADS // B_KV_HEADS
    d = B_HEAD_DIM

    @pl.when(i == 0)
    def _():
        ones = jnp.ones((s_len, d), BF16)
        zeros = jnp.zeros((s_len, d), BF16)
        for g in range(B_KV_HEADS):
            kg = k_ref[:, g * d:(g + 1) * d]
            vg = v_ref[:, g * d:(g + 1) * d]
            ks[g, 0] = jnp.concatenate([kg, zeros], axis=1)
            ks[g, 1] = jnp.concatenate([zeros, kg], axis=1)
            vaug[g, 0] = jnp.concatenate([vg, ones], axis=1)
            vaug[g, 1] = jnp.concatenate([ones, vg], axis=1)

    half = lax.shift_right_logical(i, 1)
    odd = i & 1
    n_t = half + 1
    sub = 8
    key_i = lax.broadcasted_iota(jnp.int32, (KT, QB), 0)
    qry_i = i * QB + lax.broadcasted_iota(jnp.int32, (KT, QB), 1)

    for hp in range(IDX_HEADS // 2):
        for par in range(2):
            h = 2 * hp + par
            iqs[hp, par * QB:(par + 1) * QB, :] = iq_ref[:, h * IDX_DIM:(h + 1) * IDX_DIM]
    w = iw_ref[...] * (IDX_DIM ** -0.5 * IDX_HEADS ** -0.5)
    wrows = [w[h:h + 1, :] for h in range(IDX_HEADS)]

    def score_tile(kt, c):
        k0 = pl.multiple_of(kt * KT, KT)
        ikt = ik_ref[pl.ds(k0, KT), :]
        acc = jnp.zeros((KT, QB), F32)
        for hp in range(IDX_HEADS // 2):
            dots = _dot_t(ikt, iqs[hp])
            acc = acc + jnp.maximum(dots[:, :QB], 0.0) * wrows[2 * hp]
            acc = acc + jnp.maximum(dots[:, QB:], 0.0) * wrows[2 * hp + 1]
        sc[pl.ds(k0, KT), :] = jnp.where(qry_i >= k0 + key_i, acc, NEG_INF)
        return c

    lax.fori_loop(0, n_t, score_tile, 0)

    tail = (s_len - n_t * KT).astype(F32)
    int_min = jnp.int32(-2 ** 31)

    grp = (4, KT // (4 * sub), sub, QB)
    key_g = lax.broadcasted_iota(jnp.int32, grp, 0) * (KT // 4) + lax.broadcasted_iota(jnp.int32, grp, 1) * sub \
        + lax.broadcasted_iota(jnp.int32, grp, 2)

    def count(pred):
        def count_tile(kt, c):
            k0 = pl.multiple_of(kt * KT, KT)
            part = jnp.sum(jnp.where(pred(sc[pl.ds(k0, KT), :].reshape(grp), k0 + key_g), 1.0, 0.0), axis=1)
            return c + ((part[0] + part[1]) + (part[2] + part[3]))

        cnt = lax.fori_loop(0, n_t, count_tile, jnp.zeros((sub, QB), F32))
        return jnp.broadcast_to(jnp.sum(cnt, axis=0, keepdims=True), (sub, QB))

    def bisect(b, prefix):
        cand = prefix + lax.shift_left(jnp.int32(1), 31 - b)
        cand_f = _ordered_to_f32(cand)
        total = count(lambda t, _: t >= cand_f[None, None]) + jnp.where(cand_f <= NEG_INF, tail, 0.0)
        return jnp.where(total >= k_top, cand, prefix)

    thr8 = _ordered_to_f32(lax.fori_loop(0, 32, bisect, jnp.full((sub, QB), int_min, jnp.int32)))
    thr = thr8[:1]

    n_ge = count(lambda t, _: t >= thr8[None, None])
    tied = (n_ge > k_top) & (thr8 > NEG_INF)
    jcut[...] = jnp.full((sub, QB), s_len, jnp.int32)

    @pl.when(jnp.max(jnp.where(tied, 1.0, 0.0)) > 0.0)
    def _():
        need = k_top - count(lambda t, _: t > thr8[None, None])
        n_bits = (s_len - 1).bit_length()

        def index_bisect(b, j):
            cand = j + lax.shift_left(jnp.int32(1), n_bits - 1 - b)
            below = count(lambda t, idx: (t == thr8[None, None]) & (idx < cand[None, None]))
            return jnp.where(below < need, cand, j)

        j = lax.fori_loop(0, n_bits, index_bisect, jnp.zeros((sub, QB), jnp.int32))
        jcut[...] = jnp.where(tied, j, s_len)

    def mask_tile(kt, c):
        k0 = pl.multiple_of(kt * KT, KT)
        t = sc[pl.ds(k0, KT), :]
        sel = (t > thr) | ((t == thr) & (k0 + key_i <= jcut[:1]))
        madd[:, pl.ds(k0, KT)] = jnp.where(sel & (qry_i >= k0 + key_i), 0.0, NEG_INF).T
        return c

    lax.fori_loop(0, n_t, mask_tile, 0)

    near0 = jnp.where(odd == 1, half, jnp.maximum(half - 1, 0))
    m_scr[...] = jnp.full(m_scr.shape, -jnp.inf, F32)
    acc_scr[...] = jnp.zeros(acc_scr.shape, F32)

    def step(kt, near):
        k0 = pl.multiple_of(kt * KT, KT)
        mt = madd[:, pl.ds(k0, KT)]
        if near:
            li = jnp.where(odd == 1, 2, jnp.where(kt == half, 1, 0))
            ri = jnp.where(odd == 1, 1, jnp.where(kt == half, 0, 2))
        for g in range(B_KV_HEADS):
            kg = (ks[g, 0, pl.ds(k0, KT), :], ks[g, 1, pl.ds(k0, KT), :])
            vg = (vaug[g, 0, pl.ds(k0, KT), :], vaug[g, 1, pl.ds(k0, KT), :])
            for j in range(n_per):
                h = g * n_per + j
                q2 = q_ref[:, (h // 2) * LANE:(h // 2 + 1) * LANE]
                s = _dot_t(q2, kg[h % 2]) + mt
                if near:
                    rows = pl.ds(j * QB, QB)
                    s = s + jnp.concatenate([t_ref[li, g, rows, :], t_ref[ri, g, rows, :]], axis=1)
                m_prev = m_scr[h]
                m_new = jnp.maximum(m_prev, jnp.max(s, axis=1, keepdims=True))
                alpha = jnp.exp2(m_prev - m_new)
                p = jnp.exp2(s - _rep2(m_new))
                acc_scr[h] = alpha * acc_scr[h] + _dot(p.astype(BF16), vg[h % 2])
                m_scr[h] = m_new

    def far_body(kt, c):
        step(kt, False)
        return c

    def near_body(kt, c):
        step(kt, True)
        return c

    lax.fori_loop(0, near0, far_body, 0)
    lax.fori_loop(near0, n_t, near_body, 0)
    low = lax.broadcasted_iota(jnp.int32, (QB, LANE), 1) < d
    outs = []
    for hp in range(B_HEADS // 2):
        acc_e = acc_scr[2 * hp]
        acc_o = acc_scr[2 * hp + 1]
        num = jnp.where(low, acc_e, acc_o)
        den = pltpu.roll(jnp.where(low, acc_o, acc_e), d, 1)
        outs.append(num / den)
    o = jnp.concatenate(outs, axis=1) * _silu(g_ref[...].astype(F32))
    o_ref[...] = o.astype(o_ref.dtype)


def _dsa_attention(g3, ik3, iwt3, tdsa, q_blk, k_blk, v_blk, iq_blk):
    b, s, _ = g3.shape
    k_top = min(IDX_TOPK_MAX, s // 4)
    wq = B_HEADS * B_HEAD_DIM
    wk = B_KV_HEADS * B_HEAD_DIM
    wi = IDX_HEADS * IDX_DIM
    return pl.pallas_call(
        functools.partial(_dsa_kernel, s_len=s, k_top=k_top),
        out_shape=jax.ShapeDtypeStruct((b, s, wq), BF16),
        grid=(b, s // QB),
        in_specs=[
            pl.BlockSpec((None, QB, wq), lambda bb, i: (bb, i, q_blk)),
            pl.BlockSpec((None, s, wk), lambda bb, i: (bb, 0, k_blk)),
            pl.BlockSpec((None, s, wk), lambda bb, i: (bb, 0, v_blk)),
            pl.BlockSpec((None, QB, wi), lambda bb, i: (bb, i, iq_blk)),
            pl.BlockSpec((None, s, IDX_DIM), lambda bb, i: (bb, 0, 0)),
            pl.BlockSpec((None, IDX_HEADS, QB), lambda bb, i: (bb, 0, i)),
            pl.BlockSpec(tdsa.shape, lambda bb, i: (0, 0, 0, 0)),
            pl.BlockSpec((None, QB, wq), lambda bb, i: (bb, i, 0)),
        ],
        out_specs=pl.BlockSpec((None, QB, wq), lambda bb, i: (bb, i, 0)),
        scratch_shapes=[
            pltpu.VMEM((s, QB), F32),
            pltpu.VMEM((QB, s), F32),
            pltpu.VMEM((8, QB), jnp.int32),
            pltpu.VMEM((IDX_HEADS // 2, 2 * QB, IDX_DIM), BF16),
            pltpu.VMEM((B_KV_HEADS, 2, s, 2 * B_HEAD_DIM), BF16),
            pltpu.VMEM((B_KV_HEADS, 2, s, 2 * B_HEAD_DIM), BF16),
            pltpu.VMEM((B_HEADS, QB, LANE), F32),
            pltpu.VMEM((B_HEADS, QB, 2 * B_HEAD_DIM), F32),
        ],
        compiler_params=_cp("parallel", "arbitrary"),
        name="dsa_attention",
    )(g3, g3, g3, g3, ik3, iwt3, tdsa, g3)


def _mla_layer(h, w_in_t, q_norm, w_uq, kv_norm, w_ukv, rope1, rope2, b, s):
    o_ckv = A_Q_LORA + A_KV_LORA
    o_mq = o_ckv + A_ROPE
    o_gate = o_mq + MEM_WIDTH
    half = A_ROPE // 2
    w_f, w_g = _repack(w_in_t, (
        _pieces((0, o_ckv), (o_ckv, A_ROPE), (o_ckv + half, half), (o_ckv, half)),
        _pieces((o_gate, A_HEADS * A_V + MEM_WIDTH), (o_mq, MEM_WIDTH)),
    ), by_rows=True)
    cqkv = _matmul(h, w_f, F32, w_rows_are_outputs=True)
    g2 = _matmul(h, w_g, BF16, w_rows_are_outputs=True)

    hd = A_NOPE + A_ROPE
    (w_uq2,) = _repack(w_uq, (
        _pieces(*[(hh * hd, A_NOPE, hd ** -0.5 * LOG2E) for hh in range(A_HEADS)],
                *[(hh * hd + A_NOPE, A_ROPE, hd ** -0.5 * LOG2E) for hh in range(A_HEADS)]),
    ))
    qp = _uq_proj(cqkv, q_norm, w_uq2, rope2[0], rope2[1], s)
    kv, krp = _ukv_proj(cqkv, kv_norm, w_ukv, rope1[0], rope1[1], s)
    self_out = _mla_attention(qp.reshape(b, s, -1), kv.reshape(b, s, -1), krp.reshape(b, s, -1),
                              g2.reshape(b, s, -1), b, s)
    return self_out.reshape(b * s, -1), g2


def _dsa_layer(h, w_in_t, tdsa, b, s):
    wq = B_HEADS * B_HEAD_DIM
    wk = B_KV_HEADS * B_HEAD_DIM
    wi = IDX_HEADS * IDX_DIM
    o_k = wq
    o_v = o_k + wk
    o_iq = o_v + wk
    o_ik = o_iq + wi
    o_iw = o_ik + IDX_DIM
    o_mq = o_iw + IDX_HEADS
    o_gate = o_mq + MEM_WIDTH
    layout = _pieces((o_gate, wq + MEM_WIDTH), (o_mq, MEM_WIDTH), (0, wq, B_HEAD_DIM ** -0.5 * LOG2E),
                     (o_iq, wi), (o_k, wk), (o_v, wk))
    w_g, w_i = _repack(w_in_t, (layout, _pieces((o_ik, IDX_DIM + IDX_HEADS))), by_rows=True)
    g2 = _matmul(h, w_g, BF16, w_rows_are_outputs=True)
    ii = _matmul(h, w_i, F32, w_rows_are_outputs=True)
    ik3 = ii[:, :IDX_DIM].astype(BF16).reshape(b, s, IDX_DIM)
    iwt3 = ii[:, IDX_DIM:].reshape(b, s, IDX_HEADS).transpose(0, 2, 1)
    _, _, q_at, iq_at, k_at, v_at = [p[2] for p in layout]
    self_out = _dsa_attention(g2.reshape(b, s, -1), ik3, iwt3, tdsa,
                              q_blk=q_at // wq, k_blk=k_at // wk, v_blk=v_at // wk, iq_blk=iq_at // wi)
    return self_out.reshape(b * s, -1), g2


def _swa_layer(h, w_in, sinks, tswa, b, s):
    wq = C_HEADS * C_HEAD_DIM
    wk = C_KV_HEADS * C_HEAD_DIM
    o_k = wq
    o_v = o_k + wk
    o_mq = o_v + wk
    o_gate = o_mq + MEM_WIDTH
    layout = _pieces((o_gate, wq + MEM_WIDTH), (o_mq, MEM_WIDTH), (0, wq, C_HEAD_DIM ** -0.5 * LOG2E),
                     (o_k, wk), (o_v, wk))
    (w_g,) = _repack(w_in, (layout,))
    g2 = _matmul(h, w_g, BF16)
    _, _, q_at, k_at, v_at = [p[2] for p in layout]
    self_out = _swa_attention(g2.reshape(b, s, -1), sinks, tswa,
                              q_blk=q_at // wq, k_blk=k_at // wk, v_blk=v_at // wk)
    return self_out.reshape(b * s, -1), g2


def kernel(x, mem, norm_in, final_norm, mem_norm, rel_bias, w_in_a, a_q_norm, w_uq, a_kv_norm, w_ukv,
           w_in_b, w_in_c, c_sinks, w_mem_kv, w_out):
    b, s, d = x.shape
    depth = norm_in.shape[0]
    ml = mem.shape[1]
    xf = x.reshape(b * s, d)
    mem_n = _rmsnorm(mem.reshape(b * ml, d), mem_norm, BF16)
    rope1, rope2 = _rope_tables(s)
    tswa, tdsa = _bias_tables(rel_bias)
    h = _rmsnorm(xf, norm_in[0], BF16)
    w_ukv_b, w_mem_b, w_out_b = w_ukv.astype(BF16), w_mem_kv.astype(BF16), w_out.astype(BF16)
    w_in_a_t, w_in_b_t = jnp.swapaxes(w_in_a, 1, 2), jnp.swapaxes(w_in_b, 1, 2)
    for i in range(depth):
        kind, j = i % N_MIXERS, i // N_MIXERS
        if kind == 0:
            ys, g2 = _mla_layer(h, (w_in_a_t, j), a_q_norm[j], (w_uq, j), a_kv_norm[j], (w_ukv_b, j),
                                rope1, rope2, b, s)
        elif kind == 1:
            ys, g2 = _dsa_layer(h, (w_in_b_t, j), tdsa, b, s)
        else:
            ys, g2 = _swa_layer(h, (w_in_c, j), c_sinks[j], tswa, b, s)
        memkv = _matmul(mem_n, (w_mem_b, i), BF16)
        self_w = w_out.shape[1] - MEM_WIDTH
        ym = _mem_attention(g2.reshape(b, s, -1), (self_w + MEM_WIDTH) // MEM_WIDTH, self_w // MEM_WIDTH,
                            memkv.reshape(b, ml, -1)).reshape(b * s, -1)
        last = i == depth - 1
        res = _out_proj(ys, ym, xf, (w_out_b, i), final_norm if last else norm_in[i + 1], not last)
        if last:
            return res[0].reshape(b, s, d)
        xf, h = res
```

```python
import functools
import math

import jax
import jax.numpy as jnp
from jax import lax
from jax.experimental import pallas as pl
from jax.experimental.pallas import tpu as pltpu

F32 = jnp.float32
BF16 = jnp.bfloat16

EPS = 1e-6
NEG_INF = -1e30
N_MIXERS = 3

N_BUCKETS = 32
MAX_DISTANCE = 128
N_BIAS_HEADS = 32

MEM_HEADS = 4
MEM_HEAD_DIM = 256
MEM_WIDTH = MEM_HEADS * MEM_HEAD_DIM

A_HEADS = 16
A_Q_LORA = 1536
A_KV_LORA = 512
A_NOPE = 128
A_ROPE = 64
A_V = 128
ROPE_THETA = 10000.0

B_HEADS = 32
B_KV_HEADS = 4
B_HEAD_DIM = 64
IDX_HEADS = 16
IDX_DIM = 64
IDX_TOPK_MAX = 256

C_HEADS = 32
C_KV_HEADS = 4
C_HEAD_DIM = 64
WINDOW = 128

LANE = 128
MXU_N = 256
QB = 128
KT = MXU_N
VMEM_LIMIT = 56 * 1024 * 1024
LOG2E = math.log2(math.e)
MM_MAX_TN = 13 * MXU_N


def _cp(*sem):
    return pltpu.CompilerParams(dimension_semantics=sem, vmem_limit_bytes=VMEM_LIMIT)


def _tile(n, pref):
    t = min(n, pref)
    assert n % t == 0, (n, pref)
    return t


def _dot(a, b):
    return jnp.dot(a, b, preferred_element_type=F32)


def _dot_t(a, b):
    return lax.dot_general(a, b, (((1,), (1,)), ((), ())), preferred_element_type=F32)


def _rms_kernel(x_ref, g_ref, o_ref):
    x = x_ref[...].astype(F32)
    y = x * lax.rsqrt(jnp.mean(x * x, axis=-1, keepdims=True) + EPS)
    o_ref[...] = (y * g_ref[...]).astype(o_ref.dtype)


def _rmsnorm(x2d, g, out_dtype):
    m, d = x2d.shape
    tm = _tile(m, 512)
    return pl.pallas_call(
        _rms_kernel,
        out_shape=jax.ShapeDtypeStruct((m, d), out_dtype),
        grid=(m // tm,),
        in_specs=[pl.BlockSpec((tm, d), lambda i: (i, 0)), pl.BlockSpec((1, d), lambda i: (0, 0))],
        out_specs=pl.BlockSpec((tm, d), lambda i: (i, 0)),
        compiler_params=_cp("parallel"),
        name="rmsnorm",
    )(x2d, g.reshape(1, d).astype(F32))


def _mm_kernel(a_ref, w_ref, o_ref, *, w_rows_are_outputs):
    dot = _dot_t if w_rows_are_outputs else _dot
    o_ref[...] = dot(a_ref[...], w_ref[...]).astype(o_ref.dtype)


def _layer_spec(w, block, index_map):
    if isinstance(w, tuple):
        stack, layer = w
        return stack, pl.BlockSpec((None,) + block, lambda *g: (layer,) + index_map(*g))
    return w, pl.BlockSpec(block, index_map)


def _w_shape(w):
    return w[0].shape[1:] if isinstance(w, tuple) else w.shape


def _matmul(a, w, out_dtype, w_rows_are_outputs=False):
    m, k = a.shape
    n = _w_shape(w)[0 if w_rows_are_outputs else 1]
    tm = _tile(m, 1024)
    tn = max([t for t in range(MXU_N, min(n, MM_MAX_TN) + 1, MXU_N) if n % t == 0]
             or [t for t in range(LANE, min(n, MM_MAX_TN) + 1, LANE) if n % t == 0] or [n])
    if w_rows_are_outputs:
        w_arr, w_spec = _layer_spec(w, (tn, k), lambda i, j: (j, 0))
    else:
        w_arr, w_spec = _layer_spec(w, (k, tn), lambda i, j: (0, j))
    return pl.pallas_call(
        functools.partial(_mm_kernel, w_rows_are_outputs=w_rows_are_outputs),
        out_shape=jax.ShapeDtypeStruct((m, n), out_dtype),
        grid=(m // tm, n // tn),
        in_specs=[pl.BlockSpec((tm, k), lambda i, j: (i, 0)), w_spec],
        out_specs=pl.BlockSpec((tm, tn), lambda i, j: (i, j)),
        compiler_params=_cp("parallel", "arbitrary"),
        name="matmul",
    )(a, w_arr)


def _repack_kernel(w_ref, *o_refs, groups, by_rows):
    for o_ref, pieces in zip(o_refs, groups):
        for src, width, dst, scale in pieces:
            v = w_ref[src:src + width, :] if by_rows else w_ref[:, src:src + width]
            if scale != 1.0:
                v = v * scale
            if by_rows:
                o_ref[dst:dst + width, :] = v.astype(o_ref.dtype)
            else:
                o_ref[:, dst:dst + width] = v.astype(o_ref.dtype)


def _repack(w, groups, by_rows=False):
    shape = _w_shape(w)
    k = shape[1 if by_rows else 0]
    tk = _tile(k, 256)
    widths = [sum(p[1] for p in pieces) for pieces in groups]
    if by_rows:
        w_arr, w_spec = _layer_spec(w, (shape[0], tk), lambda i: (0, i))
        out_shape = tuple(jax.ShapeDtypeStruct((wd, k), BF16) for wd in widths)
        out_specs = tuple(pl.BlockSpec((wd, tk), lambda i: (0, i)) for wd in widths)
    else:
        w_arr, w_spec = _layer_spec(w, (tk, shape[1]), lambda i: (i, 0))
        out_shape = tuple(jax.ShapeDtypeStruct((k, wd), BF16) for wd in widths)
        out_specs = tuple(pl.BlockSpec((tk, wd), lambda i: (i, 0)) for wd in widths)
    return pl.pallas_call(
        functools.partial(_repack_kernel, groups=groups, by_rows=by_rows),
        out_shape=out_shape,
        grid=(k // tk,),
        in_specs=[w_spec],
        out_specs=out_specs,
        compiler_params=_cp("parallel"),
        name="weight_repack",
    )(w_arr)


def _pieces(*src_width_scale):
    out, dst = [], 0
    for p in src_width_scale:
        src, width = p[0], p[1]
        out.append((src, width, dst, p[2] if len(p) > 2 else 1.0))
        dst += width
    return tuple(out)


def _rope_tables(s):
    d = A_ROPE
    inv = 1.0 / (ROPE_THETA ** (jnp.arange(0, d, 2, dtype=F32) / d))
    ang = jnp.arange(s, dtype=F32)[:, None] * inv[None, :]
    cos, sin = jnp.cos(ang), jnp.sin(ang)
    z = jnp.zeros((s, LANE - d), F32)
    one = (jnp.concatenate([cos, cos, z], axis=1), jnp.concatenate([-sin, sin, z], axis=1))
    two = (jnp.concatenate([cos] * 4, axis=1), jnp.concatenate([-sin, sin] * 2, axis=1))
    return one, two


def _uq_kernel(cq_ref, g_ref, w_ref, cos_ref, sin_ref, o_ref, *, n_nope):
    x = cq_ref[...]
    y = x * lax.rsqrt(jnp.mean(x * x, axis=-1, keepdims=True) + EPS)
    h = (y * g_ref[...]).astype(BF16)
    n = w_ref.shape[1]
    cw = 512
    for c in range(0, n_nope, cw):
        o_ref[:, c:c + cw] = _dot(h, w_ref[:, c:c + cw]).astype(BF16)
    cos = cos_ref[...]
    sin = sin_ref[...]
    lane = lax.broadcasted_iota(jnp.int32, (x.shape[0], LANE), 1)
    first_half = (lane & (A_ROPE - 1)) < A_ROPE // 2
    low = lane < A_ROPE
    for c in range(n_nope, n, cw):
        y = _dot(h, w_ref[:, c:c + cw])
        parts = []
        for k in range(cw // LANE):
            slab = y[:, k * LANE:(k + 1) * LANE]
            partner = jnp.where(first_half, pltpu.roll(slab, LANE - A_ROPE // 2, 1), pltpu.roll(slab, A_ROPE // 2, 1))
            r = slab * cos + partner * sin
            parts.append(jnp.where(low, r, 0.0))
            parts.append(jnp.where(low, pltpu.roll(r, A_ROPE, 1), 0.0))
        d0 = n_nope + 2 * (c - n_nope)
        o_ref[:, d0:d0 + 2 * cw] = jnp.concatenate(parts, axis=1).astype(BF16)


def _uq_proj(cqkv, q_norm, w_uq2, cos_t, sin_t, s):
    m = cqkv.shape[0]
    n = w_uq2.shape[1]
    n_nope = A_HEADS * A_NOPE
    n_out = n_nope + 2 * (n - n_nope)
    tm = _tile(s, 1024)
    nblk = s // tm
    return pl.pallas_call(
        functools.partial(_uq_kernel, n_nope=n_nope),
        out_shape=jax.ShapeDtypeStruct((m, n_out), BF16),
        grid=(m // tm,),
        in_specs=[
            pl.BlockSpec((tm, A_Q_LORA), lambda i: (i, 0)),
            pl.BlockSpec((1, A_Q_LORA), lambda i: (0, 0)),
            pl.BlockSpec((A_Q_LORA, n), lambda i: (0, 0)),
            pl.BlockSpec((tm, LANE), lambda i: (i % nblk, 0)),
            pl.BlockSpec((tm, LANE), lambda i: (i % nblk, 0)),
        ],
        out_specs=pl.BlockSpec((tm, n_out), lambda i: (i, 0)),
        compiler_params=_cp("parallel"),
        name="mla_q_up",
    )(cqkv, q_norm.reshape(1, -1).astype(F32), w_uq2, cos_t, sin_t)


def _ukv_kernel(ckv_ref, kr_ref, g_ref, w_ref, cos_ref, sin_ref, kv_ref, krp_ref):
    x = ckv_ref[...]
    y = x * lax.rsqrt(jnp.mean(x * x, axis=-1, keepdims=True) + EPS)
    h = (y * g_ref[...]).astype(BF16)
    n = w_ref.shape[1]
    cw = 512
    for c in range(0, n, cw):
        kv_ref[:, c:c + cw] = _dot(h, w_ref[:, c:c + cw]).astype(BF16)
    slab = kr_ref[...]
    krp_ref[...] = (slab * cos_ref[...] + pltpu.roll(slab, LANE // 2, 1) * sin_ref[...]).astype(BF16)


def _ukv_proj(cqkv, kv_norm, w_ukv, cos_t, sin_t, s):
    m = cqkv.shape[0]
    n = _w_shape(w_ukv)[1]
    tm = _tile(s, 1024)
    nblk = s // tm
    cblk = A_Q_LORA // A_KV_LORA
    w_arr, w_spec = _layer_spec(w_ukv, (A_KV_LORA, n), lambda i: (0, 0))
    return pl.pallas_call(
        _ukv_kernel,
        out_shape=(jax.ShapeDtypeStruct((m, n), BF16), jax.ShapeDtypeStruct((m, LANE), BF16)),
        grid=(m // tm,),
        in_specs=[
            pl.BlockSpec((tm, A_KV_LORA), lambda i: (i, cblk)),
            pl.BlockSpec((tm, LANE), lambda i: (i, (A_Q_LORA + A_KV_LORA) // LANE)),
            pl.BlockSpec((1, A_KV_LORA), lambda i: (0, 0)),
            w_spec,
            pl.BlockSpec((tm, LANE), lambda i: (i % nblk, 0)),
            pl.BlockSpec((tm, LANE), lambda i: (i % nblk, 0)),
        ],
        out_specs=(pl.BlockSpec((tm, n), lambda i: (i, 0)), pl.BlockSpec((tm, LANE), lambda i: (i, 0))),
        compiler_params=_cp("parallel"),
        name="mla_kv_up",
    )(cqkv, cqkv, kv_norm.reshape(1, -1).astype(F32), w_arr, cos_t, sin_t)


def _rep2(x):
    return jnp.concatenate([x, x], axis=1)


def _silu(g):
    return g * (1.0 / (1.0 + jnp.exp(-g)))


def _mla_attn_kernel(qn_ref, qr_ref, kv_ref, kr_ref, g_ref, o_ref, kcat, vaug, m_scr, acc_scr, *, tq, hb):
    i = pl.program_id(2)

    @pl.when(i == 0)
    def _():
        ones = jnp.ones((kv_ref.shape[0], LANE), BF16)
        for hh in range(hb):
            kcat[hh, :, :LANE] = kv_ref[:, 2 * hh * LANE:(2 * hh + 1) * LANE]
            kcat[hh, :, LANE:] = kr_ref[...]
            vaug[hh, :, :LANE] = kv_ref[:, (2 * hh + 1) * LANE:(2 * hh + 2) * LANE]
            vaug[hh, :, LANE:] = ones

    m_scr[...] = jnp.full(m_scr.shape, -jnp.inf, F32)
    acc_scr[...] = jnp.zeros(acc_scr.shape, F32)
    qs = [jnp.concatenate([qn_ref[:, hh * LANE:(hh + 1) * LANE], qr_ref[:, hh * LANE:(hh + 1) * LANE]], axis=1)
          for hh in range(hb)]

    n_wide = lax.shift_right_logical(i, 1)

    def step(k0, tk, diag):
        for hh in range(hb):
            s = _dot_t(qs[hh], kcat[hh, pl.ds(k0, tk), :])
            if diag:
                row = lax.broadcasted_iota(jnp.int32, s.shape, 0) + (tk - tq)
                col = lax.broadcasted_iota(jnp.int32, s.shape, 1)
                s = jnp.where(row >= col, s, NEG_INF)
            m_prev = m_scr[hh]
            m_new = jnp.maximum(m_prev, jnp.max(s, axis=1, keepdims=True))
            alpha = jnp.exp2(m_prev - m_new)
            p = jnp.exp2(s - jnp.concatenate([m_new] * (tk // LANE), axis=1))
            acc_scr[hh] = _rep2(alpha) * acc_scr[hh] + _dot(p.astype(BF16), vaug[hh, pl.ds(k0, tk), :])
            m_scr[hh] = m_new

    def body(j, c):
        step(pl.multiple_of(j * 2 * tq, 2 * tq), 2 * tq, False)
        return c

    lax.fori_loop(0, n_wide, body, 0)

    @pl.when((i & 1) == 1)
    def _():
        step(pl.multiple_of((i - 1) * tq, 2 * tq), 2 * tq, True)

    @pl.when((i & 1) == 0)
    def _():
        step(pl.multiple_of(i * tq, 2 * tq), tq, True)
    for hh in range(hb):
        acc = acc_scr[hh]
        cols = slice(hh * LANE, (hh + 1) * LANE)
        gate = _silu(g_ref[:, cols].astype(F32))
        o_ref[:, cols] = (acc[:, :LANE] / acc[:, LANE:] * gate).astype(o_ref.dtype)


def _mla_attention(qp, kv, krp, g3, b, s):
    tq = _tile(s, 256)
    hn = A_HEADS
    hb = 8
    ng = hn // hb
    return pl.pallas_call(
        functools.partial(_mla_attn_kernel, tq=tq, hb=hb),
        out_shape=jax.ShapeDtypeStruct((b, s, hn * A_V), BF16),
        grid=(b, ng, s // tq),
        in_specs=[
            pl.BlockSpec((None, tq, hb * LANE), lambda bb, h, i: (bb, i, h)),
            pl.BlockSpec((None, tq, hb * LANE), lambda bb, h, i: (bb, i, ng + h)),
            pl.BlockSpec((None, s, 2 * hb * LANE), lambda bb, h, i: (bb, 0, h)),
            pl.BlockSpec((None, s, LANE), lambda bb, h, i: (bb, 0, 0)),
            pl.BlockSpec((None, tq, hb * LANE), lambda bb, h, i: (bb, i, h)),
        ],
        out_specs=pl.BlockSpec((None, tq, hb * LANE), lambda bb, h, i: (bb, i, h)),
        scratch_shapes=[
            pltpu.VMEM((hb, s, 2 * LANE), BF16),
            pltpu.VMEM((hb, s, 2 * LANE), BF16),
            pltpu.VMEM((hb, tq, LANE), F32),
            pltpu.VMEM((hb, tq, 2 * LANE), F32),
        ],
        compiler_params=_cp("parallel", "parallel", "arbitrary"),
        name="mla_attention",
    )(qp, qp, kv, krp, g3)


def _mem_attn_kernel(q_ref, kv_ref, g_ref, o_ref):
    d = MEM_HEAD_DIM
    for h in range(MEM_HEADS):
        cols = slice(h * d, (h + 1) * d)
        k = kv_ref[:, cols]
        v = kv_ref[:, MEM_WIDTH + h * d:MEM_WIDTH + (h + 1) * d]
        s = _dot_t(q_ref[:, cols], k) * (d ** -0.5)
        m = jnp.max(s, axis=1, keepdims=True)
        e = jnp.exp(s - m)
        p = e / jnp.sum(e, axis=1, keepdims=True)
        o = _dot(p.astype(BF16), v) * _silu(g_ref[:, cols].astype(F32))
        o_ref[:, cols] = o.astype(o_ref.dtype)


def _mem_attention(g3, mq_blk, gate_blk, memkv3):
    b, s, _ = g3.shape
    ml = memkv3.shape[1]
    tq = _tile(s, 512)
    return pl.pallas_call(
        _mem_attn_kernel,
        out_shape=jax.ShapeDtypeStruct((b, s, MEM_WIDTH), BF16),
        grid=(b, s // tq),
        in_specs=[
            pl.BlockSpec((None, tq, MEM_WIDTH), lambda bb, i: (bb, i, mq_blk)),
            pl.BlockSpec((None, ml, 2 * MEM_WIDTH), lambda bb, i: (bb, 0, 0)),
            pl.BlockSpec((None, tq, MEM_WIDTH), lambda bb, i: (bb, i, gate_blk)),
        ],
        out_specs=pl.BlockSpec((None, tq, MEM_WIDTH), lambda bb, i: (bb, i, 0)),
        compiler_params=_cp("parallel", "parallel"),
        name="mem_attention",
    )(g3, memkv3, g3)


def _out_kernel(ys_ref, ym_ref, x_ref, w_ref, g_ref, *refs, keep_x):
    if keep_x:
        xo_ref, h_ref, xrow = refs
    else:
        h_ref, xrow = refs
    j = pl.program_id(1)
    ws = ys_ref.shape[1]
    tn = x_ref.shape[1]
    xn = x_ref[...] + _dot(ys_ref[...], w_ref[:ws, :]) + _dot(ym_ref[...], w_ref[ws:, :])
    xrow[:, pl.ds(pl.multiple_of(j * tn, tn), tn)] = xn
    if keep_x:
        xo_ref[...] = xn

    @pl.when(j == pl.num_programs(1) - 1)
    def _():
        x = xrow[...]
        y = x * lax.rsqrt(jnp.mean(x * x, axis=-1, keepdims=True) + EPS)
        h_ref[...] = (y * g_ref[...]).astype(h_ref.dtype)


def _out_proj(ys, ym, x2d, w_out, g_next, keep_x):
    m, ws = ys.shape
    wm = ym.shape[1]
    kk, n = _w_shape(w_out)
    tm = _tile(m, 1024)
    tn = _tile(n, 512)
    w_arr, w_spec = _layer_spec(w_out, (kk, tn), lambda i, j: (0, j))
    h_dtype = BF16 if keep_x else F32
    row_spec = pl.BlockSpec((tm, n), lambda i, j: (i, 0))
    tile_spec = pl.BlockSpec((tm, tn), lambda i, j: (i, j))
    out_shape = [jax.ShapeDtypeStruct((m, n), h_dtype)]
    out_specs = [row_spec]
    if keep_x:
        out_shape.insert(0, jax.ShapeDtypeStruct((m, n), F32))
        out_specs.insert(0, tile_spec)
    return pl.pallas_call(
        functools.partial(_out_kernel, keep_x=keep_x),
        out_shape=tuple(out_shape),
        grid=(m // tm, n // tn),
        in_specs=[
            pl.BlockSpec((tm, ws), lambda i, j: (i, 0)),
            pl.BlockSpec((tm, wm), lambda i, j: (i, 0)),
            tile_spec,
            w_spec,
            pl.BlockSpec((1, n), lambda i, j: (0, 0)),
        ],
        out_specs=tuple(out_specs),
        scratch_shapes=[pltpu.VMEM((tm, n), F32)],
        compiler_params=_cp("parallel", "arbitrary"),
        name="out_proj_norm",
    )(ys, ym, x2d, w_arr, g_next.reshape(1, n).astype(F32))


def _t5_bucket(rel):
    n = jnp.maximum(rel, 0)
    max_exact = N_BUCKETS // 2
    nf = jnp.maximum(n, 1).astype(F32)
    large = max_exact + (jnp.log(nf / max_exact) / math.log(MAX_DISTANCE / max_exact)
                         * (N_BUCKETS - max_exact)).astype(jnp.int32)
    large = jnp.minimum(large, N_BUCKETS - 1)
    return jnp.where(n < max_exact, n, large)


def _bias_kernel(rb_ref, bd_ref, bp_ref, tswa_ref, tdsa_ref):
    bd = bd_ref[...]
    bp = bp_ref[...]
    for h in range(N_BIAS_HEADS):
        g, j = divmod(h, N_BIAS_HEADS // B_KV_HEADS)

        def body(b, c, h=h):
            d, p = c
            v = rb_ref[b, h]
            return jnp.where(bd == b, v, d), jnp.where(bp == b, v, p)

        z = jnp.zeros((QB, QB), F32)
        d, p = lax.fori_loop(0, N_BUCKETS, body, (z, z))
        far = rb_ref[N_BUCKETS - 1, h]
        rows = pl.ds(j * QB, QB)
        t = lax.broadcasted_iota(jnp.int32, (QB, QB), 0)
        c = lax.broadcasted_iota(jnp.int32, (QB, QB), 1)
        tswa_ref[0, g, rows, :] = jnp.where(t >= c, d * LOG2E, NEG_INF)
        tswa_ref[1, g, rows, :] = jnp.where(t + QB - c < WINDOW, p * LOG2E, NEG_INF)
        tdsa_ref[0, g, rows, :] = z
        tdsa_ref[1, g, rows, :] = (d - far) * LOG2E
        tdsa_ref[2, g, rows, :] = (p - far) * LOG2E


def _bias_tables(rel_bias):
    t = jnp.arange(QB)[:, None]
    c = jnp.arange(QB)[None, :]
    bd = _t5_bucket(t - c).astype(jnp.int32)
    bp = _t5_bucket(t + QB - c).astype(jnp.int32)
    rows = (N_BIAS_HEADS // B_KV_HEADS) * QB
    return pl.pallas_call(
        _bias_kernel,
        out_shape=(jax.ShapeDtypeStruct((2, B_KV_HEADS, rows, QB), F32),
                   jax.ShapeDtypeStruct((3, B_KV_HEADS, rows, QB), F32)),
        in_specs=[pl.BlockSpec(memory_space=pltpu.SMEM),
                  pl.BlockSpec(memory_space=pltpu.VMEM),
                  pl.BlockSpec(memory_space=pltpu.VMEM)],
        out_specs=(pl.BlockSpec(memory_space=pltpu.VMEM), pl.BlockSpec(memory_space=pltpu.VMEM)),
        compiler_params=pltpu.CompilerParams(vmem_limit_bytes=VMEM_LIMIT),
        name="t5_bias_tables",
    )(rel_bias.astype(F32), bd, bp)


def _swa_kernel(sink_ref, q_ref, kp_ref, kc_ref, vp_ref, vc_ref, t_ref, g_ref, o_ref):
    i = pl.program_id(1)
    n_per = C_HEADS // C_KV_HEADS
    d = C_HEAD_DIM
    kb = jnp.concatenate([kp_ref[...], kc_ref[...]], axis=0)
    vb = jnp.concatenate([vp_ref[...], vc_ref[...]], axis=0)
    ones = jnp.ones((2 * QB, d), BF16)
    no_prev = jnp.where(i == 0, NEG_INF, 0.0).astype(F32)
    low = lax.broadcasted_iota(jnp.int32, (QB, LANE), 1) < d
    outs = []
    for g in range(C_KV_HEADS):
        kg = kb[:, g * d:(g + 1) * d]
        vg = vb[:, g * d:(g + 1) * d]
        vaug = (jnp.concatenate([vg, ones, ones, vg], axis=1), jnp.concatenate([ones, vg, vg, ones], axis=1))
        zk = jnp.zeros_like(kg)
        kpar = (jnp.concatenate([kg, zk], axis=1), jnp.concatenate([zk, kg], axis=1))
        for jp in range(n_per // 2):
            pair = []
            q2 = q_ref[:, (g * n_per + 2 * jp) * d:(g * n_per + 2 * jp + 2) * d]
            for par in range(2):
                j = 2 * jp + par
                h = g * n_per + j
                rows = pl.ds(j * QB, QB)
                s = _dot_t(q2, kpar[par])
                s = s + jnp.concatenate([t_ref[1, g, rows, :] + no_prev, t_ref[0, g, rows, :]], axis=1)
                sink = sink_ref[0, h] * LOG2E
                m = jnp.maximum(jnp.broadcast_to(jnp.max(s, axis=1, keepdims=True), (QB, LANE)), sink)
                e = jnp.exp2(s - _rep2(m))
                acc = _dot(e.astype(BF16), vaug[par])
                pair.append(acc[:, :LANE] / (acc[:, LANE:] + jnp.exp2(sink - m)))
            outs.append(jnp.where(low, pair[0], pair[1]))
    o = jnp.concatenate(outs, axis=1) * _silu(g_ref[...].astype(F32))
    o_ref[...] = o.astype(o_ref.dtype)


def _swa_attention(g3, sinks, tswa, q_blk, k_blk, v_blk):
    b, s, _ = g3.shape
    wq = C_HEADS * C_HEAD_DIM
    wk = C_KV_HEADS * C_HEAD_DIM
    prev = lambda i: jnp.maximum(i - 1, 0)
    return pl.pallas_call(
        _swa_kernel,
        out_shape=jax.ShapeDtypeStruct((b, s, wq), BF16),
        grid=(b, s // QB),
        in_specs=[
            pl.BlockSpec(memory_space=pltpu.SMEM),
            pl.BlockSpec((None, QB, wq), lambda bb, i: (bb, i, q_blk)),
            pl.BlockSpec((None, QB, wk), lambda bb, i: (bb, prev(i), k_blk)),
            pl.BlockSpec((None, QB, wk), lambda bb, i: (bb, i, k_blk)),
            pl.BlockSpec((None, QB, wk), lambda bb, i: (bb, prev(i), v_blk)),
            pl.BlockSpec((None, QB, wk), lambda bb, i: (bb, i, v_blk)),
            pl.BlockSpec(tswa.shape, lambda bb, i: (0, 0, 0, 0)),
            pl.BlockSpec((None, QB, wq), lambda bb, i: (bb, i, 0)),
        ],
        out_specs=pl.BlockSpec((None, QB, wq), lambda bb, i: (bb, i, 0)),
        compiler_params=_cp("parallel", "parallel"),
        name="swa_attention",
    )(sinks.reshape(1, -1).astype(F32), g3, g3, g3, g3, g3, tswa, g3)


def _ordered_to_f32(key):
    bits = jnp.where(key >= 0, key, key ^ jnp.int32(0x7FFFFFFF))
    return lax.bitcast_convert_type(bits, F32)


def _dsa_kernel(q_ref, k_ref, v_ref, iq_ref, ik_ref, iw_ref, t_ref, g_ref, o_ref,
                sc, madd, iqs, ks, vaug, m_scr, acc_scr, *, s_len, k_top):
    i = pl.program_id(1)
    n_per = B_HEADS // B_KV_HEADS
    d = B_HEAD_DIM

    @pl.when(i == 0)
    def _():
        ones = jnp.ones((s_len, d), BF16)
        zeros = jnp.zeros((s_len, d), BF16)
        for g in range(B_KV_HEADS):
            kg = k_ref[:, g * d:(g + 1) * d]
            vg = v_ref[:, g * d:(g + 1) * d]
            ks[g, 0] = jnp.concatenate([kg, zeros], axis=1)
            ks[g, 1] = jnp.concatenate([zeros, kg], axis=1)
            vaug[g, 0] = jnp.concatenate([vg, ones], axis=1)
            vaug[g, 1] = jnp.concatenate([ones, vg], axis=1)

    half = lax.shift_right_logical(i, 1)
    odd = i & 1
    n_t = half + 1
    sub = 8
    key_i = lax.broadcasted_iota(jnp.int32, (KT, QB), 0)
    qry_i = i * QB + lax.broadcasted_iota(jnp.int32, (KT, QB), 1)

    for hp in range(IDX_HEADS // 2):
        for par in range(2):
            h = 2 * hp + par
            iqs[hp, par * QB:(par + 1) * QB, :] = iq_ref[:, h * IDX_DIM:(h + 1) * IDX_DIM]
    w = iw_ref[...] * (IDX_DIM ** -0.5 * IDX_HEADS ** -0.5)
    wrows = [w[h:h + 1, :] for h in range(IDX_HEADS)]

    def score_tile(kt, c):
        k0 = pl.multiple_of(kt * KT, KT)
        ikt = ik_ref[pl.ds(k0, KT), :]
        acc = jnp.zeros((KT, QB), F32)
        for hp in range(IDX_HEADS // 2):
            dots = _dot_t(ikt, iqs[hp])
            acc = acc + jnp.maximum(dots[:, :QB], 0.0) * wrows[2 * hp]
            acc = acc + jnp.maximum(dots[:, QB:], 0.0) * wrows[2 * hp + 1]
        sc[pl.ds(k0, KT), :] = jnp.where(qry_i >= k0 + key_i, acc, NEG_INF)
        return c

    lax.fori_loop(0, n_t, score_tile, 0)

    tail = (s_len - n_t * KT).astype(F32)
    int_min = jnp.int32(-2 ** 31)

    grp = (4, KT // (4 * sub), sub, QB)
    key_g = lax.broadcasted_iota(jnp.int32, grp, 0) * (KT // 4) + lax.broadcasted_iota(jnp.int32, grp, 1) * sub \
        + lax.broadcasted_iota(jnp.int32, grp, 2)

    def count(pred):
        def count_tile(kt, c):
            k0 = pl.multiple_of(kt * KT, KT)
            part = jnp.sum(jnp.where(pred(sc[pl.ds(k0, KT), :].reshape(grp), k0 + key_g), 1.0, 0.0), axis=1)
            return c + ((part[0] + part[1]) + (part[2] + part[3]))

        cnt = lax.fori_loop(0, n_t, count_tile, jnp.zeros((sub, QB), F32))
        return jnp.broadcast_to(jnp.sum(cnt, axis=0, keepdims=True), (sub, QB))

    def bisect(b, prefix):
        cand = prefix + lax.shift_left(jnp.int32(1), 31 - b)
        cand_f = _ordered_to_f32(cand)
        total = count(lambda t, _: t >= cand_f[None, None]) + jnp.where(cand_f <= NEG_INF, tail, 0.0)
        return jnp.where(total >= k_top, cand, prefix)

    thr8 = _ordered_to_f32(lax.fori_loop(0, 32, bisect, jnp.full((sub, QB), int_min, jnp.int32)))
    thr = thr8[:1]

    n_ge = count(lambda t, _: t >= thr8[None, None])
    tied = (n_ge > k_top) & (thr8 > NEG_INF)
    any_tied = jnp.max(jnp.where(tied, 1.0, 0.0)) > 0.0

    def write_mask(jlast):
        def mask_tile(kt, c):
            k0 = pl.multiple_of(kt * KT, KT)
            t = sc[pl.ds(k0, KT), :]
            sel = t >= thr if jlast is None else (t > thr) | ((t == thr) & (k0 + key_i <= jlast))
            madd[:, pl.ds(k0, KT)] = jnp.where(sel & (qry_i >= k0 + key_i), 0.0, NEG_INF).T
            return c

        lax.fori_loop(0, n_t, mask_tile, 0)

    write_mask(None)
    m_scr[...] = jnp.full(m_scr.shape, -jnp.inf, F32)
    acc_scr[...] = jnp.zeros(acc_scr.shape, F32)

    @pl.when(any_tied)
    def _():
        need = k_top - count(lambda t, _: t > thr8[None, None])
        n_bits = (s_len - 1).bit_length()

        def index_bisect(b, j):
            cand = j + lax.shift_left(jnp.int32(1), n_bits - 1 - b)
            below = count(lambda t, idx: (t == thr8[None, None]) & (idx < cand[None, None]))
            return jnp.where(below < need, cand, j)

        j = lax.fori_loop(0, n_bits, index_bisect, jnp.zeros((sub, QB), jnp.int32))
        write_mask(jnp.where(tied, j, s_len)[:1])

    near0 = jnp.where(odd == 1, half, jnp.maximum(half - 1, 0))

    def step(kt, near):
        k0 = pl.multiple_of(kt * KT, KT)
        mt = madd[:, pl.ds(k0, KT)]
        if near:
            li = jnp.where(odd == 1, 2, jnp.where(kt == half, 1, 0))
            ri = jnp.where(odd == 1, 1, jnp.where(kt == half, 0, 2))
        for g in range(B_KV_HEADS):
            kg = (ks[g, 0, pl.ds(k0, KT), :], ks[g, 1, pl.ds(k0, KT), :])
            vg = (vaug[g, 0, pl.ds(k0, KT), :], vaug[g, 1, pl.ds(k0, KT), :])
            for j in range(n_per):
                h = g * n_per + j
                q2 = q_ref[:, (h // 2) * LANE:(h // 2 + 1) * LANE]
                s = _dot_t(q2, kg[h % 2]) + mt
                if near:
                    rows = pl.ds(j * QB, QB)
                    s = s + jnp.concatenate([t_ref[li, g, rows, :], t_ref[ri, g, rows, :]], axis=1)
                m_prev = m_scr[h]
                m_new = jnp.maximum(m_prev, jnp.max(s, axis=1, keepdims=True))
                alpha = jnp.exp2(m_prev - m_new)
                p = jnp.exp2(s - _rep2(m_new))
                acc_scr[h] = alpha * acc_scr[h] + _dot(p.astype(BF16), vg[h % 2])
                m_scr[h] = m_new

    def far_body(kt, c):
        step(kt, False)
        return c

    def near_body(kt, c):
        step(kt, True)
        return c

    lax.fori_loop(0, near0, far_body, 0)
    lax.fori_loop(near0, n_t, near_body, 0)
    low = lax.broadcasted_iota(jnp.int32, (QB, LANE), 1) < d
    outs = []
    for hp in range(B_HEADS // 2):
        acc_e = acc_scr[2 * hp]
        acc_o = acc_scr[2 * hp + 1]
        num = jnp.where(low, acc_e, acc_o)
        den = pltpu.roll(jnp.where(low, acc_o, acc_e), d, 1)
        outs.append(num / den)
    o = jnp.concatenate(outs, axis=1) * _silu(g_ref[...].astype(F32))
    o_ref[...] = o.astype(o_ref.dtype)


def _dsa_attention(g3, ik3, iwt3, tdsa, q_blk, k_blk, v_blk, iq_blk):
    b, s, _ = g3.shape
    k_top = min(IDX_TOPK_MAX, s // 4)
    wq = B_HEADS * B_HEAD_DIM
    wk = B_KV_HEADS * B_HEAD_DIM
    wi = IDX_HEADS * IDX_DIM
    return pl.pallas_call(
        functools.partial(_dsa_kernel, s_len=s, k_top=k_top),
        out_shape=jax.ShapeDtypeStruct((b, s, wq), BF16),
        grid=(b, s // QB),
        in_specs=[
            pl.BlockSpec((None, QB, wq), lambda bb, i: (bb, i, q_blk)),
            pl.BlockSpec((None, s, wk), lambda bb, i: (bb, 0, k_blk)),
            pl.BlockSpec((None, s, wk), lambda bb, i: (bb, 0, v_blk)),
            pl.BlockSpec((None, QB, wi), lambda bb, i: (bb, i, iq_blk)),
            pl.BlockSpec((None, s, IDX_DIM), lambda bb, i: (bb, 0, 0)),
            pl.BlockSpec((None, IDX_HEADS, QB), lambda bb, i: (bb, 0, i)),
            pl.BlockSpec(tdsa.shape, lambda bb, i: (0, 0, 0, 0)),
            pl.BlockSpec((None, QB, wq), lambda bb, i: (bb, i, 0)),
        ],
        out_specs=pl.BlockSpec((None, QB, wq), lambda bb, i: (bb, i, 0)),
        scratch_shapes=[
            pltpu.VMEM((s, QB), F32),
            pltpu.VMEM((QB, s), F32),
            pltpu.VMEM((IDX_HEADS // 2, 2 * QB, IDX_DIM), BF16),
            pltpu.VMEM((B_KV_HEADS, 2, s, 2 * B_HEAD_DIM), BF16),
            pltpu.VMEM((B_KV_HEADS, 2, s, 2 * B_HEAD_DIM), BF16),
            pltpu.VMEM((B_HEADS, QB, LANE), F32),
            pltpu.VMEM((B_HEADS, QB, 2 * B_HEAD_DIM), F32),
        ],
        compiler_params=_cp("parallel", "arbitrary"),
        name="dsa_attention",
    )(g3, g3, g3, g3, ik3, iwt3, tdsa, g3)


def _mla_layer(h, w_in_t, q_norm, w_uq, kv_norm, w_ukv, rope1, rope2, b, s):
    o_ckv = A_Q_LORA + A_KV_LORA
    o_mq = o_ckv + A_ROPE
    o_gate = o_mq + MEM_WIDTH
    half = A_ROPE // 2
    w_f, w_g = _repack(w_in_t, (
        _pieces((0, o_ckv), (o_ckv, A_ROPE), (o_ckv + half, half), (o_ckv, half)),
        _pieces((o_gate, A_HEADS * A_V + MEM_WIDTH), (o_mq, MEM_WIDTH)),
    ), by_rows=True)
    cqkv = _matmul(h, w_f, F32, w_rows_are_outputs=True)
    g2 = _matmul(h, w_g, BF16, w_rows_are_outputs=True)

    hd = A_NOPE + A_ROPE
    (w_uq2,) = _repack(w_uq, (
        _pieces(*[(hh * hd, A_NOPE, hd ** -0.5 * LOG2E) for hh in range(A_HEADS)],
                *[(hh * hd + A_NOPE, A_ROPE, hd ** -0.5 * LOG2E) for hh in range(A_HEADS)]),
    ))
    qp = _uq_proj(cqkv, q_norm, w_uq2, rope2[0], rope2[1], s)
    kv, krp = _ukv_proj(cqkv, kv_norm, w_ukv, rope1[0], rope1[1], s)
    self_out = _mla_attention(qp.reshape(b, s, -1), kv.reshape(b, s, -1), krp.reshape(b, s, -1),
                              g2.reshape(b, s, -1), b, s)
    return self_out.reshape(b * s, -1), g2


def _dsa_layer(h, w_in_t, tdsa, b, s):
    wq = B_HEADS * B_HEAD_DIM
    wk = B_KV_HEADS * B_HEAD_DIM
    wi = IDX_HEADS * IDX_DIM
    o_k = wq
    o_v = o_k + wk
    o_iq = o_v + wk
    o_ik = o_iq + wi
    o_iw = o_ik + IDX_DIM
    o_mq = o_iw + IDX_HEADS
    o_gate = o_mq + MEM_WIDTH
    layout = _pieces((o_gate, wq + MEM_WIDTH), (o_mq, MEM_WIDTH), (0, wq, B_HEAD_DIM ** -0.5 * LOG2E),
                     (o_iq, wi), (o_k, wk), (o_v, wk))
    w_g, w_i = _repack(w_in_t, (layout, _pieces((o_ik, IDX_DIM + IDX_HEADS))), by_rows=True)
    g2 = _matmul(h, w_g, BF16, w_rows_are_outputs=True)
    ii = _matmul(h, w_i, F32, w_rows_are_outputs=True)
    ik3 = ii[:, :IDX_DIM].astype(BF16).reshape(b, s, IDX_DIM)
    iwt3 = ii[:, IDX_DIM:].reshape(b, s, IDX_HEADS).transpose(0, 2, 1)
    _, _, q_at, iq_at, k_at, v_at = [p[2] for p in layout]
    self_out = _dsa_attention(g2.reshape(b, s, -1), ik3, iwt3, tdsa,
                              q_blk=q_at // wq, k_blk=k_at // wk, v_blk=v_at // wk, iq_blk=iq_at // wi)
    return self_out.reshape(b * s, -1), g2


def _swa_layer(h, w_in, sinks, tswa, b, s):
    wq = C_HEADS * C_HEAD_DIM
    wk = C_KV_HEADS * C_HEAD_DIM
    o_k = wq
    o_v = o_k + wk
    o_mq = o_v + wk
    o_gate = o_mq + MEM_WIDTH
    layout = _pieces((o_gate, wq + MEM_WIDTH), (o_mq, MEM_WIDTH), (0, wq, C_HEAD_DIM ** -0.5 * LOG2E),
                     (o_k, wk), (o_v, wk))
    (w_g,) = _repack(w_in, (layout,))
    g2 = _matmul(h, w_g, BF16)
    _, _, q_at, k_at, v_at = [p[2] for p in layout]
    self_out = _swa_attention(g2.reshape(b, s, -1), sinks, tswa,
                              q_blk=q_at // wq, k_blk=k_at // wk, v_blk=v_at // wk)
    return self_out.reshape(b * s, -1), g2


def kernel(x, mem, norm_in, final_norm, mem_norm, rel_bias, w_in_a, a_q_norm, w_uq, a_kv_norm, w_ukv,
           w_in_b, w_in_c, c_sinks, w_mem_kv, w_out):
    b, s, d = x.shape
    depth = norm_in.shape[0]
    ml = mem.shape[1]
    xf = x.reshape(b * s, d)
    mem_n = _rmsnorm(mem.reshape(b * ml, d), mem_norm, BF16)
    rope1, rope2 = _rope_tables(s)
    tswa, tdsa = _bias_tables(rel_bias)
    h = _rmsnorm(xf, norm_in[0], BF16)
    w_ukv_b, w_mem_b, w_out_b = w_ukv.astype(BF16), w_mem_kv.astype(BF16), w_out.astype(BF16)
    w_in_a_t, w_in_b_t = jnp.swapaxes(w_in_a, 1, 2), jnp.swapaxes(w_in_b, 1, 2)
    for i in range(depth):
        kind, j = i % N_MIXERS, i // N_MIXERS
        if kind == 0:
            ys, g2 = _mla_layer(h, (w_in_a_t, j), a_q_norm[j], (w_uq, j), a_kv_norm[j], (w_ukv_b, j),
                                rope1, rope2, b, s)
        elif kind == 1:
            ys, g2 = _dsa_layer(h, (w_in_b_t, j), tdsa, b, s)
        else:
            ys, g2 = _swa_layer(h, (w_in_c, j), c_sinks[j], tswa, b, s)
        memkv = _matmul(mem_n, (w_mem_b, i), BF16)
        self_w = w_out.shape[1] - MEM_WIDTH
        ym = _mem_attention(g2.reshape(b, s, -1), (self_w + MEM_WIDTH) // MEM_WIDTH, self_w // MEM_WIDTH,
                            memkv.reshape(b, ml, -1)).reshape(b * s, -1)
        last = i == depth - 1
        res = _out_proj(ys, ym, xf, (w_out_b, i), final_norm if last else norm_in[i + 1], not last)
        if last:
            return res[0].reshape(b, s, d)
        xf, h = res
```
